```python
import math
import jax
import jax.numpy as jnp
from jax import lax
import numpy as np

D_MODEL = 1024
BATCH = 8
SEQ = 2048
DEPTH = 2
DEC_BATCH = 128
DEC_SEQ = 1
PAST_LEN = 16384
PAGE_SIZE = 128

F32 = jnp.float32
EPS = 1e-6
CHUNK = 128
N_EVEN = (DEPTH + 1) // 2
N_ODD = DEPTH // 2
D_FF = 2816
SSD_D_INNER = 2 * D_MODEL
SSD_HEAD_DIM = 64
SSD_HEADS = SSD_D_INNER // SSD_HEAD_DIM
SSD_GROUPS = 4
SSD_D_STATE = 128
SSD_CONV = 4
SSD_CONV_CH = SSD_D_INNER + 2 * SSD_GROUPS * SSD_D_STATE
RET_HEADS = 4
RET_QK = D_MODEL
RET_V = 2 * D_MODEL
RET_DK = RET_QK // RET_HEADS
RET_DV = RET_V // RET_HEADS
ROPE_BASE = 10000.0
AB_SIZES = (SSD_D_INNER, SSD_CONV_CH, SSD_HEADS, RET_QK, RET_QK, RET_V, RET_V)
IN_AB = sum(AB_SIZES)
M_HEADS = 4
M_QK = D_MODEL // 2
M_V = D_MODEL
M_DK = M_QK // M_HEADS
M_DV = M_V // M_HEADS
C_SIZES = (M_QK, M_QK, M_V, M_HEADS, M_HEADS, M_V)
IN_C = sum(C_SIZES)
MEM_LEN = 256
X_HEADS = 4
X_HEAD_DIM = D_MODEL // X_HEADS

kernel_name = 'hybrid_ssd_retention_mlstm_macaron_step'


def _split(a, sizes):
    return jnp.split(a, np.cumsum(sizes)[:-1].tolist(), axis=-1)


def _to_chunks(a, c):
    b, l = a.shape[:2]
    return jnp.moveaxis(a.reshape(b, l // c, c, *a.shape[2:]), 1, 0)


def _from_chunks(a):
    n, b, c = a.shape[:3]
    return jnp.moveaxis(a, 0, 1).reshape(b, n * c, *a.shape[3:])


def group_rms(x, groups):
    shp = x.shape
    xf = x.astype(F32).reshape(*shp[:-1], groups, shp[-1] // groups)
    xf = xf * lax.rsqrt(jnp.mean(xf * xf, axis=-1, keepdims=True) + EPS)
    return xf.reshape(shp).astype(x.dtype)


def rmsnorm(x, g):
    return group_rms(x, 1) * g


def swiglu(x, w_in, w_out):
    g, u = jnp.split(x @ w_in, 2, axis=-1)
    return (jax.nn.silu(g) * u) @ w_out


def rotary(x, pos):
    half = x.shape[-1] // 2
    inv = jnp.exp(-math.log(ROPE_BASE) * jnp.arange(half, dtype=F32) / half)
    ang = pos.astype(F32)[:, None] * inv
    cos = jnp.cos(ang)[None, :, None, :]
    sin = jnp.sin(ang)[None, :, None, :]
    xf = x.astype(F32)
    x1, x2 = xf[..., :half], xf[..., half:]
    return jnp.concatenate([x1 * cos - x2 * sin, x1 * sin + x2 * cos], axis=-1).astype(x.dtype)


def causal_conv(u, buf, w, b):
    l = u.shape[1]
    full = jnp.concatenate([buf.astype(u.dtype), u], axis=1)
    out = b + sum(full[:, j:j + l] * w[j] for j in range(SSD_CONV))
    return out, full[:, l:]


def ssd_scan(x, dt, a, bm, cm, h0):
    bsz, l, nh, hp = x.shape
    g = SSD_GROUPS
    r = nh // g
    ns = bm.shape[-1]
    c = math.gcd(l, CHUNK)
    dt = dt.astype(F32)
    da = (dt * a.astype(F32)).reshape(bsz, l, g, r)
    xdt = x.astype(F32).reshape(bsz, l, g, r, hp) * dt.reshape(bsz, l, g, r)[..., None]
    mask = jnp.tril(jnp.ones((c, c), bool))[None, :, :, None, None]

    def step(h, inp):
        xc, dac, bc, cc = inp
        cs = jnp.cumsum(dac, axis=1)
        seg = cs[:, :, None] - cs[:, None, :]
        decay = jnp.exp(jnp.where(mask, seg, -jnp.inf))
        att = jnp.einsum('btgn,bsgn->btsg', cc, bc)[..., None] * decay
        y = jnp.einsum('btsgr,bsgrp->btgrp', att, xc)
        y = y + jnp.einsum('btgn,bgrpn->btgrp', cc, h) * jnp.exp(cs)[..., None]
        last = cs[:, -1]
        w = jnp.exp(last[:, None] - cs)
        h = h * jnp.exp(last)[..., None, None] + jnp.einsum('bsgr,bsgrp,bsgn->bgrpn', w, xc, bc)
        return h, y

    h_init = h0.astype(F32).reshape(bsz, g, r, hp, ns)
    hT, ys = lax.scan(step, h_init, (_to_chunks(xdt, c), _to_chunks(da, c),
                                     _to_chunks(bm.astype(F32), c), _to_chunks(cm.astype(F32), c)))
    y = _from_chunks(ys).reshape(bsz, l, nh, hp)
    return y.astype(x.dtype), hT.reshape(bsz, nh, hp, ns).astype(h0.dtype)


def retention_scan(q, k, v, s0):
    bsz, l, nh, dk = q.shape
    c = math.gcd(l, CHUNK)
    lg = jnp.log1p(-jnp.exp2(-5.0 - jnp.arange(nh, dtype=F32)))
    idx = jnp.arange(c, dtype=F32)
    diff = idx[:, None] - idx[None, :]
    decay = jnp.exp(jnp.where((diff >= 0)[..., None], diff[..., None] * lg, -jnp.inf))
    inner = jnp.exp((idx[:, None] + 1.0) * lg)
    tail = jnp.exp((c - 1.0 - idx)[:, None] * lg)
    carry_decay = jnp.exp(c * lg)

    def step(s, inp):
        qc, kc, vc = inp
        att = jnp.einsum('bthd,bshd->btsh', qc, kc) * decay
        y = jnp.einsum('btsh,bshe->bthe', att, vc) + jnp.einsum('bthd,bhde->bthe', qc, s) * inner[:, :, None]
        s = s * carry_decay[:, None, None] + jnp.einsum('bshd,sh,bshe->bhde', kc, tail, vc)
        return s, y

    sT, ys = lax.scan(step, s0.astype(F32), (_to_chunks(q.astype(F32), c), _to_chunks(k.astype(F32), c),
                                            _to_chunks(v.astype(F32), c)))
    return _from_chunks(ys).astype(v.dtype), sT.astype(s0.dtype)


def mlstm_scan(q, k, v, i_pre, f_pre, c0, n0, m0):
    bsz, l, nh, dk = q.shape
    c = math.gcd(l, CHUNK)
    mask = jnp.tril(jnp.ones((c, c), bool))[None, :, :, None]
    logf = jax.nn.log_sigmoid(f_pre.astype(F32))

    def step(carry, inp):
        cmat, nvec, m = carry
        qc, kc, vc, ic, lfc = inp
        b = jnp.cumsum(lfc, axis=1)
        g = ic - b
        mt = b + jnp.maximum(m[:, None], lax.cummax(g, axis=1))
        w = jnp.exp(jnp.where(mask, g[:, None] + (b - mt)[:, :, None], -jnp.inf))
        a = jnp.einsum('bthd,bshd->btsh', qc, kc) * w
        inter = jnp.exp(b + m[:, None] - mt)
        num = jnp.einsum('btsh,bshe->bthe', a, vc) + jnp.einsum('bthd,bhde->bthe', qc, cmat) * inter[..., None]
        den = a.sum(axis=2) + jnp.einsum('bthd,bhd->bth', qc, nvec) * inter
        hc = num / jnp.maximum(jnp.abs(den), jnp.exp(-mt))[..., None]
        m_new = mt[:, -1]
        wl = jnp.exp(g + (b[:, -1] - m_new)[:, None])
        dp = jnp.exp(b[:, -1] + m - m_new)
        cmat = cmat * dp[..., None, None] + jnp.einsum('bsh,bshd,bshe->bhde', wl, kc, vc)
        nvec = nvec * dp[..., None] + jnp.einsum('bsh,bshd->bhd', wl, kc)
        return (cmat, nvec, m_new), hc

    init = (c0.astype(F32), n0.astype(F32), m0.astype(F32))
    (cT, nT, mT), hs = lax.scan(step, init, (_to_chunks(q.astype(F32), c), _to_chunks(k.astype(F32), c),
                                             _to_chunks(v.astype(F32), c), _to_chunks(i_pre.astype(F32), c),
                                             _to_chunks(logf, c)))
    return (_from_chunks(hs).astype(v.dtype), cT.astype(c0.dtype), nT.astype(n0.dtype), mT.astype(m0.dtype))


def mixer_ab(h, pos, conv_buf, ssm_h, ret_s, w_in, conv_w, conv_b, dt_bias, a_log, d_skip, ssd_norm, w_out):
    bsz, l, _ = h.shape
    z, xbc, dt_raw, rq, rk, rv, rg = _split(h @ w_in, AB_SIZES)
    xbc, conv_buf = causal_conv(xbc, conv_buf, conv_w, conv_b)
    xbc = jax.nn.silu(xbc)
    xs, bm, cm = _split(xbc, (SSD_D_INNER, SSD_GROUPS * SSD_D_STATE, SSD_GROUPS * SSD_D_STATE))
    xs = xs.reshape(bsz, l, SSD_HEADS, SSD_HEAD_DIM)
    dt = jax.nn.softplus((dt_raw + dt_bias).astype(F32))
    a = -jnp.exp(a_log.astype(F32))
    y, ssm_h = ssd_scan(xs, dt, a, bm.reshape(bsz, l, SSD_GROUPS, SSD_D_STATE),
                        cm.reshape(bsz, l, SSD_GROUPS, SSD_D_STATE), ssm_h)
    y = (y + xs * d_skip[:, None]).reshape(bsz, l, SSD_D_INNER) * jax.nn.silu(z)
    y = group_rms(y, SSD_GROUPS) * ssd_norm
    q = rotary(rq.reshape(bsz, l, RET_HEADS, RET_DK), pos)
    k = rotary(rk.reshape(bsz, l, RET_HEADS, RET_DK), pos) * (RET_DK ** -0.5)
    v = rv.reshape(bsz, l, RET_HEADS, RET_DV)
    r, ret_s = retention_scan(q, k, v, ret_s)
    r = jax.nn.silu(rg) * group_rms(r.reshape(bsz, l, RET_V), RET_HEADS)
    return jnp.concatenate([y, r], axis=-1) @ w_out, conv_buf, ssm_h, ret_s


def mixer_c(h, c0, n0, m0, w_in, i_bias, f_bias, norm_g, w_out):
    bsz, l, _ = h.shape
    q, k, v, ig, fg, o = _split(h @ w_in, C_SIZES)
    q = q.reshape(bsz, l, M_HEADS, M_DK)
    k = k.reshape(bsz, l, M_HEADS, M_DK) * (M_DK ** -0.5)
    v = v.reshape(bsz, l, M_HEADS, M_DV)
    hh, c0, n0, m0 = mlstm_scan(q, k, v, ig + i_bias, fg + f_bias, c0, n0, m0)
    hh = group_rms(hh.reshape(bsz, l, M_V), M_HEADS) * norm_g
    return (jax.nn.sigmoid(o) * hh) @ w_out, c0, n0, m0


def mem_kv(mem, norm_g, w_kv):
    bsz = mem.shape[0]
    k, v = jnp.split(rmsnorm(mem, norm_g) @ w_kv, 2, axis=-1)
    return (k.reshape(bsz, MEM_LEN, X_HEADS, X_HEAD_DIM), v.reshape(bsz, MEM_LEN, X_HEADS, X_HEAD_DIM))


def cross_attn(h, k, v, w_q, w_o):
    bsz, l, _ = h.shape
    q = (h @ w_q).reshape(bsz, l, X_HEADS, X_HEAD_DIM)
    s = jnp.einsum('blhd,bmhd->bhlm', q, k.astype(q.dtype)).astype(F32) * (X_HEAD_DIM ** -0.5)
    p = jax.nn.softmax(s, axis=-1).astype(h.dtype)
    o = jnp.einsum('bhlm,bmhd->blhd', p, v.astype(h.dtype)).reshape(bsz, l, D_MODEL)
    return o @ w_o


def trunk(x, pos, mem_k, mem_v, conv_buf, ssm_h, ret_s, mc, mn, mm, p):
    conv_new, ssm_new, ret_new, mc_new, mn_new, mm_new = [], [], [], [], [], []
    for layer in range(DEPTH):
        x = x + 0.5 * swiglu(rmsnorm(x, p['norm_ffn1'][layer]), p['w_ffn1_in'][layer], p['w_ffn1_out'][layer])
        hn = rmsnorm(x, p['norm_mix'][layer])
        e = layer // 2
        if layer % 2 == 0:
            out, cb, sh, rs = mixer_ab(hn, pos, conv_buf[e], ssm_h[e], ret_s[e], p['w_in_ab'][e],
                                       p['ssd_conv_w'][e], p['ssd_conv_b'][e], p['ssd_dt_bias'][e],
                                       p['ssd_a_log'][e], p['ssd_d'][e], p['ssd_norm'][e], p['w_out_ab'][e])
            conv_new.append(cb)
            ssm_new.append(sh)
            ret_new.append(rs)
        else:
            out, c1, n1, m1 = mixer_c(hn, mc[e], mn[e], mm[e], p['w_in_c'][e], p['mlstm_i_bias'][e],
                                      p['mlstm_f_bias'][e], p['mlstm_norm'][e], p['w_out_c'][e])
            mc_new.append(c1)
            mn_new.append(n1)
            mm_new.append(m1)
        x = x + out
        x = x + cross_attn(rmsnorm(x, p['norm_xattn'][layer]), mem_k[layer], mem_v[layer],
                           p['w_xq'][layer], p['w_xo'][layer])
        x = x + 0.5 * swiglu(rmsnorm(x, p['norm_ffn2'][layer]), p['w_ffn2_in'][layer], p['w_ffn2_out'][layer])
    return (rmsnorm(x, p['norm_final']), jnp.stack(conv_new), jnp.stack(ssm_new), jnp.stack(ret_new),
            jnp.stack(mc_new), jnp.stack(mn_new), jnp.stack(mm_new))


def setup_inputs(seed: int = 0) -> dict:
    key = jax.random.key(seed)
    keys = iter(jax.random.split(key, 48))

    def nrm(shape, scale):
        return jax.random.normal(next(keys), shape, F32) * scale

    def gain(shape):
        return 1.0 + nrm(shape, 0.02)

    dt0 = jnp.exp(jax.random.uniform(next(keys), (N_EVEN, SSD_HEADS), F32,
                                     minval=math.log(1e-3), maxval=math.log(1e-1)))
    return {
        'x_prompt': nrm((BATCH, SEQ, D_MODEL), 1.0),
        'x_sample': nrm((DEC_BATCH, DEC_SEQ, D_MODEL), 1.0),
        'cache_mem_k': nrm((DEPTH, DEC_BATCH, MEM_LEN, X_HEADS, X_HEAD_DIM), 1.0),
        'cache_mem_v': nrm((DEPTH, DEC_BATCH, MEM_LEN, X_HEADS, X_HEAD_DIM), 1.0),
        'state_conv': nrm((N_EVEN, DEC_BATCH, SSD_CONV - 1, SSD_CONV_CH), 1.0),
        'state_ssm': nrm((N_EVEN, DEC_BATCH, SSD_HEADS, SSD_HEAD_DIM, SSD_D_STATE), 0.3),
        'state_ret': nrm((N_EVEN, DEC_BATCH, RET_HEADS, RET_DK, RET_DV), 1.0),
        'state_mlstm_c': nrm((N_ODD, DEC_BATCH, M_HEADS, M_DK, M_DV), 0.3),
        'state_mlstm_n': nrm((N_ODD, DEC_BATCH, M_HEADS, M_DK), 0.3),
        'state_mlstm_m': nrm((N_ODD, DEC_BATCH, M_HEADS), 1.0),
        'mem_prompt': nrm((BATCH, MEM_LEN, D_MODEL), 1.0),
        'norm_ffn1': gain((DEPTH, D_MODEL)),
        'w_ffn1_in': nrm((DEPTH, D_MODEL, 2 * D_FF), D_MODEL ** -0.5),
        'w_ffn1_out': nrm((DEPTH, D_FF, D_MODEL), D_FF ** -0.5),
        'norm_mix': gain((DEPTH, D_MODEL)),
        'w_in_ab': nrm((N_EVEN, D_MODEL, IN_AB), D_MODEL ** -0.5),
        'ssd_conv_w': nrm((N_EVEN, SSD_CONV, SSD_CONV_CH), SSD_CONV ** -0.5),
        'ssd_conv_b': nrm((N_EVEN, SSD_CONV_CH), 0.02),
        'ssd_dt_bias': dt0 + jnp.log(-jnp.expm1(-dt0)),
        'ssd_a_log': jnp.log(jax.random.uniform(next(keys), (N_EVEN, SSD_HEADS), F32, minval=1.0, maxval=16.0)),
        'ssd_d': 1.0 + nrm((N_EVEN, SSD_HEADS), 0.1),
        'ssd_norm': gain((N_EVEN, SSD_D_INNER)),
        'w_out_ab': nrm((N_EVEN, SSD_D_INNER + RET_V, D_MODEL), (SSD_D_INNER + RET_V) ** -0.5),
        'w_in_c': nrm((N_ODD, D_MODEL, IN_C), D_MODEL ** -0.5),
        'mlstm_i_bias': nrm((N_ODD, M_HEADS), 0.5),
        'mlstm_f_bias': jnp.linspace(3.0, 6.0, M_HEADS, dtype=F32)[None] + nrm((N_ODD, M_HEADS), 0.1),
        'mlstm_norm': gain((N_ODD, M_V)),
        'w_out_c': nrm((N_ODD, M_V, D_MODEL), M_V ** -0.5),
        'norm_xattn': gain((DEPTH, D_MODEL)),
        'norm_mem': gain((DEPTH, D_MODEL)),
        'w_xq': nrm((DEPTH, D_MODEL, D_MODEL), D_MODEL ** -0.5),
        'w_xkv': nrm((DEPTH, D_MODEL, 2 * D_MODEL), D_MODEL ** -0.5),
        'w_xo': nrm((DEPTH, D_MODEL, D_MODEL), D_MODEL ** -0.5),
        'norm_ffn2': gain((DEPTH, D_MODEL)),
        'w_ffn2_in': nrm((DEPTH, D_MODEL, 2 * D_FF), D_MODEL ** -0.5),
        'w_ffn2_out': nrm((DEPTH, D_FF, D_MODEL), D_FF ** -0.5),
        'norm_final': gain((D_MODEL,)),
    }


def reference(x_prompt, x_sample, cache_mem_k, cache_mem_v, state_conv, state_ssm, state_ret,
              state_mlstm_c, state_mlstm_n, state_mlstm_m, mem_prompt,
              norm_ffn1, w_ffn1_in, w_ffn1_out, norm_mix, w_in_ab, ssd_conv_w, ssd_conv_b,
              ssd_dt_bias, ssd_a_log, ssd_d, ssd_norm, w_out_ab, w_in_c, mlstm_i_bias, mlstm_f_bias,
              mlstm_norm, w_out_c, norm_xattn, norm_mem, w_xq, w_xkv, w_xo, norm_ffn2, w_ffn2_in,
              w_ffn2_out, norm_final):
    p = dict(norm_ffn1=norm_ffn1, w_ffn1_in=w_ffn1_in, w_ffn1_out=w_ffn1_out, norm_mix=norm_mix,
             w_in_ab=w_in_ab, ssd_conv_w=ssd_conv_w, ssd_conv_b=ssd_conv_b, ssd_dt_bias=ssd_dt_bias,
             ssd_a_log=ssd_a_log, ssd_d=ssd_d, ssd_norm=ssd_norm, w_out_ab=w_out_ab, w_in_c=w_in_c,
             mlstm_i_bias=mlstm_i_bias, mlstm_f_bias=mlstm_f_bias, mlstm_norm=mlstm_norm, w_out_c=w_out_c,
             norm_xattn=norm_xattn, w_xq=w_xq, w_xo=w_xo, norm_ffn2=norm_ffn2, w_ffn2_in=w_ffn2_in,
             w_ffn2_out=w_ffn2_out, norm_final=norm_final)
    dt = x_prompt.dtype
    kv = [mem_kv(mem_prompt, norm_mem[layer], w_xkv[layer]) for layer in range(DEPTH)]
    mem_k_prompt = jnp.stack([pair[0] for pair in kv])
    mem_v_prompt = jnp.stack([pair[1] for pair in kv])
    (y_prompt, conv_p, ssm_p, ret_p, mc_p, mn_p, mm_p) = trunk(
        x_prompt, jnp.arange(SEQ), mem_k_prompt, mem_v_prompt,
        jnp.zeros((N_EVEN, BATCH, SSD_CONV - 1, SSD_CONV_CH), dt),
        jnp.zeros((N_EVEN, BATCH, SSD_HEADS, SSD_HEAD_DIM, SSD_D_STATE), dt),
        jnp.zeros((N_EVEN, BATCH, RET_HEADS, RET_DK, RET_DV), dt),
        jnp.zeros((N_ODD, BATCH, M_HEADS, M_DK, M_DV), dt),
        jnp.zeros((N_ODD, BATCH, M_HEADS, M_DK), dt),
        jnp.zeros((N_ODD, BATCH, M_HEADS), dt), p)
    (y_sample, conv_s, ssm_s, ret_s, mc_s, mn_s, mm_s) = trunk(
        x_sample, PAST_LEN + jnp.arange(DEC_SEQ), cache_mem_k, cache_mem_v,
        state_conv, state_ssm, state_ret, state_mlstm_c, state_mlstm_n, state_mlstm_m, p)
    return (y_prompt, y_sample, mem_k_prompt, mem_v_prompt, conv_p, conv_s, ssm_p, ssm_s,
            ret_p, ret_s, mc_p, mc_s, mn_p, mn_s, mm_p, mm_s)
```

```python
import functools
import math

import jax
import jax.numpy as jnp
from jax import lax
from jax.experimental import pallas as pl
from jax.experimental.pallas import tpu as pltpu

F32 = jnp.float32
BF16 = jnp.bfloat16
EPS = 1e-6

D_MODEL = 1024
BATCH = 8
SEQ = 2048
DEPTH = 2
DEC_BATCH = 128
PAST_LEN = 16384
CHUNK = 128
D_FF = 2816
SSD_D_INNER = 2 * D_MODEL
SSD_HEAD_DIM = 64
SSD_HEADS = SSD_D_INNER // SSD_HEAD_DIM
SSD_GROUPS = 4
SSD_D_STATE = 128
SSD_CONV = 4
SSD_CONV_CH = SSD_D_INNER + 2 * SSD_GROUPS * SSD_D_STATE
RET_HEADS = 4
RET_QK = D_MODEL
RET_V = 2 * D_MODEL
RET_DK = RET_QK // RET_HEADS
RET_DV = RET_V // RET_HEADS
ROPE_BASE = 10000.0
AB_SIZES = (SSD_D_INNER, SSD_CONV_CH, SSD_HEADS, RET_QK, RET_QK, RET_V, RET_V)
M_HEADS = 4
M_QK = D_MODEL // 2
M_V = D_MODEL
M_DK = M_QK // M_HEADS
M_DV = M_V // M_HEADS
C_SIZES = (M_QK, M_QK, M_V, M_HEADS, M_HEADS, M_V)
MEM_LEN = 256
X_HEADS = 4
X_HEAD_DIM = D_MODEL // X_HEADS

LANES = 128
SUBLANES = 8
GROUP_W = SSD_D_INNER // SSD_GROUPS
HEADS_PER_GROUP = SSD_HEADS // SSD_GROUPS
ROW_TILE = 1024
FF_TILE = 256
SB = 8


def _cp(sem, mib):
    return pltpu.CompilerParams(dimension_semantics=sem, vmem_limit_bytes=mib * 1024 * 1024)


def _bf(x):
    return x.astype(BF16)


def _dot(a, b):
    return jnp.dot(a, b, preferred_element_type=F32)


def _dot_nt(a, b):
    return lax.dot_general(a, b, (((1,), (1,)), ((), ())), preferred_element_type=F32)


def _dot_tn(a, b):
    return lax.dot_general(a, b, (((0,), (0,)), ((), ())), preferred_element_type=F32)


def _rms(x):
    return x * lax.rsqrt(jnp.mean(x * x, axis=-1, keepdims=True) + EPS)


def _silu(x):
    return x * jax.nn.sigmoid(x)


def _softplus(x):
    return jnp.maximum(x, 0.0) + jnp.log1p(jnp.exp(-jnp.abs(x)))


def _split3(x):
    hi = x.astype(BF16)
    r = x - hi.astype(F32)
    mid = r.astype(BF16)
    lo = (r - mid.astype(F32)).astype(BF16)
    return hi, mid, lo


def _cumsum_rows(x):
    n = x.shape[0]
    r = lax.broadcasted_iota(jnp.int32, (n, n), 0)
    c = lax.broadcasted_iota(jnp.int32, (n, n), 1)
    t = jnp.where(r >= c, 1.0, 0.0).astype(BF16)
    hi, mid, lo = _split3(x)
    return _dot(t, hi) + _dot(t, mid) + _dot(t, lo)


def _lane_bcast(x, h, width=LANES):
    return jnp.broadcast_to(x[:, h:h + 1], (x.shape[0], width))


def _pair_expand(x, n_heads):
    rows = x.shape[0]
    lo = lax.broadcasted_iota(jnp.int32, (rows, LANES), 1) < SSD_HEAD_DIM
    return jnp.concatenate(
        [jnp.where(lo, _lane_bcast(x, 2 * j), _lane_bcast(x, 2 * j + 1)) for j in range(n_heads // 2)], axis=1)


def _pad_t(x):
    pad = jnp.zeros((LANES - x.shape[0], x.shape[1]), F32)
    return jnp.concatenate([x, pad], axis=0).T


def _ffn_body(*refs, nf, final):
    if final:
        x_ref, g_ref, wg_ref, wu_ref, wo_ref, fg_ref, o_ref, xn_ref, acc_ref = refs
    else:
        x_ref, g_ref, wg_ref, wu_ref, wo_ref, o_ref, xn_ref, acc_ref = refs
    f = pl.program_id(1)

    @pl.when(f == 0)
    def _():
        xn_ref[...] = _bf(_rms(x_ref[...]) * g_ref[...])
        acc_ref[...] = jnp.zeros_like(acc_ref)

    xn = xn_ref[...]
    g = _dot(xn, _bf(wg_ref[...]))
    u = _dot(xn, _bf(wu_ref[...]))
    acc_ref[...] += _dot(_bf(_silu(g) * u), _bf(wo_ref[...]))

    @pl.when(f == nf - 1)
    def _():
        y = x_ref[...] + 0.5 * acc_ref[...]
        if final:
            y = _rms(y) * fg_ref[...]
        o_ref[...] = y


def _ffn(x, g3, w_in, w_out, layer, final_g=None):
    m = x.shape[0]
    tm = min(m, ROW_TILE)
    nf = D_FF // FF_TILE
    in_specs = [
        pl.BlockSpec((tm, D_MODEL), lambda i, f: (i, 0)),
        pl.BlockSpec((None, 1, D_MODEL), lambda i, f: (layer, 0, 0)),
        pl.BlockSpec((None, D_MODEL, FF_TILE), lambda i, f: (layer, 0, f)),
        pl.BlockSpec((None, D_MODEL, FF_TILE), lambda i, f: (layer, 0, f + nf)),
        pl.BlockSpec((None, FF_TILE, D_MODEL), lambda i, f: (layer, f, 0)),
    ]
    args = [x, g3, w_in, w_in, w_out]
    if final_g is not None:
        in_specs.append(pl.BlockSpec((1, D_MODEL), lambda i, f: (0, 0)))
        args.append(final_g.reshape(1, D_MODEL))
    return pl.pallas_call(
        functools.partial(_ffn_body, nf=nf, final=final_g is not None),
        grid=(m // tm, nf),
        in_specs=in_specs,
        out_specs=pl.BlockSpec((tm, D_MODEL), lambda i, f: (i, 0)),
        out_shape=jax.ShapeDtypeStruct((m, D_MODEL), F32),
        scratch_shapes=[pltpu.VMEM((tm, D_MODEL), BF16), pltpu.VMEM((tm, D_MODEL), F32)],
        compiler_params=_cp(("parallel", "arbitrary"), 48),
        name="ffn",
    )(*args)


def _norm_proj_body(*refs, small):
    if small:
        x_ref, g_ref, w_ref, ws_ref, o_ref, os_ref, xn_ref = refs
    else:
        x_ref, g_ref, w_ref, o_ref, xn_ref = refs
    n = pl.program_id(1)

    @pl.when(n == 0)
    def _():
        xn = _bf(_rms(x_ref[...]) * g_ref[...])
        xn_ref[...] = xn
        if small:
            os_ref[...] = _dot(xn, _bf(ws_ref[...]))

    o_ref[...] = _dot(xn_ref[...], _bf(w_ref[...]))


def _norm_proj(x, g3, glayer, w3, wlayer, tn, w_small=None):
    m = x.shape[0]
    tm = min(m, ROW_TILE)
    n_out = w3.shape[-1]
    in_specs = [
        pl.BlockSpec((tm, D_MODEL), lambda i, n: (i, 0)),
        pl.BlockSpec((None, 1, D_MODEL), lambda i, n: (glayer, 0, 0)),
        pl.BlockSpec((None, D_MODEL, tn), lambda i, n: (wlayer, 0, n)),
    ]
    args = [x, g3, w3]
    out_specs = [pl.BlockSpec((tm, tn), lambda i, n: (i, n))]
    out_shape = [jax.ShapeDtypeStruct((m, n_out), F32)]
    if w_small is not None:
        ns = w_small.shape[-1]
        in_specs.append(pl.BlockSpec((None, D_MODEL, ns), lambda i, n: (0, 0, 0)))
        args.append(w_small)
        out_specs.append(pl.BlockSpec((tm, ns), lambda i, n: (i, 0)))
        out_shape.append(jax.ShapeDtypeStruct((m, ns), F32))
    res = pl.pallas_call(
        functools.partial(_norm_proj_body, small=w_small is not None),
        grid=(m // tm, n_out // tn),
        in_specs=in_specs,
        out_specs=out_specs,
        out_shape=out_shape,
        scratch_shapes=[pltpu.VMEM((tm, D_MODEL), BF16)],
        compiler_params=_cp(("parallel", "arbitrary"), 48),
        name="norm_proj",
    )(*args)
    return res if w_small is not None else res[0]


def _proj_res_body(x_ref, y_ref, w_ref, o_ref):
    k = pl.program_id(1)

    @pl.when(k == 0)
    def _():
        o_ref[...] = x_ref[...]

    o_ref[...] += _dot(_bf(y_ref[...]), _bf(w_ref[...]))


def _proj_residual(x, y, w3, layer):
    m = x.shape[0]
    tm = min(m, ROW_TILE)
    kdim = y.shape[1]
    tk = min(kdim, 1024)
    return pl.pallas_call(
        _proj_res_body,
        grid=(m // tm, kdim // tk),
        in_specs=[
            pl.BlockSpec((tm, D_MODEL), lambda i, k: (i, 0)),
            pl.BlockSpec((tm, tk), lambda i, k: (i, k)),
            pl.BlockSpec((None, tk, D_MODEL), lambda i, k: (layer, k, 0)),
        ],
        out_specs=pl.BlockSpec((tm, D_MODEL), lambda i, k: (i, 0)),
        out_shape=jax.ShapeDtypeStruct((m, D_MODEL), F32),
        compiler_params=_cp(("parallel", "arbitrary"), 48),
        name="proj_residual",
    )(x, y, w3)


def _mem_kv_body(x_ref, g_ref, w_ref, k_ref, v_ref):
    xn = _bf(_rms(x_ref[...]) * g_ref[...])
    kv = _dot(xn, _bf(w_ref[...]))
    k_ref[...] = kv[:, :D_MODEL]
    v_ref[...] = kv[:, D_MODEL:]


def _mem_kv(mem2d, g3, w_xkv):
    m = mem2d.shape[0]
    tm = 512
    shp = jax.ShapeDtypeStruct((DEPTH, m, D_MODEL), F32)
    return pl.pallas_call(
        _mem_kv_body,
        grid=(DEPTH, m // tm),
        in_specs=[
            pl.BlockSpec((tm, D_MODEL), lambda l, i: (i, 0)),
            pl.BlockSpec((None, 1, D_MODEL), lambda l, i: (l, 0, 0)),
            pl.BlockSpec((None, D_MODEL, 2 * D_MODEL), lambda l, i: (l, 0, 0)),
        ],
        out_specs=[pl.BlockSpec((None, tm, D_MODEL), lambda l, i: (l, i, 0))] * 2,
        out_shape=[shp, shp],
        compiler_params=_cp(("arbitrary", "arbitrary"), 48),
        name="mem_kv",
    )(mem2d, g3, w_xkv)


def _xattn_prompt_body(x_ref, g_ref, wq_ref, wo_ref, k_ref, v_ref, o_ref):
    x = x_ref[...]
    xn = _bf(_rms(x) * g_ref[...])
    q = _dot(xn, _bf(wq_ref[...]))
    k = _bf(k_ref[...])
    v = _bf(v_ref[...])
    outs = []
    for h in range(X_HEADS):
        sl = slice(h * X_HEAD_DIM, (h + 1) * X_HEAD_DIM)
        s = _dot_nt(_bf(q[:, sl]), k[:, sl]) * (X_HEAD_DIM ** -0.5)
        e = jnp.exp(s - jnp.max(s, axis=-1, keepdims=True))
        p = e / jnp.sum(e, axis=-1, keepdims=True)
        outs.append(_bf(_dot(_bf(p), v[:, sl])))
    o_ref[...] = x + _dot(jnp.concatenate(outs, axis=1), _bf(wo_ref[...]))


def _xattn_prompt(x, g3, w_xq, w_xo, memk, memv, layer):
    tq = 512
    nq = SEQ // tq
    return pl.pallas_call(
        _xattn_prompt_body,
        grid=(BATCH, nq),
        in_specs=[
            pl.BlockSpec((tq, D_MODEL), lambda b, j: (b * nq + j, 0)),
            pl.BlockSpec((None, 1, D_MODEL), lambda b, j: (layer, 0, 0)),
            pl.BlockSpec((None, D_MODEL, D_MODEL), lambda b, j: (layer, 0, 0)),
            pl.BlockSpec((None, D_MODEL, D_MODEL), lambda b, j: (layer, 0, 0)),
            pl.BlockSpec((None, MEM_LEN, D_MODEL), lambda b, j: (layer, b, 0)),
            pl.BlockSpec((None, MEM_LEN, D_MODEL), lambda b, j: (layer, b, 0)),
        ],
        out_specs=pl.BlockSpec((tq, D_MODEL), lambda b, j: (b * nq + j, 0)),
        out_shape=jax.ShapeDtypeStruct((BATCH * SEQ, D_MODEL), F32),
        compiler_params=_cp(("parallel", "arbitrary"), 48),
        name="xattn_prompt",
    )(x, g3, w_xq, w_xo, memk, memv)


def _xattn_sample_body(q_ref, k_ref, v_ref, o_ref):
    for b in range(SB):
        prod = k_ref[b] * q_ref[b:b + 1, :]
        vb = v_ref[b]
        outs = []
        for h in range(X_HEADS):
            sl = slice(h * X_HEAD_DIM, (h + 1) * X_HEAD_DIM)
            s = jnp.sum(prod[:, sl], axis=-1, keepdims=True) * (X_HEAD_DIM ** -0.5)
            e = jnp.exp(s - jnp.max(s, axis=0, keepdims=True))
            p = e / jnp.sum(e, axis=0, keepdims=True)
            outs.append(jnp.sum(p * vb[:, sl], axis=0, keepdims=True))
        o_ref[b:b + 1, :] = jnp.concatenate(outs, axis=1)


def _xattn_sample(q, cache_k, cache_v, layer):
    blk = pl.BlockSpec((None, SB, MEM_LEN, D_MODEL), lambda i: (layer, i, 0, 0))
    return pl.pallas_call(
        _xattn_sample_body,
        grid=(DEC_BATCH // SB,),
        in_specs=[pl.BlockSpec((SB, D_MODEL), lambda i: (i, 0)), blk, blk],
        out_specs=pl.BlockSpec((SB, D_MODEL), lambda i: (i, 0)),
        out_shape=jax.ShapeDtypeStruct((DEC_BATCH, D_MODEL), F32),
        compiler_params=_cp(("parallel",), 48),
        name="xattn_sample",
    )(q, cache_k, cache_v)


def _ret_log_gamma(h):
    return math.log1p(-(2.0 ** (-5.0 - h)))


def _ab_prompt_body(z_ref, rv_ref, rg_ref, xbc_ref, rq_ref, rk_ref, dt_ref, cos_ref, sin_ref,
                    cw_ref, cb_ref, dtb_ref, alog_ref, dsk_ref, nrm_ref,
                    y_ref, conv_ref, h_ref, s_ref, cbuf):
    c = pl.program_id(1)

    @pl.when(c == 0)
    def _():
        cbuf[0:SUBLANES, :] = jnp.zeros((SUBLANES, SSD_CONV_CH), F32)
        h_ref[...] = jnp.zeros_like(h_ref)
        s_ref[...] = jnp.zeros_like(s_ref)

    u = xbc_ref[...]
    cbuf[SUBLANES:SUBLANES + CHUNK, :] = u
    w = cw_ref[...]
    conv = cb_ref[...] + (((cbuf[5:5 + CHUNK, :] * w[0:1, :] + cbuf[6:6 + CHUNK, :] * w[1:2, :])
                           + cbuf[7:7 + CHUNK, :] * w[2:3, :]) + u * w[3:4, :])
    tail = cbuf[CHUNK + 5:CHUNK + 8, :]
    cbuf[5:8, :] = tail
    for j in range(SSD_CONV - 1):
        conv_ref[0:1, j * SSD_CONV_CH:(j + 1) * SSD_CONV_CH] = tail[j:j + 1, :]
    xbc = _silu(conv)
    xs = xbc[:, :SSD_D_INNER]
    bm = xbc[:, SSD_D_INNER:SSD_D_INNER + GROUP_W]
    cm = xbc[:, SSD_D_INNER + GROUP_W:]

    dt = _softplus(dt_ref[...] + dtb_ref[...])
    da = dt * (-jnp.exp(alog_ref[...]))
    cs = _cumsum_rows(da)
    cs_t = cs.T
    row = lax.broadcasted_iota(jnp.int32, (CHUNK, CHUNK), 0)
    col = lax.broadcasted_iota(jnp.int32, (CHUNK, CHUNK), 1)
    tri = row >= col
    lo = col < SSD_HEAD_DIM
    for g in range(SSD_GROUPS):
        gs = slice(g * GROUP_W, (g + 1) * GROUP_W)
        ns = slice(g * SSD_D_STATE, (g + 1) * SSD_D_STATE)
        cmg = _bf(cm[:, ns])
        bmg = _bf(bm[:, ns])
        att = _dot_nt(cmg, bmg)
        hprev = h_ref[gs, :]
        yint = _dot_nt(cmg, _bf(hprev))
        ys, wxs, css = [], [], []
        for j in range(HEADS_PER_GROUP // 2):
            h0 = g * HEADS_PER_GROUP + 2 * j
            cb0 = _lane_bcast(cs, h0)
            cb1 = _lane_bcast(cs, h0 + 1)
            cs_p = jnp.where(lo, cb0, cb1)
            dt_p = jnp.where(lo, _lane_bcast(dt, h0), _lane_bcast(dt, h0 + 1))
            off = g * GROUP_W + j * LANES
            xdt = xs[:, off:off + LANES] * dt_p
            d0 = jnp.exp(jnp.where(tri, cb0 - cs_t[h0:h0 + 1, :], -jnp.inf))
            d1 = jnp.exp(jnp.where(tri, cb1 - cs_t[h0 + 1:h0 + 2, :], -jnp.inf))
            yy = _dot(jnp.concatenate([_bf(att * d0), _bf(att * d1)], axis=0), _bf(xdt))
            ys.append(jnp.where(lo, yy[:CHUNK], yy[CHUNK:]) + yint[:, j * LANES:(j + 1) * LANES] * jnp.exp(cs_p))
            wxs.append(_bf(xdt * jnp.exp(cs_p[CHUNK - 1:CHUNK, :] - cs_p)))
            css.append(cs_p)
        cs_g = jnp.concatenate(css, axis=1)
        last_t = jnp.broadcast_to(cs_g[CHUNK - 1:CHUNK, :], (CHUNK, GROUP_W)).T
        h_ref[gs, :] = hprev * jnp.exp(last_t) + _dot_tn(jnp.concatenate(wxs, axis=1), bmg)
        yg = jnp.concatenate(ys, axis=1)
        yg = (yg + xs[:, gs] * dsk_ref[:, gs]) * _silu(z_ref[:, gs])
        y_ref[:, gs] = _bf(_rms(yg) * nrm_ref[:, gs])

    cos = cos_ref[...]
    sin = sin_ref[...]
    tcol = row.astype(F32)
    diff = tcol - col.astype(F32)
    for h in range(RET_HEADS):
        lg = _ret_log_gamma(h)
        ks = slice(h * RET_DK, (h + 1) * RET_DK)
        vs = slice(h * RET_DV, (h + 1) * RET_DV)

        def rot(ref):
            x1 = ref[:, h * RET_DK:h * RET_DK + LANES]
            x2 = ref[:, h * RET_DK + LANES:(h + 1) * RET_DK]
            return jnp.concatenate([x1 * cos - x2 * sin, x1 * sin + x2 * cos], axis=1)

        qr = rot(rq_ref)
        kr = rot(rk_ref) * (RET_DK ** -0.5)
        qb = _bf(qr)
        decay = jnp.exp(jnp.where(tri, diff * lg, -jnp.inf))
        att = _dot_nt(qb, _bf(kr)) * decay
        vb = _bf(rv_ref[:, vs])
        s_prev = s_ref[h]
        inner = jnp.exp((tcol + 1.0) * lg)
        r = _dot(_bf(att), vb) + _dot(qb, _bf(s_prev)) * jnp.concatenate([inner] * (RET_DV // LANES), axis=1)
        tail_w = jnp.exp((CHUNK - 1.0 - tcol) * lg)
        kt = _bf(kr * jnp.concatenate([tail_w] * (RET_DK // LANES), axis=1))
        s_ref[h] = s_prev * math.exp(CHUNK * lg) + _dot_tn(kt, vb)
        os = slice(SSD_D_INNER + h * RET_DV, SSD_D_INNER + (h + 1) * RET_DV)
        y_ref[:, os] = _bf(_silu(rg_ref[:, vs]) * _rms(r))


def _mixer_ab_prompt(p_main, p_dt, cos, sin, conv_w, conv_b, dt_bias, a_log, d_skip_e, ssd_norm):
    nc = SEQ // CHUNK
    m = BATCH * SEQ

    def rowspec(width, cb):
        return pl.BlockSpec((CHUNK, width), lambda b, c: (b * nc + c, cb))

    def full(a):
        return pl.BlockSpec(a.shape, lambda b, c: (0,) * a.ndim)

    params = [conv_w, conv_b, dt_bias, a_log, d_skip_e, ssd_norm]
    return pl.pallas_call(
        _ab_prompt_body,
        grid=(BATCH, nc),
        in_specs=[rowspec(SSD_D_INNER, 0), rowspec(RET_V, 1), rowspec(RET_V, 2), rowspec(SSD_CONV_CH, 2),
                  rowspec(RET_QK, 9), rowspec(RET_QK, 10), rowspec(LANES, 0),
                  pl.BlockSpec((CHUNK, LANES), lambda b, c: (c, 0)),
                  pl.BlockSpec((CHUNK, LANES), lambda b, c: (c, 0))] + [full(a) for a in params],
        out_specs=[
            pl.BlockSpec((CHUNK, SSD_D_INNER + RET_V), lambda b, c: (b * nc + c, 0)),
            pl.BlockSpec((None, 1, (SSD_CONV - 1) * SSD_CONV_CH), lambda b, c: (b, 0, 0)),
            pl.BlockSpec((None, SSD_D_INNER, SSD_D_STATE), lambda b, c: (b, 0, 0)),
            pl.BlockSpec((None, RET_HEADS, RET_DK, RET_DV), lambda b, c: (b, 0, 0, 0)),
        ],
        out_shape=[
            jax.ShapeDtypeStruct((m, SSD_D_INNER + RET_V), BF16),
            jax.ShapeDtypeStruct((BATCH, 1, (SSD_CONV - 1) * SSD_CONV_CH), F32),
            jax.ShapeDtypeStruct((BATCH, SSD_D_INNER, SSD_D_STATE), F32),
            jax.ShapeDtypeStruct((BATCH, RET_HEADS, RET_DK, RET_DV), F32),
        ],
        scratch_shapes=[pltpu.VMEM((SUBLANES + CHUNK, SSD_CONV_CH), F32)],
        compiler_params=_cp(("parallel", "arbitrary"), 56),
        name="mixer_ab_prompt",
    )(p_main, p_main, p_main, p_main, p_main, p_main, p_dt, cos, sin, *params)


def _c_prompt_body(q_ref, k_ref, v_ref, o_ref, gt_ref, ib_ref, fb_ref, nrm_ref,
                   h_ref, c_ref, n_ref, m_ref, m_s):
    c = pl.program_id(1)

    @pl.when(c == 0)
    def _():
        c_ref[...] = jnp.zeros_like(c_ref)
        n_ref[...] = jnp.zeros_like(n_ref)
        m_s[...] = jnp.zeros_like(m_s)

    ipre = gt_ref[:, :LANES] + ib_ref[...]
    lf = -_softplus(-(gt_ref[:, LANES:] + fb_ref[...]))
    b = _cumsum_rows(lf)
    g = ipre - b
    g_t = g.T
    b_t = b.T
    row = lax.broadcasted_iota(jnp.int32, (CHUNK, CHUNK), 0)
    col = lax.broadcasted_iota(jnp.int32, (CHUNK, CHUNK), 1)
    tri = row >= col
    cmax = g_t
    sh = 1
    while sh < CHUNK:
        cmax = jnp.maximum(cmax, jnp.where(col >= sh, pltpu.roll(cmax, sh, axis=1), -jnp.inf))
        sh *= 2
    m_prev = m_s[...]
    mt_t = b_t + jnp.maximum(m_prev, cmax)
    mt = mt_t.T
    m_prev_c = m_prev.T
    inter = jnp.exp(b + m_prev_c - mt)
    emt = jnp.exp(-mt)
    wl = jnp.exp(g + b[CHUNK - 1:CHUNK, :] - mt[CHUNK - 1:CHUNK, :])
    bm = b - mt
    m_new = _lane_bcast(mt_t, CHUNK - 1)
    dp_t = jnp.exp(_lane_bcast(b_t, CHUNK - 1) + m_prev - m_new)
    m_s[...] = m_new
    m_ref[...] = m_new[0:SUBLANES, :]
    outs = []
    for h in range(M_HEADS):
        ks = slice(h * M_DK, (h + 1) * M_DK)
        vs = slice(h * M_DV, (h + 1) * M_DV)
        wgt = jnp.exp(jnp.where(tri, g_t[h:h + 1, :] + bm[:, h:h + 1], -jnp.inf))
        qh = q_ref[:, ks]
        kh = k_ref[:, ks] * (M_DK ** -0.5)
        qb = _bf(qh)
        vb = _bf(v_ref[:, vs])
        a = _dot_nt(qb, _bf(kh)) * wgt
        c_prev = c_ref[h]
        n_prev = n_ref[h:h + 1, :]
        ic = inter[:, h:h + 1]
        num = _dot(_bf(a), vb) + _dot(qb, _bf(c_prev)) * ic
        den = jnp.sum(a, axis=1, keepdims=True) + jnp.sum(qh * n_prev, axis=1, keepdims=True) * ic
        outs.append(num / jnp.maximum(jnp.abs(den), emt[:, h:h + 1]))
        kw = kh * wl[:, h:h + 1]
        dp_row = dp_t[h:h + 1, :]
        c_ref[h] = c_prev * jnp.concatenate([dp_row] * (M_DV // LANES), axis=1) + _dot_tn(_bf(kw), vb)
        n_ref[h:h + 1, :] = n_prev * dp_row + jnp.sum(kw, axis=0, keepdims=True)
    for h in range(M_HEADS):
        vs = slice(h * M_DV, (h + 1) * M_DV)
        h_ref[:, vs] = _bf(jax.nn.sigmoid(o_ref[:, vs]) * (_rms(outs[h]) * nrm_ref[:, vs]))


def _mixer_c_prompt(p_main, p_gate, i_bias, f_bias, norm_g):
    nc = SEQ // CHUNK
    m = BATCH * SEQ

    def rowspec(width, cb):
        return pl.BlockSpec((CHUNK, width), lambda b, c: (b * nc + c, cb))

    def full(a):
        return pl.BlockSpec(a.shape, lambda b, c: (0,) * a.ndim)

    params = [i_bias, f_bias, norm_g]
    return pl.pallas_call(
        _c_prompt_body,
        grid=(BATCH, nc),
        in_specs=[rowspec(M_QK, 0), rowspec(M_QK, 1), rowspec(M_V, 1), rowspec(M_V, 2), rowspec(2 * LANES, 0)]
        + [full(a) for a in params],
        out_specs=[
            pl.BlockSpec((CHUNK, M_V), lambda b, c: (b * nc + c, 0)),
            pl.BlockSpec((None, M_HEADS, M_DK, M_DV), lambda b, c: (b, 0, 0, 0)),
            pl.BlockSpec((None, M_HEADS, M_DK), lambda b, c: (b, 0, 0)),
            pl.BlockSpec((None, SUBLANES, LANES), lambda b, c: (b, 0, 0)),
        ],
        out_shape=[
            jax.ShapeDtypeStruct((m, M_V), BF16),
            jax.ShapeDtypeStruct((BATCH, M_HEADS, M_DK, M_DV), F32),
            jax.ShapeDtypeStruct((BATCH, M_HEADS, M_DK), F32),
            jax.ShapeDtypeStruct((BATCH, SUBLANES, LANES), F32),
        ],
        scratch_shapes=[pltpu.VMEM((CHUNK, LANES), F32)],
        compiler_params=_cp(("parallel", "arbitrary"), 48),
        name="mixer_c_prompt",
    )(p_main, p_main, p_main, p_main, p_gate, *params)


def _ab_sample_prep_body(xbc_ref, rq_ref, rk_ref, dt_ref, cst_ref, cos_ref, sin_ref,
                         cw_ref, cb_ref, dtb_ref, alog_ref,
                         conv_ref, xs_ref, xdt_ref, eda_ref, bm_ref, cm_ref, q_ref, k_ref, gam_ref):
    ch = SSD_CONV_CH
    u = xbc_ref[...]
    w = cw_ref[...]
    b0 = cst_ref[:, 0:ch]
    b1 = cst_ref[:, ch:2 * ch]
    b2 = cst_ref[:, 2 * ch:3 * ch]
    conv = cb_ref[...] + (((b0 * w[0:1, :] + b1 * w[1:2, :]) + b2 * w[2:3, :]) + u * w[3:4, :])
    conv_ref[:, 0:ch] = b1
    conv_ref[:, ch:2 * ch] = b2
    conv_ref[:, 2 * ch:3 * ch] = u
    xbc = _silu(conv)
    xs = xbc[:, :SSD_D_INNER]
    xs_ref[...] = xs
    bm_ref[...] = xbc[:, SSD_D_INNER:SSD_D_INNER + GROUP_W]
    cm_ref[...] = xbc[:, SSD_D_INNER + GROUP_W:]
    dt = _softplus(dt_ref[...] + dtb_ref[...])
    eda = jnp.exp(dt * (-jnp.exp(alog_ref[...])))
    xdt_ref[...] = xs * _pair_expand(dt, SSD_HEADS)
    eda_ref[...] = _pair_expand(eda, SSD_HEADS)
    cos = cos_ref[...]
    sin = sin_ref[...]
    for h in range(RET_HEADS):
        a = slice(h * RET_DK, h * RET_DK + LANES)
        b = slice(h * RET_DK + LANES, (h + 1) * RET_DK)
        q1, q2 = rq_ref[:, a], rq_ref[:, b]
        k1, k2 = rk_ref[:, a], rk_ref[:, b]
        q_ref[:, a] = q1 * cos - q2 * sin
        q_ref[:, b] = q1 * sin + q2 * cos
        k_ref[:, a] = (k1 * cos - k2 * sin) * (RET_DK ** -0.5)
        k_ref[:, b] = (k1 * sin + k2 * cos) * (RET_DK ** -0.5)
        gam_ref[:, h * RET_DK:(h + 1) * RET_DK] = jnp.full((DEC_BATCH, RET_DK), math.exp(_ret_log_gamma(h)), F32)


def _ab_sample_prep(p_main, p_dt, conv_state, cos, sin, conv_w, conv_b, dt_bias, a_log):
    n = DEC_BATCH

    def colspec(width, cb):
        return pl.BlockSpec((n, width), lambda i: (0, cb))

    def full(a):
        return pl.BlockSpec(a.shape, lambda i: (0,) * a.ndim)

    small = [conv_state, cos, sin, conv_w, conv_b, dt_bias, a_log]

    def out(width):
        return jax.ShapeDtypeStruct((n, width), F32)

    widths = [(SSD_CONV - 1) * SSD_CONV_CH, SSD_D_INNER, SSD_D_INNER, SSD_D_INNER, GROUP_W, GROUP_W,
              RET_QK, RET_QK, RET_QK]
    return pl.pallas_call(
        _ab_sample_prep_body,
        grid=(1,),
        in_specs=[colspec(SSD_CONV_CH, 2), colspec(RET_QK, 9), colspec(RET_QK, 10), full(p_dt)]
        + [full(a) for a in small],
        out_specs=[pl.BlockSpec((n, wd), lambda i: (0, 0)) for wd in widths],
        out_shape=[out(wd) for wd in widths],
        compiler_params=_cp(("arbitrary",), 48),
        name="ab_sample_prep",
    )(p_main, p_main, p_main, p_dt, *small)


def _ssd_state_body(eda_ref, xdt_ref, bm_ref, cm_ref, h_ref, ho_ref, y_ref):
    eda_t = _pad_t(eda_ref[...])
    xdt_t = _pad_t(xdt_ref[...])
    lane = lax.broadcasted_iota(jnp.int32, (GROUP_W, LANES), 1)
    ycols = jnp.zeros((GROUP_W, LANES), F32)
    for b in range(SB):
        hn = h_ref[b] * eda_t[:, b:b + 1] + xdt_t[:, b:b + 1] * bm_ref[b:b + 1, :]
        ho_ref[b] = hn
        ycols = jnp.where(lane == b, jnp.sum(hn * cm_ref[b:b + 1, :], axis=1, keepdims=True), ycols)
    y_ref[...] = ycols.T[0:SB, :]


def _ssd_state(eda, xdt, bm, cm, state):
    vec = pl.BlockSpec((SB, GROUP_W), lambda i, g: (i, g))
    bc = pl.BlockSpec((SB, SSD_D_STATE), lambda i, g: (i, g))
    st = pl.BlockSpec((SB, GROUP_W, SSD_D_STATE), lambda i, g: (i, g, 0))
    return pl.pallas_call(
        _ssd_state_body,
        grid=(DEC_BATCH // SB, SSD_GROUPS),
        in_specs=[vec, vec, bc, bc, st],
        out_specs=[st, vec],
        out_shape=[jax.ShapeDtypeStruct(state.shape, F32), jax.ShapeDtypeStruct((DEC_BATCH, SSD_D_INNER), F32)],
        compiler_params=_cp(("parallel", "arbitrary"), 48),
        name="ssd_state",
    )(eda, xdt, bm, cm, state)


def _outer_state_body(d_ref, k_ref, q_ref, v_ref, s_ref, so_ref, o_ref):
    d_t = _pad_t(d_ref[...])
    k_t = _pad_t(k_ref[...])
    q_t = _pad_t(q_ref[...])
    for b in range(SB):
        sn = s_ref[b] * d_t[:, b:b + 1] + k_t[:, b:b + 1] * v_ref[b:b + 1, :]
        so_ref[b] = sn
        o_ref[b:b + 1, :] = jnp.sum(sn * q_t[:, b:b + 1], axis=0, keepdims=True)


def _outer_state(d, k, q, v, state):
    _, nh, dk, dv = state.shape
    kv = pl.BlockSpec((SB, dk), lambda i, h: (i, h))
    vv = pl.BlockSpec((SB, dv), lambda i, h: (i, h))
    st = pl.BlockSpec((SB, None, dk, dv), lambda i, h: (i, h, 0, 0))
    return pl.pallas_call(
        _outer_state_body,
        grid=(DEC_BATCH // SB, nh),
        in_specs=[kv, kv, kv, vv, st],
        out_specs=[st, vv],
        out_shape=[jax.ShapeDtypeStruct(state.shape, F32), jax.ShapeDtypeStruct((DEC_BATCH, nh * dv), F32)],
        compiler_params=_cp(("parallel", "arbitrary"), 48),
        name="outer_state",
    )(d, k, q, v, state)


def _ab_sample_post_body(y_ref, xs_ref, z_ref, r_ref, rg_ref, dsk_ref, nrm_ref, o_ref):
    for g in range(SSD_GROUPS):
        gs = slice(g * GROUP_W, (g + 1) * GROUP_W)
        yg = (y_ref[:, gs] + xs_ref[:, gs] * dsk_ref[:, gs]) * _silu(z_ref[:, gs])
        o_ref[:, gs] = _bf(_rms(yg) * nrm_ref[:, gs])
    for h in range(RET_HEADS):
        vs = slice(h * RET_DV, (h + 1) * RET_DV)
        os = slice(SSD_D_INNER + h * RET_DV, SSD_D_INNER + (h + 1) * RET_DV)
        o_ref[:, os] = _bf(_silu(rg_ref[:, vs]) * _rms(r_ref[:, vs]))


def _ab_sample_post(y, xs, p_main, r, d_skip_e, ssd_norm):
    n = DEC_BATCH

    def full(a):
        return pl.BlockSpec(a.shape, lambda i: (0,) * a.ndim)

    return pl.pallas_call(
        _ab_sample_post_body,
        grid=(1,),
        in_specs=[full(y), full(xs), pl.BlockSpec((n, SSD_D_INNER), lambda i: (0, 0)), full(r),
                  pl.BlockSpec((n, RET_V), lambda i: (0, 2)), full(d_skip_e), full(ssd_norm)],
        out_specs=pl.BlockSpec((n, SSD_D_INNER + RET_V), lambda i: (0, 0)),
        out_shape=jax.ShapeDtypeStruct((n, SSD_D_INNER + RET_V), BF16),
        compiler_params=_cp(("arbitrary",), 48),
        name="ab_sample_post",
    )(y, xs, p_main, r, p_main, d_skip_e, ssd_norm)


def _c_sample_prep_body(q_ref, k_ref, gt_ref, n_ref, m_ref, ib_ref, fb_ref,
                        dpe_ref, kw_ref, nn_ref, mn_ref, dn_ref):
    ipre = gt_ref[:, :LANES] + ib_ref[...]
    lf = -_softplus(-(gt_ref[:, LANES:] + fb_ref[...]))
    m_prev = m_ref[...]
    mt = jnp.maximum(lf + m_prev, ipre)
    wgt = jnp.exp(ipre - mt)
    dp = jnp.exp(lf + m_prev - mt)
    emt = jnp.exp(-mt)
    mn_ref[...] = mt
    for h in range(M_HEADS):
        ks = slice(h * M_DK, (h + 1) * M_DK)
        dpe = _lane_bcast(dp, h)
        kw = k_ref[:, ks] * (M_DK ** -0.5) * _lane_bcast(wgt, h)
        nn = n_ref[:, ks] * dpe + kw
        den = jnp.sum(nn * q_ref[:, ks], axis=1, keepdims=True)
        dpe_ref[:, ks] = dpe
        kw_ref[:, ks] = kw
        nn_ref[:, ks] = nn
        dn_ref[:, h * M_DV:(h + 1) * M_DV] = jnp.broadcast_to(
            jnp.maximum(jnp.abs(den), emt[:, h:h + 1]), (DEC_BATCH, M_DV))


def _c_sample_prep(p_main, p_gate, n_state, m_state, i_bias, f_bias):
    n = DEC_BATCH

    def full(a):
        return pl.BlockSpec(a.shape, lambda i: (0,) * a.ndim)

    widths = [M_QK, M_QK, M_QK, LANES, M_V]
    return pl.pallas_call(
        _c_sample_prep_body,
        grid=(1,),
        in_specs=[pl.BlockSpec((n, M_QK), lambda i: (0, 0)), pl.BlockSpec((n, M_QK), lambda i: (0, 1)),
                  full(p_gate), full(n_state), full(m_state), full(i_bias), full(f_bias)],
        out_specs=[pl.BlockSpec((n, wd), lambda i: (0, 0)) for wd in widths],
        out_shape=[jax.ShapeDtypeStruct((n, wd), F32) for wd in widths],
        compiler_params=_cp(("arbitrary",), 48),
        name="c_sample_prep",
    )(p_main, p_main, p_gate, n_state, m_state, i_bias, f_bias)


def _c_sample_post_body(num_ref, dn_ref, o_ref, nrm_ref, h_ref):
    for h in range(M_HEADS):
        vs = slice(h * M_DV, (h + 1) * M_DV)
        hc = num_ref[:, vs] / dn_ref[:, vs]
        h_ref[:, vs] = _bf(jax.nn.sigmoid(o_ref[:, vs]) * (_rms(hc) * nrm_ref[:, vs]))


def _c_sample_post(num, den, p_main, norm_g):
    n = DEC_BATCH

    def full(a):
        return pl.BlockSpec(a.shape, lambda i: (0,) * a.ndim)

    return pl.pallas_call(
        _c_sample_post_body,
        grid=(1,),
        in_specs=[full(num), full(den), pl.BlockSpec((n, M_V), lambda i: (0, 2)), full(norm_g)],
        out_specs=pl.BlockSpec((n, M_V), lambda i: (0, 0)),
        out_shape=jax.ShapeDtypeStruct((n, M_V), BF16),
        compiler_params=_cp(("arbitrary",), 48),
        name="c_sample_post",
    )(num, den, p_main, norm_g)


def _rope_tables(pos):
    half = RET_DK // 2
    inv = jnp.exp(-math.log(ROPE_BASE) * jnp.arange(half, dtype=F32) / half)
    ang = pos.astype(F32)[:, None] * inv
    return jnp.cos(ang), jnp.sin(ang)


def _pad_lanes(v, width=LANES):
    return jnp.pad(v.reshape(1, -1), ((0, 0), (0, width - v.size)))


def kernel(x_prompt, x_sample, cache_mem_k, cache_mem_v, state_conv, state_ssm, state_ret, state_mlstm_c, state_mlstm_n, state_mlstm_m, mem_prompt, norm_ffn1, w_ffn1_in, w_ffn1_out, norm_mix, w_in_ab, ssd_conv_w, ssd_conv_b, ssd_dt_bias, ssd_a_log, ssd_d, ssd_norm, w_out_ab, w_in_c, mlstm_i_bias, mlstm_f_bias, mlstm_norm, w_out_c, norm_xattn, norm_mem, w_xq, w_xkv, w_xo, norm_ffn2, w_ffn2_in, w_ffn2_out, norm_final):
    g3 = lambda g: g.reshape(DEPTH, 1, D_MODEL)
    n_ffn1, n_mix, n_x, n_mem, n_ffn2 = g3(norm_ffn1), g3(norm_mix), g3(norm_xattn), g3(norm_mem), g3(norm_ffn2)

    wz, wxbc, wdt, wrq, wrk, wrv, wrg = jnp.split(w_in_ab[0], np_cumsum(AB_SIZES), axis=1)
    w_ab_main = jnp.concatenate([wz, wrv, wrg, wxbc, wrq, wrk], axis=1).astype(BF16)[None]
    w_ab_dt = jnp.pad(wdt, ((0, 0), (0, LANES - SSD_HEADS))).astype(BF16)[None]
    wq, wk, wv, wi, wf, wo = jnp.split(w_in_c[0], np_cumsum(C_SIZES), axis=1)
    w_c_main = jnp.concatenate([wq, wk, wv, wo], axis=1).astype(BF16)[None]
    gpad = ((0, 0), (0, LANES - M_HEADS))
    w_c_gate = jnp.concatenate([jnp.pad(wi, gpad), jnp.pad(wf, gpad)], axis=1).astype(BF16)[None]

    conv_w = ssd_conv_w[0]
    conv_b = ssd_conv_b.reshape(1, SSD_CONV_CH)
    dt_bias = _pad_lanes(ssd_dt_bias[0])
    a_log = _pad_lanes(ssd_a_log[0])
    d_skip_e = jnp.repeat(ssd_d[0], SSD_HEAD_DIM).reshape(1, SSD_D_INNER)
    s_norm = ssd_norm.reshape(1, SSD_D_INNER)
    i_bias = _pad_lanes(mlstm_i_bias[0])
    f_bias = _pad_lanes(mlstm_f_bias[0])
    m_norm = mlstm_norm.reshape(1, M_V)

    memk, memv = _mem_kv(mem_prompt.reshape(BATCH * MEM_LEN, D_MODEL), n_mem, w_xkv)
    cos_p, sin_p = _rope_tables(jnp.arange(SEQ))
    x = x_prompt.reshape(BATCH * SEQ, D_MODEL)
    x = _ffn(x, n_ffn1, w_ffn1_in, w_ffn1_out, 0)
    p_main, p_dt = _norm_proj(x, n_mix, 0, w_ab_main, 0, 1024, w_ab_dt)
    ycat, conv_p, ssm_p, ret_p = _mixer_ab_prompt(p_main, p_dt, cos_p, sin_p, conv_w, conv_b, dt_bias, a_log,
                                                  d_skip_e, s_norm)
    x = _proj_residual(x, ycat, w_out_ab, 0)
    x = _xattn_prompt(x, n_x, w_xq, w_xo, memk, memv, 0)
    x = _ffn(x, n_ffn2, w_ffn2_in, w_ffn2_out, 0)
    x = _ffn(x, n_ffn1, w_ffn1_in, w_ffn1_out, 1)
    pc_main, pc_gate = _norm_proj(x, n_mix, 1, w_c_main, 0, 1024, w_c_gate)
    hout, mc_p, mn_p, mm_p = _mixer_c_prompt(pc_main, pc_gate, i_bias, f_bias, m_norm)
    x = _proj_residual(x, hout, w_out_c, 0)
    x = _xattn_prompt(x, n_x, w_xq, w_xo, memk, memv, 1)
    y_prompt = _ffn(x, n_ffn2, w_ffn2_in, w_ffn2_out, 1, norm_final).reshape(BATCH, SEQ, D_MODEL)

    cache_k = cache_mem_k.reshape(DEPTH, DEC_BATCH, MEM_LEN, D_MODEL)
    cache_v = cache_mem_v.reshape(DEPTH, DEC_BATCH, MEM_LEN, D_MODEL)
    cos_s, sin_s = _rope_tables(PAST_LEN + jnp.arange(1))

    def xattn_s(xs_, layer):
        q = _norm_proj(xs_, n_x, layer, w_xq, layer, 1024)
        o = _xattn_sample(q, cache_k, cache_v, layer)
        return _proj_residual(xs_, o, w_xo, layer)

    xs_ = x_sample.reshape(DEC_BATCH, D_MODEL)
    xs_ = _ffn(xs_, n_ffn1, w_ffn1_in, w_ffn1_out, 0)
    sp_main, sp_dt = _norm_proj(xs_, n_mix, 0, w_ab_main, 0, 1024, w_ab_dt)
    conv_s, xs_c, xdt, eda, bm_s, cm_s, q_s, k_s, gam = _ab_sample_prep(
        sp_main, sp_dt, state_conv.reshape(DEC_BATCH, (SSD_CONV - 1) * SSD_CONV_CH), cos_s, sin_s,
        conv_w, conv_b, dt_bias, a_log)
    ssm_s, y_s = _ssd_state(eda, xdt, bm_s, cm_s, state_ssm.reshape(DEC_BATCH, SSD_D_INNER, SSD_D_STATE))
    ret_s, r_s = _outer_state(gam, k_s, q_s, sp_main[:, SSD_D_INNER:SSD_D_INNER + RET_V], state_ret[0])
    ycat_s = _ab_sample_post(y_s, xs_c, sp_main, r_s, d_skip_e, s_norm)
    xs_ = _proj_residual(xs_, ycat_s, w_out_ab, 0)
    xs_ = xattn_s(xs_, 0)
    xs_ = _ffn(xs_, n_ffn2, w_ffn2_in, w_ffn2_out, 0)
    xs_ = _ffn(xs_, n_ffn1, w_ffn1_in, w_ffn1_out, 1)
    sc_main, sc_gate = _norm_proj(xs_, n_mix, 1, w_c_main, 0, 1024, w_c_gate)
    m_in = jnp.pad(state_mlstm_m[0], ((0, 0), (0, LANES - M_HEADS)))
    dpe, kw, mn_s, mm_s, den = _c_sample_prep(sc_main, sc_gate, state_mlstm_n.reshape(DEC_BATCH, M_QK), m_in,
                                              i_bias, f_bias)
    mc_s, num = _outer_state(dpe, kw, sc_main[:, :M_QK], sc_main[:, 2 * M_QK:2 * M_QK + M_V], state_mlstm_c[0])
    hout_s = _c_sample_post(num, den, sc_main, m_norm)
    xs_ = _proj_residual(xs_, hout_s, w_out_c, 0)
    xs_ = xattn_s(xs_, 1)
    y_sample = _ffn(xs_, n_ffn2, w_ffn2_in, w_ffn2_out, 1, norm_final).reshape(DEC_BATCH, 1, D_MODEL)

    kv_shape = (DEPTH, BATCH, MEM_LEN, X_HEADS, X_HEAD_DIM)
    return (y_prompt, y_sample, memk.reshape(kv_shape), memv.reshape(kv_shape),
            conv_p.reshape(1, BATCH, SSD_CONV - 1, SSD_CONV_CH),
            conv_s.reshape(1, DEC_BATCH, SSD_CONV - 1, SSD_CONV_CH),
            ssm_p.reshape(1, BATCH, SSD_HEADS, SSD_HEAD_DIM, SSD_D_STATE),
            ssm_s.reshape(1, DEC_BATCH, SSD_HEADS, SSD_HEAD_DIM, SSD_D_STATE),
            ret_p[None], ret_s[None], mc_p[None], mc_s[None],
            mn_p[None], mn_s.reshape(1, DEC_BATCH, M_HEADS, M_DK),
            mm_p[:, :M_HEADS, 0][None], mm_s[:, :M_HEADS][None])


def np_cumsum(sizes):
    out, acc = [], 0
    for s in sizes[:-1]:
        acc += s
        out.append(acc)
    return out
```

```python
import functools
import math

import jax
import jax.numpy as jnp
from jax import lax
from jax.experimental import pallas as pl
from jax.experimental.pallas import tpu as pltpu

F32 = jnp.float32
BF16 = jnp.bfloat16
EPS = 1e-6

D_MODEL = 1024
BATCH = 8
SEQ = 2048
DEPTH = 2
DEC_BATCH = 128
PAST_LEN = 16384
CHUNK = 128
D_FF = 2816
SSD_D_INNER = 2 * D_MODEL
SSD_HEAD_DIM = 64
SSD_HEADS = SSD_D_INNER // SSD_HEAD_DIM
SSD_GROUPS = 4
SSD_D_STATE = 128
SSD_CONV = 4
SSD_CONV_CH = SSD_D_INNER + 2 * SSD_GROUPS * SSD_D_STATE
RET_HEADS = 4
RET_QK = D_MODEL
RET_V = 2 * D_MODEL
RET_DK = RET_QK // RET_HEADS
RET_DV = RET_V // RET_HEADS
ROPE_BASE = 10000.0
AB_SIZES = (SSD_D_INNER, SSD_CONV_CH, SSD_HEADS, RET_QK, RET_QK, RET_V, RET_V)
M_HEADS = 4
M_QK = D_MODEL // 2
M_V = D_MODEL
M_DK = M_QK // M_HEADS
M_DV = M_V // M_HEADS
C_SIZES = (M_QK, M_QK, M_V, M_HEADS, M_HEADS, M_V)
MEM_LEN = 256
X_HEADS = 4
X_HEAD_DIM = D_MODEL // X_HEADS

LANES = 128
SUBLANES = 8
GROUP_W = SSD_D_INNER // SSD_GROUPS
HEADS_PER_GROUP = SSD_HEADS // SSD_GROUPS
ROW_TILE = 1024
FF_TILE = 256
SB = 8
XB = 4


def _cp(sem, mib):
    return pltpu.CompilerParams(dimension_semantics=sem, vmem_limit_bytes=mib * 1024 * 1024)


def _bf(x):
    return x.astype(BF16)


def _dot(a, b):
    return jnp.dot(a, b, preferred_element_type=F32)


def _dot_nt(a, b):
    return lax.dot_general(a, b, (((1,), (1,)), ((), ())), preferred_element_type=F32)


def _dot_tn(a, b):
    return lax.dot_general(a, b, (((0,), (0,)), ((), ())), preferred_element_type=F32)


def _rms(x):
    return x * lax.rsqrt(jnp.mean(x * x, axis=-1, keepdims=True) + EPS)


def _silu(x):
    return x * jax.nn.sigmoid(x)


def _softplus(x):
    return jnp.maximum(x, 0.0) + jnp.log1p(jnp.exp(-jnp.abs(x)))


def _split3(x):
    hi = x.astype(BF16)
    r = x - hi.astype(F32)
    mid = r.astype(BF16)
    lo = (r - mid.astype(F32)).astype(BF16)
    return hi, mid, lo


def _cumsum_rows(x):
    n = x.shape[0]
    r = lax.broadcasted_iota(jnp.int32, (n, n), 0)
    c = lax.broadcasted_iota(jnp.int32, (n, n), 1)
    t = jnp.where(r >= c, 1.0, 0.0).astype(BF16)
    hi, mid, lo = _split3(x)
    return _dot(t, hi) + _dot(t, mid) + _dot(t, lo)


def _lane_bcast(x, h, width=LANES):
    return jnp.broadcast_to(x[:, h:h + 1], (x.shape[0], width))


def _pair_expand(x, n_heads):
    rows = x.shape[0]
    lo = lax.broadcasted_iota(jnp.int32, (rows, LANES), 1) < SSD_HEAD_DIM
    return jnp.concatenate(
        [jnp.where(lo, _lane_bcast(x, 2 * j), _lane_bcast(x, 2 * j + 1)) for j in range(n_heads // 2)], axis=1)


def _pad_t(x):
    pad = jnp.zeros((LANES - x.shape[0], x.shape[1]), F32)
    return jnp.concatenate([x, pad], axis=0).T


def _ffn_body(*refs, nf, final):
    if final:
        x_ref, g_ref, wg_ref, wu_ref, wo_ref, fg_ref, o_ref = refs
    else:
        x_ref, g_ref, wg_ref, wu_ref, wo_ref, o_ref = refs
    x = x_ref[...]
    xn = _bf(_rms(x) * g_ref[...])
    acc = None
    for f in range(nf):
        g = _dot(xn, wg_ref[f])
        u = _dot(xn, wu_ref[f])
        t = _dot(_bf(_silu(g) * u), wo_ref[f])
        acc = t if acc is None else acc + t
    y = x + 0.5 * acc
    if final:
        y = _rms(y) * fg_ref[...]
    o_ref[...] = y


def _ffn(x, g3, w_in5, w_out4, layer, final_g=None):
    m = x.shape[0]
    tm = min(m, ROW_TILE)
    nf = D_FF // FF_TILE
    once = pl.Buffered(1)
    in_specs = [
        pl.BlockSpec((tm, D_MODEL), lambda i: (i, 0)),
        pl.BlockSpec((None, 1, D_MODEL), lambda i: (layer, 0, 0)),
        pl.BlockSpec((None, None, nf, D_MODEL, FF_TILE), lambda i: (layer, 0, 0, 0, 0), pipeline_mode=once),
        pl.BlockSpec((None, None, nf, D_MODEL, FF_TILE), lambda i: (layer, 1, 0, 0, 0), pipeline_mode=once),
        pl.BlockSpec((None, nf, FF_TILE, D_MODEL), lambda i: (layer, 0, 0, 0), pipeline_mode=once),
    ]
    args = [x, g3, w_in5, w_in5, w_out4]
    if final_g is not None:
        in_specs.append(pl.BlockSpec((1, D_MODEL), lambda i: (0, 0)))
        args.append(final_g.reshape(1, D_MODEL))
    return pl.pallas_call(
        functools.partial(_ffn_body, nf=nf, final=final_g is not None),
        grid=(m // tm,),
        in_specs=in_specs,
        out_specs=pl.BlockSpec((tm, D_MODEL), lambda i: (i, 0)),
        out_shape=jax.ShapeDtypeStruct((m, D_MODEL), F32),
        compiler_params=_cp(("parallel",), 56),
        name="ffn",
    )(*args)


def _norm_proj_body(*refs, small):
    if small:
        x_ref, g_ref, w_ref, ws_ref, o_ref, os_ref, xn_ref = refs
    else:
        x_ref, g_ref, w_ref, o_ref, xn_ref = refs
    n = pl.program_id(1)

    @pl.when(n == 0)
    def _():
        xn = _bf(_rms(x_ref[...]) * g_ref[...])
        xn_ref[...] = xn
        if small:
            os_ref[...] = _dot(xn, _bf(ws_ref[...]))

    o_ref[...] = _dot(xn_ref[...], _bf(w_ref[...]))


def _norm_proj(x, g3, glayer, w3, wlayer, tn, w_small=None):
    m = x.shape[0]
    tm = min(m, ROW_TILE)
    n_out = w3.shape[-1]
    in_specs = [
        pl.BlockSpec((tm, D_MODEL), lambda i, n: (i, 0)),
        pl.BlockSpec((None, 1, D_MODEL), lambda i, n: (glayer, 0, 0)),
        pl.BlockSpec((None, D_MODEL, tn), lambda i, n: (wlayer, 0, n)),
    ]
    args = [x, g3, w3]
    out_specs = [pl.BlockSpec((tm, tn), lambda i, n: (i, n))]
    out_shape = [jax.ShapeDtypeStruct((m, n_out), F32)]
    if w_small is not None:
        ns = w_small.shape[-1]
        in_specs.append(pl.BlockSpec((None, D_MODEL, ns), lambda i, n: (0, 0, 0)))
        args.append(w_small)
        out_specs.append(pl.BlockSpec((tm, ns), lambda i, n: (i, 0)))
        out_shape.append(jax.ShapeDtypeStruct((m, ns), F32))
    res = pl.pallas_call(
        functools.partial(_norm_proj_body, small=w_small is not None),
        grid=(m // tm, n_out // tn),
        in_specs=in_specs,
        out_specs=out_specs,
        out_shape=out_shape,
        scratch_shapes=[pltpu.VMEM((tm, D_MODEL), BF16)],
        compiler_params=_cp(("parallel", "arbitrary"), 48),
        name="norm_proj",
    )(*args)
    return res if w_small is not None else res[0]


def _proj_res_body(x_ref, y_ref, w_ref, o_ref):
    k = pl.program_id(1)

    @pl.when(k == 0)
    def _():
        o_ref[...] = x_ref[...]

    o_ref[...] += _dot(_bf(y_ref[...]), _bf(w_ref[...]))


def _proj_residual(x, y, w3, layer):
    m = x.shape[0]
    tm = min(m, ROW_TILE)
    kdim = y.shape[1]
    tk = min(kdim, 1024)
    return pl.pallas_call(
        _proj_res_body,
        grid=(m // tm, kdim // tk),
        in_specs=[
            pl.BlockSpec((tm, D_MODEL), lambda i, k: (i, 0)),
            pl.BlockSpec((tm, tk), lambda i, k: (i, k)),
            pl.BlockSpec((None, tk, D_MODEL), lambda i, k: (layer, k, 0)),
        ],
        out_specs=pl.BlockSpec((tm, D_MODEL), lambda i, k: (i, 0)),
        out_shape=jax.ShapeDtypeStruct((m, D_MODEL), F32),
        compiler_params=_cp(("parallel", "arbitrary"), 48),
        name="proj_residual",
    )(x, y, w3)


def _mem_kv_body(x_ref, g_ref, w_ref, k_ref, v_ref):
    xn = _bf(_rms(x_ref[...]) * g_ref[...])
    kv = _dot(xn, _bf(w_ref[...]))
    k_ref[...] = kv[:, :D_MODEL]
    v_ref[...] = kv[:, D_MODEL:]


def _mem_kv(mem2d, g3, w_xkv):
    m = mem2d.shape[0]
    tm = 512
    shp = jax.ShapeDtypeStruct((DEPTH, m, D_MODEL), F32)
    return pl.pallas_call(
        _mem_kv_body,
        grid=(DEPTH, m // tm),
        in_specs=[
            pl.BlockSpec((tm, D_MODEL), lambda l, i: (i, 0)),
            pl.BlockSpec((None, 1, D_MODEL), lambda l, i: (l, 0, 0)),
            pl.BlockSpec((None, D_MODEL, 2 * D_MODEL), lambda l, i: (l, 0, 0)),
        ],
        out_specs=[pl.BlockSpec((None, tm, D_MODEL), lambda l, i: (l, i, 0))] * 2,
        out_shape=[shp, shp],
        compiler_params=_cp(("arbitrary", "arbitrary"), 48),
        name="mem_kv",
    )(mem2d, g3, w_xkv)


def _xattn_prompt_body(x_ref, g_ref, wq_ref, wo_ref, k_ref, v_ref, o_ref):
    x = x_ref[...]
    xn = _bf(_rms(x) * g_ref[...])
    q = _dot(xn, _bf(wq_ref[...]))
    k = _bf(k_ref[...])
    v = _bf(v_ref[...])
    outs = []
    for h in range(X_HEADS):
        sl = slice(h * X_HEAD_DIM, (h + 1) * X_HEAD_DIM)
        s = _dot_nt(_bf(q[:, sl]), k[:, sl]) * (X_HEAD_DIM ** -0.5)
        e = jnp.exp(s - jnp.max(s, axis=-1, keepdims=True))
        p = e / jnp.sum(e, axis=-1, keepdims=True)
        outs.append(_bf(_dot(_bf(p), v[:, sl])))
    o_ref[...] = x + _dot(jnp.concatenate(outs, axis=1), _bf(wo_ref[...]))


def _xattn_prompt(x, g3, w_xq, w_xo, memk, memv, layer):
    tq = 512
    nq = SEQ // tq
    return pl.pallas_call(
        _xattn_prompt_body,
        grid=(BATCH, nq),
        in_specs=[
            pl.BlockSpec((tq, D_MODEL), lambda b, j: (b * nq + j, 0)),
            pl.BlockSpec((None, 1, D_MODEL), lambda b, j: (layer, 0, 0)),
            pl.BlockSpec((None, D_MODEL, D_MODEL), lambda b, j: (layer, 0, 0)),
            pl.BlockSpec((None, D_MODEL, D_MODEL), lambda b, j: (layer, 0, 0)),
            pl.BlockSpec((None, MEM_LEN, D_MODEL), lambda b, j: (layer, b, 0)),
            pl.BlockSpec((None, MEM_LEN, D_MODEL), lambda b, j: (layer, b, 0)),
        ],
        out_specs=pl.BlockSpec((tq, D_MODEL), lambda b, j: (b * nq + j, 0)),
        out_shape=jax.ShapeDtypeStruct((BATCH * SEQ, D_MODEL), F32),
        compiler_params=_cp(("parallel", "arbitrary"), 48),
        name="xattn_prompt",
    )(x, g3, w_xq, w_xo, memk, memv)


def _xattn_sample_body(q_ref, k_ref, v_ref, o_ref):
    for b in range(XB):
        s = jnp.sum(k_ref[b] * q_ref[b][None], axis=-1, keepdims=True) * (X_HEAD_DIM ** -0.5)
        e = jnp.exp(s - jnp.max(s, axis=0, keepdims=True))
        p = e / jnp.sum(e, axis=0, keepdims=True)
        o_ref[b] = jnp.sum(p * v_ref[b], axis=0)


def _xattn_sample(q, cache_k, cache_v, layer):
    blk = pl.BlockSpec((None, XB, MEM_LEN, X_HEADS, X_HEAD_DIM), lambda i: (layer, i, 0, 0, 0))
    qo = pl.BlockSpec((XB, X_HEADS, X_HEAD_DIM), lambda i: (i, 0, 0))
    return pl.pallas_call(
        _xattn_sample_body,
        grid=(DEC_BATCH // XB,),
        in_specs=[qo, blk, blk],
        out_specs=qo,
        out_shape=jax.ShapeDtypeStruct((DEC_BATCH, X_HEADS, X_HEAD_DIM), F32),
        compiler_params=_cp(("parallel",), 48),
        name="xattn_sample",
    )(q, cache_k, cache_v)


def _ret_log_gamma(h):
    return math.log1p(-(2.0 ** (-5.0 - h)))


def _ab_prompt_body(z_ref, rv_ref, rg_ref, xbc_ref, rq_ref, rk_ref, dt_ref, cos_ref, sin_ref,
                    cw_ref, cb_ref, dtb_ref, alog_ref, dsk_ref, nrm_ref,
                    y_ref, conv_ref, h_ref, s_ref, cbuf):
    c = pl.program_id(1)

    @pl.when(c == 0)
    def _():
        cbuf[0:SUBLANES, :] = jnp.zeros((SUBLANES, SSD_CONV_CH), F32)
        h_ref[...] = jnp.zeros_like(h_ref)
        s_ref[...] = jnp.zeros_like(s_ref)

    u = xbc_ref[...]
    cbuf[SUBLANES:SUBLANES + CHUNK, :] = u
    w = cw_ref[...]
    conv = cb_ref[...] + (((cbuf[5:5 + CHUNK, :] * w[0:1, :] + cbuf[6:6 + CHUNK, :] * w[1:2, :])
                           + cbuf[7:7 + CHUNK, :] * w[2:3, :]) + u * w[3:4, :])
    tail = cbuf[CHUNK + 5:CHUNK + 8, :]
    cbuf[5:8, :] = tail
    for j in range(SSD_CONV - 1):
        conv_ref[0:1, j * SSD_CONV_CH:(j + 1) * SSD_CONV_CH] = tail[j:j + 1, :]
    xbc = _silu(conv)
    xs = xbc[:, :SSD_D_INNER]
    bm = xbc[:, SSD_D_INNER:SSD_D_INNER + GROUP_W]
    cm = xbc[:, SSD_D_INNER + GROUP_W:]

    dt = _softplus(dt_ref[...] + dtb_ref[...])
    da = dt * (-jnp.exp(alog_ref[...]))
    cs = _cumsum_rows(da)
    cs_t = cs.T
    row = lax.broadcasted_iota(jnp.int32, (CHUNK, CHUNK), 0)
    col = lax.broadcasted_iota(jnp.int32, (CHUNK, CHUNK), 1)
    tri = row >= col
    lo = col < SSD_HEAD_DIM
    for g in range(SSD_GROUPS):
        gs = slice(g * GROUP_W, (g + 1) * GROUP_W)
        ns = slice(g * SSD_D_STATE, (g + 1) * SSD_D_STATE)
        cmg = _bf(cm[:, ns])
        bmg = _bf(bm[:, ns])
        att = _dot_nt(cmg, bmg)
        hprev = h_ref[gs, :]
        yint = _dot_nt(cmg, _bf(hprev))
        ys, wxs, css = [], [], []
        for j in range(HEADS_PER_GROUP // 2):
            h0 = g * HEADS_PER_GROUP + 2 * j
            cb0 = _lane_bcast(cs, h0)
            cb1 = _lane_bcast(cs, h0 + 1)
            cs_p = jnp.where(lo, cb0, cb1)
            dt_p = jnp.where(lo, _lane_bcast(dt, h0), _lane_bcast(dt, h0 + 1))
            off = g * GROUP_W + j * LANES
            xdt = xs[:, off:off + LANES] * dt_p
            d0 = jnp.exp(jnp.where(tri, cb0 - cs_t[h0:h0 + 1, :], -jnp.inf))
            d1 = jnp.exp(jnp.where(tri, cb1 - cs_t[h0 + 1:h0 + 2, :], -jnp.inf))
            yy = _dot(jnp.concatenate([_bf(att * d0), _bf(att * d1)], axis=0), _bf(xdt))
            ys.append(jnp.where(lo, yy[:CHUNK], yy[CHUNK:]) + yint[:, j * LANES:(j + 1) * LANES] * jnp.exp(cs_p))
            wxs.append(_bf(xdt * jnp.exp(cs_p[CHUNK - 1:CHUNK, :] - cs_p)))
            css.append(cs_p)
        cs_g = jnp.concatenate(css, axis=1)
        last_t = jnp.broadcast_to(cs_g[CHUNK - 1:CHUNK, :], (CHUNK, GROUP_W)).T
        h_ref[gs, :] = hprev * jnp.exp(last_t) + _dot_tn(jnp.concatenate(wxs, axis=1), bmg)
        yg = jnp.concatenate(ys, axis=1)
        yg = (yg + xs[:, gs] * dsk_ref[:, gs]) * _silu(z_ref[:, gs])
        y_ref[:, gs] = _bf(_rms(yg) * nrm_ref[:, gs])

    cos = cos_ref[...]
    sin = sin_ref[...]
    tcol = row.astype(F32)
    diff = tcol - col.astype(F32)
    for h in range(RET_HEADS):
        lg = _ret_log_gamma(h)
        ks = slice(h * RET_DK, (h + 1) * RET_DK)
        vs = slice(h * RET_DV, (h + 1) * RET_DV)

        def rot(ref):
            x1 = ref[:, h * RET_DK:h * RET_DK + LANES]
            x2 = ref[:, h * RET_DK + LANES:(h + 1) * RET_DK]
            return jnp.concatenate([x1 * cos - x2 * sin, x1 * sin + x2 * cos], axis=1)

        qr = rot(rq_ref)
        kr = rot(rk_ref) * (RET_DK ** -0.5)
        qb = _bf(qr)
        decay = jnp.exp(jnp.where(tri, diff * lg, -jnp.inf))
        att = _dot_nt(qb, _bf(kr)) * decay
        vb = _bf(rv_ref[:, vs])
        s_prev = s_ref[h]
        inner = jnp.exp((tcol + 1.0) * lg)
        r = _dot(_bf(att), vb) + _dot(qb, _bf(s_prev)) * jnp.concatenate([inner] * (RET_DV // LANES), axis=1)
        tail_w = jnp.exp((CHUNK - 1.0 - tcol) * lg)
        kt = _bf(kr * jnp.concatenate([tail_w] * (RET_DK // LANES), axis=1))
        s_ref[h] = s_prev * math.exp(CHUNK * lg) + _dot_tn(kt, vb)
        os = slice(SSD_D_INNER + h * RET_DV, SSD_D_INNER + (h + 1) * RET_DV)
        y_ref[:, os] = _bf(_silu(rg_ref[:, vs]) * _rms(r))


def _mixer_ab_prompt(p_main, p_dt, cos, sin, conv_w, conv_b, dt_bias, a_log, d_skip_e, ssd_norm):
    nc = SEQ // CHUNK
    m = BATCH * SEQ

    def rowspec(width, cb):
        return pl.BlockSpec((CHUNK, width), lambda b, c: (b * nc + c, cb))

    def full(a):
        return pl.BlockSpec(a.shape, lambda b, c: (0,) * a.ndim)

    params = [conv_w, conv_b, dt_bias, a_log, d_skip_e, ssd_norm]
    return pl.pallas_call(
        _ab_prompt_body,
        grid=(BATCH, nc),
        in_specs=[rowspec(SSD_D_INNER, 0), rowspec(RET_V, 1), rowspec(RET_V, 2), rowspec(SSD_CONV_CH, 2),
                  rowspec(RET_QK, 9), rowspec(RET_QK, 10), rowspec(LANES, 0),
                  pl.BlockSpec((CHUNK, LANES), lambda b, c: (c, 0)),
                  pl.BlockSpec((CHUNK, LANES), lambda b, c: (c, 0))] + [full(a) for a in params],
        out_specs=[
            pl.BlockSpec((CHUNK, SSD_D_INNER + RET_V), lambda b, c: (b * nc + c, 0)),
            pl.BlockSpec((None, 1, (SSD_CONV - 1) * SSD_CONV_CH), lambda b, c: (b, 0, 0)),
            pl.BlockSpec((None, SSD_D_INNER, SSD_D_STATE), lambda b, c: (b, 0, 0)),
            pl.BlockSpec((None, RET_HEADS, RET_DK, RET_DV), lambda b, c: (b, 0, 0, 0)),
        ],
        out_shape=[
            jax.ShapeDtypeStruct((m, SSD_D_INNER + RET_V), BF16),
            jax.ShapeDtypeStruct((BATCH, 1, (SSD_CONV - 1) * SSD_CONV_CH), F32),
            jax.ShapeDtypeStruct((BATCH, SSD_D_INNER, SSD_D_STATE), F32),
            jax.ShapeDtypeStruct((BATCH, RET_HEADS, RET_DK, RET_DV), F32),
        ],
        scratch_shapes=[pltpu.VMEM((SUBLANES + CHUNK, SSD_CONV_CH), F32)],
        compiler_params=_cp(("parallel", "arbitrary"), 56),
        name="mixer_ab_prompt",
    )(p_main, p_main, p_main, p_main, p_main, p_main, p_dt, cos, sin, *params)


def _c_prompt_body(q_ref, k_ref, v_ref, o_ref, gt_ref, ib_ref, fb_ref, nrm_ref,
                   h_ref, c_ref, n_ref, m_ref, m_s):
    c = pl.program_id(1)

    @pl.when(c == 0)
    def _():
        c_ref[...] = jnp.zeros_like(c_ref)
        n_ref[...] = jnp.zeros_like(n_ref)
        m_s[...] = jnp.zeros_like(m_s)

    ipre = gt_ref[:, :LANES] + ib_ref[...]
    lf = -_softplus(-(gt_ref[:, LANES:] + fb_ref[...]))
    b = _cumsum_rows(lf)
    g = ipre - b
    g_t = g.T
    b_t = b.T
    row = lax.broadcasted_iota(jnp.int32, (CHUNK, CHUNK), 0)
    col = lax.broadcasted_iota(jnp.int32, (CHUNK, CHUNK), 1)
    tri = row >= col
    cmax = g_t
    sh = 1
    while sh < CHUNK:
        cmax = jnp.maximum(cmax, jnp.where(col >= sh, pltpu.roll(cmax, sh, axis=1), -jnp.inf))
        sh *= 2
    m_prev = m_s[...]
    mt_t = b_t + jnp.maximum(m_prev, cmax)
    mt = mt_t.T
    m_prev_c = m_prev.T
    inter = jnp.exp(b + m_prev_c - mt)
    emt = jnp.exp(-mt)
    wl = jnp.exp(g + b[CHUNK - 1:CHUNK, :] - mt[CHUNK - 1:CHUNK, :])
    bm = b - mt
    m_new = _lane_bcast(mt_t, CHUNK - 1)
    dp_t = jnp.exp(_lane_bcast(b_t, CHUNK - 1) + m_prev - m_new)
    m_s[...] = m_new
    m_ref[...] = m_new[0:SUBLANES, :]
    outs = []
    for h in range(M_HEADS):
        ks = slice(h * M_DK, (h + 1) * M_DK)
        vs = slice(h * M_DV, (h + 1) * M_DV)
        wgt = jnp.exp(jnp.where(tri, g_t[h:h + 1, :] + bm[:, h:h + 1], -jnp.inf))
        qh = q_ref[:, ks]
        kh = k_ref[:, ks] * (M_DK ** -0.5)
        qb = _bf(qh)
        vb = _bf(v_ref[:, vs])
        a = _dot_nt(qb, _bf(kh)) * wgt
        c_prev = c_ref[h]
        n_prev = n_ref[h:h + 1, :]
        ic = inter[:, h:h + 1]
        num = _dot(_bf(a), vb) + _dot(qb, _bf(c_prev)) * ic
        den = jnp.sum(a, axis=1, keepdims=True) + jnp.sum(qh * n_prev, axis=1, keepdims=True) * ic
        outs.append(num / jnp.maximum(jnp.abs(den), emt[:, h:h + 1]))
        kw = kh * wl[:, h:h + 1]
        dp_row = dp_t[h:h + 1, :]
        c_ref[h] = c_prev * jnp.concatenate([dp_row] * (M_DV // LANES), axis=1) + _dot_tn(_bf(kw), vb)
        n_ref[h:h + 1, :] = n_prev * dp_row + jnp.sum(kw, axis=0, keepdims=True)
    for h in range(M_HEADS):
        vs = slice(h * M_DV, (h + 1) * M_DV)
        h_ref[:, vs] = _bf(jax.nn.sigmoid(o_ref[:, vs]) * (_rms(outs[h]) * nrm_ref[:, vs]))


def _mixer_c_prompt(p_main, p_gate, i_bias, f_bias, norm_g):
    nc = SEQ // CHUNK
    m = BATCH * SEQ

    def rowspec(width, cb):
        return pl.BlockSpec((CHUNK, width), lambda b, c: (b * nc + c, cb))

    def full(a):
        return pl.BlockSpec(a.shape, lambda b, c: (0,) * a.ndim)

    params = [i_bias, f_bias, norm_g]
    return pl.pallas_call(
        _c_prompt_body,
        grid=(BATCH, nc),
        in_specs=[rowspec(M_QK, 0), rowspec(M_QK, 1), rowspec(M_V, 1), rowspec(M_V, 2), rowspec(2 * LANES, 0)]
        + [full(a) for a in params],
        out_specs=[
            pl.BlockSpec((CHUNK, M_V), lambda b, c: (b * nc + c, 0)),
            pl.BlockSpec((None, M_HEADS, M_DK, M_DV), lambda b, c: (b, 0, 0, 0)),
            pl.BlockSpec((None, M_HEADS, M_DK), lambda b, c: (b, 0, 0)),
            pl.BlockSpec((None, SUBLANES, LANES), lambda b, c: (b, 0, 0)),
        ],
        out_shape=[
            jax.ShapeDtypeStruct((m, M_V), BF16),
            jax.ShapeDtypeStruct((BATCH, M_HEADS, M_DK, M_DV), F32),
            jax.ShapeDtypeStruct((BATCH, M_HEADS, M_DK), F32),
            jax.ShapeDtypeStruct((BATCH, SUBLANES, LANES), F32),
        ],
        scratch_shapes=[pltpu.VMEM((CHUNK, LANES), F32)],
        compiler_params=_cp(("parallel", "arbitrary"), 48),
        name="mixer_c_prompt",
    )(p_main, p_main, p_main, p_main, p_gate, *params)


def _ab_sample_prep_body(xbc_ref, rq_ref, rk_ref, dt_ref, cst_ref, cos_ref, sin_ref,
                         cw_ref, cb_ref, dtb_ref, alog_ref,
                         conv_ref, xs_ref, xdt_ref, eda_ref, bm_ref, cm_ref, q_ref, k_ref, gam_ref):
    ch = SSD_CONV_CH
    u = xbc_ref[...]
    w = cw_ref[...]
    b0 = cst_ref[:, 0:ch]
    b1 = cst_ref[:, ch:2 * ch]
    b2 = cst_ref[:, 2 * ch:3 * ch]
    conv = cb_ref[...] + (((b0 * w[0:1, :] + b1 * w[1:2, :]) + b2 * w[2:3, :]) + u * w[3:4, :])
    conv_ref[:, 0:ch] = b1
    conv_ref[:, ch:2 * ch] = b2
    conv_ref[:, 2 * ch:3 * ch] = u
    xbc = _silu(conv)
    xs = xbc[:, :SSD_D_INNER]
    xs_ref[...] = xs
    bm_ref[...] = xbc[:, SSD_D_INNER:SSD_D_INNER + GROUP_W]
    cm_ref[...] = xbc[:, SSD_D_INNER + GROUP_W:]
    dt = _softplus(dt_ref[...] + dtb_ref[...])
    eda = jnp.exp(dt * (-jnp.exp(alog_ref[...])))
    xdt_ref[...] = xs * _pair_expand(dt, SSD_HEADS)
    eda_ref[...] = _pair_expand(eda, SSD_HEADS)
    cos = cos_ref[...]
    sin = sin_ref[...]
    for h in range(RET_HEADS):
        a = slice(h * RET_DK, h * RET_DK + LANES)
        b = slice(h * RET_DK + LANES, (h + 1) * RET_DK)
        q1, q2 = rq_ref[:, a], rq_ref[:, b]
        k1, k2 = rk_ref[:, a], rk_ref[:, b]
        q_ref[:, a] = q1 * cos - q2 * sin
        q_ref[:, b] = q1 * sin + q2 * cos
        k_ref[:, a] = (k1 * cos - k2 * sin) * (RET_DK ** -0.5)
        k_ref[:, b] = (k1 * sin + k2 * cos) * (RET_DK ** -0.5)
        gam_ref[:, h * RET_DK:(h + 1) * RET_DK] = jnp.full((DEC_BATCH, RET_DK), math.exp(_ret_log_gamma(h)), F32)


def _ab_sample_prep(p_main, p_dt, conv_state, cos, sin, conv_w, conv_b, dt_bias, a_log):
    n = DEC_BATCH

    def colspec(width, cb):
        return pl.BlockSpec((n, width), lambda i: (0, cb))

    def full(a):
        return pl.BlockSpec(a.shape, lambda i: (0,) * a.ndim)

    small = [conv_state, cos, sin, conv_w, conv_b, dt_bias, a_log]

    def out(width):
        return jax.ShapeDtypeStruct((n, width), F32)

    widths = [(SSD_CONV - 1) * SSD_CONV_CH, SSD_D_INNER, SSD_D_INNER, SSD_D_INNER, GROUP_W, GROUP_W,
              RET_QK, RET_QK, RET_QK]
    return pl.pallas_call(
        _ab_sample_prep_body,
        grid=(1,),
        in_specs=[colspec(SSD_CONV_CH, 2), colspec(RET_QK, 9), colspec(RET_QK, 10), full(p_dt)]
        + [full(a) for a in small],
        out_specs=[pl.BlockSpec((n, wd), lambda i: (0, 0)) for wd in widths],
        out_shape=[out(wd) for wd in widths],
        compiler_params=_cp(("arbitrary",), 48),
        name="ab_sample_prep",
    )(p_main, p_main, p_main, p_dt, *small)


def _ssd_state_body(eda_ref, xdt_ref, bm_ref, cm_ref, h_ref, ho_ref, y_ref):
    eda_t = _pad_t(eda_ref[...])
    xdt_t = _pad_t(xdt_ref[...])
    lane = lax.broadcasted_iota(jnp.int32, (GROUP_W, LANES), 1)
    ycols = jnp.zeros((GROUP_W, LANES), F32)
    for b in range(SB):
        hn = h_ref[b] * eda_t[:, b:b + 1] + xdt_t[:, b:b + 1] * bm_ref[b:b + 1, :]
        ho_ref[b] = hn
        ycols = jnp.where(lane == b, jnp.sum(hn * cm_ref[b:b + 1, :], axis=1, keepdims=True), ycols)
    y_ref[...] = ycols.T[0:SB, :]


def _ssd_state(eda, xdt, bm, cm, state):
    vec = pl.BlockSpec((SB, GROUP_W), lambda i, g: (i, g))
    bc = pl.BlockSpec((SB, SSD_D_STATE), lambda i, g: (i, g))
    st = pl.BlockSpec((SB, GROUP_W, SSD_D_STATE), lambda i, g: (i, g, 0))
    return pl.pallas_call(
        _ssd_state_body,
        grid=(DEC_BATCH // SB, SSD_GROUPS),
        in_specs=[vec, vec, bc, bc, st],
        out_specs=[st, vec],
        out_shape=[jax.ShapeDtypeStruct(state.shape, F32), jax.ShapeDtypeStruct((DEC_BATCH, SSD_D_INNER), F32)],
        compiler_params=_cp(("parallel", "arbitrary"), 48),
        name="ssd_state",
    )(eda, xdt, bm, cm, state)


def _outer_state_body(d_ref, k_ref, q_ref, v_ref, s_ref, so_ref, o_ref):
    d_t = _pad_t(d_ref[...])
    k_t = _pad_t(k_ref[...])
    q_t = _pad_t(q_ref[...])
    for b in range(SB):
        sn = s_ref[b] * d_t[:, b:b + 1] + k_t[:, b:b + 1] * v_ref[b:b + 1, :]
        so_ref[b] = sn
        o_ref[b:b + 1, :] = jnp.sum(sn * q_t[:, b:b + 1], axis=0, keepdims=True)


def _outer_state(d, k, q, v, state):
    _, nh, dk, dv = state.shape
    kv = pl.BlockSpec((SB, dk), lambda i, h: (i, h))
    vv = pl.BlockSpec((SB, dv), lambda i, h: (i, h))
    st = pl.BlockSpec((SB, None, dk, dv), lambda i, h: (i, h, 0, 0))
    return pl.pallas_call(
        _outer_state_body,
        grid=(DEC_BATCH // SB, nh),
        in_specs=[kv, kv, kv, vv, st],
        out_specs=[st, vv],
        out_shape=[jax.ShapeDtypeStruct(state.shape, F32), jax.ShapeDtypeStruct((DEC_BATCH, nh * dv), F32)],
        compiler_params=_cp(("parallel", "arbitrary"), 48),
        name="outer_state",
    )(d, k, q, v, state)


def _ab_sample_post_body(y_ref, xs_ref, z_ref, r_ref, rg_ref, dsk_ref, nrm_ref, o_ref):
    for g in range(SSD_GROUPS):
        gs = slice(g * GROUP_W, (g + 1) * GROUP_W)
        yg = (y_ref[:, gs] + xs_ref[:, gs] * dsk_ref[:, gs]) * _silu(z_ref[:, gs])
        o_ref[:, gs] = _bf(_rms(yg) * nrm_ref[:, gs])
    for h in range(RET_HEADS):
        vs = slice(h * RET_DV, (h + 1) * RET_DV)
        os = slice(SSD_D_INNER + h * RET_DV, SSD_D_INNER + (h + 1) * RET_DV)
        o_ref[:, os] = _bf(_silu(rg_ref[:, vs]) * _rms(r_ref[:, vs]))


def _ab_sample_post(y, xs, p_main, r, d_skip_e, ssd_norm):
    n = DEC_BATCH

    def full(a):
        return pl.BlockSpec(a.shape, lambda i: (0,) * a.ndim)

    return pl.pallas_call(
        _ab_sample_post_body,
        grid=(1,),
        in_specs=[full(y), full(xs), pl.BlockSpec((n, SSD_D_INNER), lambda i: (0, 0)), full(r),
                  pl.BlockSpec((n, RET_V), lambda i: (0, 2)), full(d_skip_e), full(ssd_norm)],
        out_specs=pl.BlockSpec((n, SSD_D_INNER + RET_V), lambda i: (0, 0)),
        out_shape=jax.ShapeDtypeStruct((n, SSD_D_INNER + RET_V), BF16),
        compiler_params=_cp(("arbitrary",), 48),
        name="ab_sample_post",
    )(y, xs, p_main, r, p_main, d_skip_e, ssd_norm)


def _c_sample_prep_body(q_ref, k_ref, gt_ref, n_ref, m_ref, ib_ref, fb_ref,
                        dpe_ref, kw_ref, nn_ref, mn_ref, dn_ref):
    ipre = gt_ref[:, :LANES] + ib_ref[...]
    lf = -_softplus(-(gt_ref[:, LANES:] + fb_ref[...]))
    m_prev = m_ref[...]
    mt = jnp.maximum(lf + m_prev, ipre)
    wgt = jnp.exp(ipre - mt)
    dp = jnp.exp(lf + m_prev - mt)
    emt = jnp.exp(-mt)
    mn_ref[...] = mt
    for h in range(M_HEADS):
        ks = slice(h * M_DK, (h + 1) * M_DK)
        dpe = _lane_bcast(dp, h)
        kw = k_ref[:, ks] * (M_DK ** -0.5) * _lane_bcast(wgt, h)
        nn = n_ref[:, ks] * dpe + kw
        den = jnp.sum(nn * q_ref[:, ks], axis=1, keepdims=True)
        dpe_ref[:, ks] = dpe
        kw_ref[:, ks] = kw
        nn_ref[:, ks] = nn
        dn_ref[:, h * M_DV:(h + 1) * M_DV] = jnp.broadcast_to(
            jnp.maximum(jnp.abs(den), emt[:, h:h + 1]), (DEC_BATCH, M_DV))


def _c_sample_prep(p_main, p_gate, n_state, m_state, i_bias, f_bias):
    n = DEC_BATCH

    def full(a):
        return pl.BlockSpec(a.shape, lambda i: (0,) * a.ndim)

    widths = [M_QK, M_QK, M_QK, LANES, M_V]
    return pl.pallas_call(
        _c_sample_prep_body,
        grid=(1,),
        in_specs=[pl.BlockSpec((n, M_QK), lambda i: (0, 0)), pl.BlockSpec((n, M_QK), lambda i: (0, 1)),
                  full(p_gate), full(n_state), full(m_state), full(i_bias), full(f_bias)],
        out_specs=[pl.BlockSpec((n, wd), lambda i: (0, 0)) for wd in widths],
        out_shape=[jax.ShapeDtypeStruct((n, wd), F32) for wd in widths],
        compiler_params=_cp(("arbitrary",), 48),
        name="c_sample_prep",
    )(p_main, p_main, p_gate, n_state, m_state, i_bias, f_bias)


def _c_sample_post_body(num_ref, dn_ref, o_ref, nrm_ref, h_ref):
    for h in range(M_HEADS):
        vs = slice(h * M_DV, (h + 1) * M_DV)
        hc = num_ref[:, vs] / dn_ref[:, vs]
        h_ref[:, vs] = _bf(jax.nn.sigmoid(o_ref[:, vs]) * (_rms(hc) * nrm_ref[:, vs]))


def _c_sample_post(num, den, p_main, norm_g):
    n = DEC_BATCH

    def full(a):
        return pl.BlockSpec(a.shape, lambda i: (0,) * a.ndim)

    return pl.pallas_call(
        _c_sample_post_body,
        grid=(1,),
        in_specs=[full(num), full(den), pl.BlockSpec((n, M_V), lambda i: (0, 2)), full(norm_g)],
        out_specs=pl.BlockSpec((n, M_V), lambda i: (0, 0)),
        out_shape=jax.ShapeDtypeStruct((n, M_V), BF16),
        compiler_params=_cp(("arbitrary",), 48),
        name="c_sample_post",
    )(num, den, p_main, norm_g)


def _rope_tables(pos):
    half = RET_DK // 2
    inv = jnp.exp(-math.log(ROPE_BASE) * jnp.arange(half, dtype=F32) / half)
    ang = pos.astype(F32)[:, None] * inv
    return jnp.cos(ang), jnp.sin(ang)


def _pad_lanes(v, width=LANES):
    return jnp.pad(v.reshape(1, -1), ((0, 0), (0, width - v.size)))


def kernel(x_prompt, x_sample, cache_mem_k, cache_mem_v, state_conv, state_ssm, state_ret, state_mlstm_c, state_mlstm_n, state_mlstm_m, mem_prompt, norm_ffn1, w_ffn1_in, w_ffn1_out, norm_mix, w_in_ab, ssd_conv_w, ssd_conv_b, ssd_dt_bias, ssd_a_log, ssd_d, ssd_norm, w_out_ab, w_in_c, mlstm_i_bias, mlstm_f_bias, mlstm_norm, w_out_c, norm_xattn, norm_mem, w_xq, w_xkv, w_xo, norm_ffn2, w_ffn2_in, w_ffn2_out, norm_final):
    g3 = lambda g: g.reshape(DEPTH, 1, D_MODEL)
    n_ffn1, n_mix, n_x, n_mem, n_ffn2 = g3(norm_ffn1), g3(norm_mix), g3(norm_xattn), g3(norm_mem), g3(norm_ffn2)

    def ffn_w(w_in, w_out):
        nf = D_FF // FF_TILE
        w_in5 = w_in.astype(BF16).reshape(DEPTH, D_MODEL, 2, nf, FF_TILE).transpose(0, 2, 3, 1, 4)
        return w_in5, w_out.astype(BF16).reshape(DEPTH, nf, FF_TILE, D_MODEL)

    f1_in, f1_out = ffn_w(w_ffn1_in, w_ffn1_out)
    f2_in, f2_out = ffn_w(w_ffn2_in, w_ffn2_out)
    w_out_ab, w_out_c, w_xq, w_xkv, w_xo = (w.astype(BF16) for w in (w_out_ab, w_out_c, w_xq, w_xkv, w_xo))

    wz, wxbc, wdt, wrq, wrk, wrv, wrg = jnp.split(w_in_ab[0], np_cumsum(AB_SIZES), axis=1)
    w_ab_main = jnp.concatenate([wz, wrv, wrg, wxbc, wrq, wrk], axis=1).astype(BF16)[None]
    w_ab_dt = jnp.pad(wdt, ((0, 0), (0, LANES - SSD_HEADS))).astype(BF16)[None]
    wq, wk, wv, wi, wf, wo = jnp.split(w_in_c[0], np_cumsum(C_SIZES), axis=1)
    w_c_main = jnp.concatenate([wq, wk, wv, wo], axis=1).astype(BF16)[None]
    gpad = ((0, 0), (0, LANES - M_HEADS))
    w_c_gate = jnp.concatenate([jnp.pad(wi, gpad), jnp.pad(wf, gpad)], axis=1).astype(BF16)[None]

    conv_w = ssd_conv_w[0]
    conv_b = ssd_conv_b.reshape(1, SSD_CONV_CH)
    dt_bias = _pad_lanes(ssd_dt_bias[0])
    a_log = _pad_lanes(ssd_a_log[0])
    d_skip_e = jnp.repeat(ssd_d[0], SSD_HEAD_DIM).reshape(1, SSD_D_INNER)
    s_norm = ssd_norm.reshape(1, SSD_D_INNER)
    i_bias = _pad_lanes(mlstm_i_bias[0])
    f_bias = _pad_lanes(mlstm_f_bias[0])
    m_norm = mlstm_norm.reshape(1, M_V)

    memk, memv = _mem_kv(mem_prompt.reshape(BATCH * MEM_LEN, D_MODEL), n_mem, w_xkv)
    cos_p, sin_p = _rope_tables(jnp.arange(SEQ))
    x = x_prompt.reshape(BATCH * SEQ, D_MODEL)
    x = _ffn(x, n_ffn1, f1_in, f1_out, 0)
    p_main, p_dt = _norm_proj(x, n_mix, 0, w_ab_main, 0, 1024, w_ab_dt)
    ycat, conv_p, ssm_p, ret_p = _mixer_ab_prompt(p_main, p_dt, cos_p, sin_p, conv_w, conv_b, dt_bias, a_log,
                                                  d_skip_e, s_norm)
    x = _proj_residual(x, ycat, w_out_ab, 0)
    x = _xattn_prompt(x, n_x, w_xq, w_xo, memk, memv, 0)
    x = _ffn(x, n_ffn2, f2_in, f2_out, 0)
    x = _ffn(x, n_ffn1, f1_in, f1_out, 1)
    pc_main, pc_gate = _norm_proj(x, n_mix, 1, w_c_main, 0, 1024, w_c_gate)
    hout, mc_p, mn_p, mm_p = _mixer_c_prompt(pc_main, pc_gate, i_bias, f_bias, m_norm)
    x = _proj_residual(x, hout, w_out_c, 0)
    x = _xattn_prompt(x, n_x, w_xq, w_xo, memk, memv, 1)
    y_prompt = _ffn(x, n_ffn2, f2_in, f2_out, 1, norm_final).reshape(BATCH, SEQ, D_MODEL)

    cos_s, sin_s = _rope_tables(PAST_LEN + jnp.arange(1))

    def xattn_s(xs_, layer):
        q = _norm_proj(xs_, n_x, layer, w_xq, layer, 1024).reshape(DEC_BATCH, X_HEADS, X_HEAD_DIM)
        o = _xattn_sample(q, cache_mem_k, cache_mem_v, layer).reshape(DEC_BATCH, D_MODEL)
        return _proj_residual(xs_, o, w_xo, layer)

    xs_ = x_sample.reshape(DEC_BATCH, D_MODEL)
    xs_ = _ffn(xs_, n_ffn1, f1_in, f1_out, 0)
    sp_main, sp_dt = _norm_proj(xs_, n_mix, 0, w_ab_main, 0, 1024, w_ab_dt)
    conv_s, xs_c, xdt, eda, bm_s, cm_s, q_s, k_s, gam = _ab_sample_prep(
        sp_main, sp_dt, state_conv.reshape(DEC_BATCH, (SSD_CONV - 1) * SSD_CONV_CH), cos_s, sin_s,
        conv_w, conv_b, dt_bias, a_log)
    ssm_s, y_s = _ssd_state(eda, xdt, bm_s, cm_s, state_ssm.reshape(DEC_BATCH, SSD_D_INNER, SSD_D_STATE))
    ret_s, r_s = _outer_state(gam, k_s, q_s, sp_main[:, SSD_D_INNER:SSD_D_INNER + RET_V], state_ret[0])
    ycat_s = _ab_sample_post(y_s, xs_c, sp_main, r_s, d_skip_e, s_norm)
    xs_ = _proj_residual(xs_, ycat_s, w_out_ab, 0)
    xs_ = xattn_s(xs_, 0)
    xs_ = _ffn(xs_, n_ffn2, f2_in, f2_out, 0)
    xs_ = _ffn(xs_, n_ffn1, f1_in, f1_out, 1)
    sc_main, sc_gate = _norm_proj(xs_, n_mix, 1, w_c_main, 0, 1024, w_c_gate)
    m_in = jnp.pad(state_mlstm_m[0], ((0, 0), (0, LANES - M_HEADS)))
    dpe, kw, mn_s, mm_s, den = _c_sample_prep(sc_main, sc_gate, state_mlstm_n.reshape(DEC_BATCH, M_QK), m_in,
                                              i_bias, f_bias)
    mc_s, num = _outer_state(dpe, kw, sc_main[:, :M_QK], sc_main[:, 2 * M_QK:2 * M_QK + M_V], state_mlstm_c[0])
    hout_s = _c_sample_post(num, den, sc_main, m_norm)
    xs_ = _proj_residual(xs_, hout_s, w_out_c, 0)
    xs_ = xattn_s(xs_, 1)
    y_sample = _ffn(xs_, n_ffn2, f2_in, f2_out, 1, norm_final).reshape(DEC_BATCH, 1, D_MODEL)

    kv_shape = (DEPTH, BATCH, MEM_LEN, X_HEADS, X_HEAD_DIM)
    return (y_prompt, y_sample, memk.reshape(kv_shape), memv.reshape(kv_shape),
            conv_p.reshape(1, BATCH, SSD_CONV - 1, SSD_CONV_CH),
            conv_s.reshape(1, DEC_BATCH, SSD_CONV - 1, SSD_CONV_CH),
            ssm_p.reshape(1, BATCH, SSD_HEADS, SSD_HEAD_DIM, SSD_D_STATE),
            ssm_s.reshape(1, DEC_BATCH, SSD_HEADS, SSD_HEAD_DIM, SSD_D_STATE),
            ret_p[None], ret_s[None], mc_p[None], mc_s[None],
            mn_p[None], mn_s.reshape(1, DEC_BATCH, M_HEADS, M_DK),
            mm_p[:, :M_HEADS, 0][None], mm_s[:, :M_HEADS][None])


def np_cumsum(sizes):
    out, acc = [], 0
    for s in sizes[:-1]:
        acc += s
        out.append(acc)
    return out
```

```python
import functools
import math

import jax
import jax.numpy as jnp
from jax import lax
from jax.experimental import pallas as pl
from jax.experimental.pallas import tpu as pltpu

F32 = jnp.float32
BF16 = jnp.bfloat16
EPS = 1e-6

D_MODEL = 1024
BATCH = 8
SEQ = 2048
DEPTH = 2
DEC_BATCH = 128
PAST_LEN = 16384
CHUNK = 128
D_FF = 2816
SSD_D_INNER = 2 * D_MODEL
SSD_HEAD_DIM = 64
SSD_HEADS = SSD_D_INNER // SSD_HEAD_DIM
SSD_GROUPS = 4
SSD_D_STATE = 128
SSD_CONV = 4
SSD_CONV_CH = SSD_D_INNER + 2 * SSD_GROUPS * SSD_D_STATE
RET_HEADS = 4
RET_QK = D_MODEL
RET_V = 2 * D_MODEL
RET_DK = RET_QK // RET_HEADS
RET_DV = RET_V // RET_HEADS
ROPE_BASE = 10000.0
AB_SIZES = (SSD_D_INNER, SSD_CONV_CH, SSD_HEADS, RET_QK, RET_QK, RET_V, RET_V)
M_HEADS = 4
M_QK = D_MODEL // 2
M_V = D_MODEL
M_DK = M_QK // M_HEADS
M_DV = M_V // M_HEADS
C_SIZES = (M_QK, M_QK, M_V, M_HEADS, M_HEADS, M_V)
MEM_LEN = 256
X_HEADS = 4
X_HEAD_DIM = D_MODEL // X_HEADS

LANES = 128
SUBLANES = 8
GROUP_W = SSD_D_INNER // SSD_GROUPS
HEADS_PER_GROUP = SSD_HEADS // SSD_GROUPS
ROW_TILE = 1024
FF_TILE = 256
SB = 8
XB = 4
CB = 2


def _cp(sem, mib):
    return pltpu.CompilerParams(dimension_semantics=sem, vmem_limit_bytes=mib * 1024 * 1024)


def _bf(x):
    return x.astype(BF16)


def _dot(a, b):
    return jnp.dot(a, b, preferred_element_type=F32)


def _dot_nt(a, b):
    return lax.dot_general(a, b, (((1,), (1,)), ((), ())), preferred_element_type=F32)


def _dot_tn(a, b):
    return lax.dot_general(a, b, (((0,), (0,)), ((), ())), preferred_element_type=F32)


def _rms(x):
    return x * lax.rsqrt(jnp.mean(x * x, axis=-1, keepdims=True) + EPS)


def _silu(x):
    return x * jax.nn.sigmoid(x)


def _softplus(x):
    return jnp.maximum(x, 0.0) + jnp.log1p(jnp.exp(-jnp.abs(x)))


def _split3(x):
    hi = x.astype(BF16)
    r = x - hi.astype(F32)
    mid = r.astype(BF16)
    lo = (r - mid.astype(F32)).astype(BF16)
    return hi, mid, lo


def _cumsum_rows(x):
    n = x.shape[0]
    r = lax.broadcasted_iota(jnp.int32, (n, n), 0)
    c = lax.broadcasted_iota(jnp.int32, (n, n), 1)
    t = jnp.where(r >= c, 1.0, 0.0).astype(BF16)
    hi, mid, lo = _split3(x)
    return _dot(t, hi) + _dot(t, mid) + _dot(t, lo)


def _lane_bcast(x, h, width=LANES):
    return jnp.broadcast_to(x[:, h:h + 1], (x.shape[0], width))


def _pair_expand(x, n_heads):
    rows = x.shape[0]
    lo = lax.broadcasted_iota(jnp.int32, (rows, LANES), 1) < SSD_HEAD_DIM
    return jnp.concatenate(
        [jnp.where(lo, _lane_bcast(x, 2 * j), _lane_bcast(x, 2 * j + 1)) for j in range(n_heads // 2)], axis=1)


def _pad_t(x):
    pad = jnp.zeros((LANES - x.shape[0], x.shape[1]), F32)
    return jnp.concatenate([x, pad], axis=0).T


def _ffn_body(*refs, nf, final):
    if final:
        x_ref, g_ref, wg_ref, wu_ref, wo_ref, fg_ref, o_ref = refs
    else:
        x_ref, g_ref, wg_ref, wu_ref, wo_ref, o_ref = refs
    x = x_ref[...]
    xn = _bf(_rms(x) * g_ref[...])
    acc = None
    for f in range(nf):
        g = _dot(xn, wg_ref[f])
        u = _dot(xn, wu_ref[f])
        t = _dot(_bf(_silu(g) * u), wo_ref[f])
        acc = t if acc is None else acc + t
    y = x + 0.5 * acc
    if final:
        y = _rms(y) * fg_ref[...]
    o_ref[...] = y


def _ffn(x, g3, w_in5, w_out4, layer, final_g=None):
    m = x.shape[0]
    tm = min(m, ROW_TILE)
    nf = D_FF // FF_TILE
    once = pl.Buffered(1)
    in_specs = [
        pl.BlockSpec((tm, D_MODEL), lambda i: (i, 0)),
        pl.BlockSpec((None, 1, D_MODEL), lambda i: (layer, 0, 0)),
        pl.BlockSpec((None, None, nf, D_MODEL, FF_TILE), lambda i: (layer, 0, 0, 0, 0), pipeline_mode=once),
        pl.BlockSpec((None, None, nf, D_MODEL, FF_TILE), lambda i: (layer, 1, 0, 0, 0), pipeline_mode=once),
        pl.BlockSpec((None, nf, FF_TILE, D_MODEL), lambda i: (layer, 0, 0, 0), pipeline_mode=once),
    ]
    args = [x, g3, w_in5, w_in5, w_out4]
    if final_g is not None:
        in_specs.append(pl.BlockSpec((1, D_MODEL), lambda i: (0, 0)))
        args.append(final_g.reshape(1, D_MODEL))
    return pl.pallas_call(
        functools.partial(_ffn_body, nf=nf, final=final_g is not None),
        grid=(m // tm,),
        in_specs=in_specs,
        out_specs=pl.BlockSpec((tm, D_MODEL), lambda i: (i, 0)),
        out_shape=jax.ShapeDtypeStruct((m, D_MODEL), F32),
        compiler_params=_cp(("parallel",), 56),
        name="ffn",
    )(*args)


def _norm_proj_body(*refs, small):
    if small:
        x_ref, g_ref, w_ref, ws_ref, o_ref, os_ref, xn_ref = refs
    else:
        x_ref, g_ref, w_ref, o_ref, xn_ref = refs
    n = pl.program_id(1)

    @pl.when(n == 0)
    def _():
        xn = _bf(_rms(x_ref[...]) * g_ref[...])
        xn_ref[...] = xn
        if small:
            os_ref[...] = _dot(xn, _bf(ws_ref[...]))

    o_ref[...] = _dot(xn_ref[...], _bf(w_ref[...]))


def _norm_proj(x, g3, glayer, w3, wlayer, tn, w_small=None, col0=0, n_out=None):
    m = x.shape[0]
    tm = min(m, ROW_TILE)
    n_out = w3.shape[-1] if n_out is None else n_out
    in_specs = [
        pl.BlockSpec((tm, D_MODEL), lambda i, n: (i, 0)),
        pl.BlockSpec((None, 1, D_MODEL), lambda i, n: (glayer, 0, 0)),
        pl.BlockSpec((None, D_MODEL, tn), lambda i, n: (wlayer, 0, n + col0)),
    ]
    args = [x, g3, w3]
    out_specs = [pl.BlockSpec((tm, tn), lambda i, n: (i, n))]
    out_shape = [jax.ShapeDtypeStruct((m, n_out), F32)]
    if w_small is not None:
        ns = w_small.shape[-1]
        in_specs.append(pl.BlockSpec((None, D_MODEL, ns), lambda i, n: (0, 0, 0)))
        args.append(w_small)
        out_specs.append(pl.BlockSpec((tm, ns), lambda i, n: (i, 0)))
        out_shape.append(jax.ShapeDtypeStruct((m, ns), F32))
    res = pl.pallas_call(
        functools.partial(_norm_proj_body, small=w_small is not None),
        grid=(m // tm, n_out // tn),
        in_specs=in_specs,
        out_specs=out_specs,
        out_shape=out_shape,
        scratch_shapes=[pltpu.VMEM((tm, D_MODEL), BF16)],
        compiler_params=_cp(("parallel", "arbitrary"), 48),
        name="norm_proj",
    )(*args)
    return res if w_small is not None else res[0]


AB_TN = 1024
AB_XBC0 = (SSD_D_INNER + 2 * RET_V) // AB_TN
AB_RQ = AB_XBC0 + SSD_CONV_CH // AB_TN


AB_GATE_BLOCKS = AB_XBC0
AB_XQK_BLOCKS = SSD_CONV_CH // AB_TN + 2 * RET_QK // AB_TN
AB_ROW_TILE = 512


def _ab_gates_body(x_ref, g_ref, w_ref, o_ref):
    xn = _bf(_rms(x_ref[...]) * g_ref[...])
    nz = SSD_D_INNER // AB_TN
    nv = RET_V // AB_TN
    for n in range(AB_GATE_BLOCKS):
        r = _dot(xn, w_ref[n])
        o_ref[:, n * AB_TN:(n + 1) * AB_TN] = _bf(r) if nz <= n < nz + nv else _bf(_silu(r))


def _ab_xqk_body(x_ref, g_ref, w_ref, ws_ref, dtb_ref, cw_ref, cb_ref, cos_ref, sin_ref,
                 o_ref, dt_ref, cbuf, carry, *, tiles_per_seq):
    i = pl.program_id(0)
    tm = x_ref.shape[0]
    nxb = SSD_CONV_CH // AB_TN

    @pl.when(i == 0)
    def _():
        carry[...] = jnp.zeros_like(carry)

    xn = _bf(_rms(x_ref[...]) * g_ref[...])
    dt_ref[...] = _softplus(_dot(xn, ws_ref[...]) + dtb_ref[...])
    seq_start = i % tiles_per_seq == 0
    for n in range(nxb):
        cs = slice(n * AB_TN, (n + 1) * AB_TN)
        r = _dot(xn, w_ref[n])
        cbuf[n, 0:SUBLANES, :] = jnp.where(seq_start, 0.0, carry[n])
        cbuf[n, SUBLANES:SUBLANES + tm, :] = r
        w = cw_ref[:, cs]
        conv = cb_ref[:, cs] + (((cbuf[n, 5:5 + tm, :] * w[0:1, :] + cbuf[n, 6:6 + tm, :] * w[1:2, :])
                                 + cbuf[n, 7:7 + tm, :] * w[2:3, :]) + r * w[3:4, :])
        carry[n] = r[tm - SUBLANES:tm, :]
        o_ref[:, cs] = _bf(_silu(conv))
    cos = cos_ref[...]
    sin = sin_ref[...]
    for n in range(nxb, AB_XQK_BLOCKS):
        r = _dot(xn, w_ref[n])
        scale = 1.0 if n == nxb else RET_DK ** -0.5
        for h in range(AB_TN // RET_DK):
            a = slice(h * RET_DK, h * RET_DK + LANES)
            b = slice(h * RET_DK + LANES, (h + 1) * RET_DK)
            x1, x2 = r[:, a], r[:, b]
            o_ref[:, n * AB_TN + a.start:n * AB_TN + a.stop] = _bf((x1 * cos - x2 * sin) * scale)
            o_ref[:, n * AB_TN + b.start:n * AB_TN + b.stop] = _bf((x1 * sin + x2 * cos) * scale)


def _ab_inproj(x, g3, w_gates, w_xqk, w_dt, dt_bias, conv_w, conv_b, cos, sin):
    m = x.shape[0]
    tm = AB_ROW_TILE
    tps = SEQ // tm
    nxb = SSD_CONV_CH // AB_TN
    once = pl.Buffered(1)

    def full(a):
        return pl.BlockSpec(a.shape, lambda i: (0,) * a.ndim, pipeline_mode=once)

    xspec = pl.BlockSpec((tm, D_MODEL), lambda i: (i, 0))
    gspec = pl.BlockSpec((None, 1, D_MODEL), lambda i: (0, 0, 0))
    gates = pl.pallas_call(
        _ab_gates_body,
        grid=(m // tm,),
        in_specs=[xspec, gspec, full(w_gates)],
        out_specs=pl.BlockSpec((tm, AB_GATE_BLOCKS * AB_TN), lambda i: (i, 0)),
        out_shape=jax.ShapeDtypeStruct((m, AB_GATE_BLOCKS * AB_TN), BF16),
        compiler_params=_cp(("parallel",), 56),
        name="ab_gates",
    )(x, g3, w_gates)
    xqk, dt = pl.pallas_call(
        functools.partial(_ab_xqk_body, tiles_per_seq=tps),
        grid=(m // tm,),
        in_specs=[xspec, gspec, full(w_xqk), full(w_dt), full(dt_bias), full(conv_w), full(conv_b),
                  pl.BlockSpec((tm, LANES), lambda i: (i % tps, 0)),
                  pl.BlockSpec((tm, LANES), lambda i: (i % tps, 0))],
        out_specs=[pl.BlockSpec((tm, AB_XQK_BLOCKS * AB_TN), lambda i: (i, 0)),
                   pl.BlockSpec((tm, LANES), lambda i: (i, 0))],
        out_shape=[jax.ShapeDtypeStruct((m, AB_XQK_BLOCKS * AB_TN), BF16), jax.ShapeDtypeStruct((m, LANES), F32)],
        scratch_shapes=[pltpu.VMEM((nxb, SUBLANES + tm, AB_TN), F32),
                        pltpu.VMEM((nxb, SUBLANES, AB_TN), F32)],
        compiler_params=_cp(("arbitrary",), 56),
        name="ab_xqk",
    )(x, g3, w_xqk, w_dt, dt_bias, conv_w, conv_b, cos, sin)
    return gates, xqk, dt


def _proj_res_body(x_ref, y_ref, w_ref, o_ref):
    k = pl.program_id(1)

    @pl.when(k == 0)
    def _():
        o_ref[...] = x_ref[...]

    o_ref[...] += _dot(_bf(y_ref[...]), _bf(w_ref[...]))


def _proj_residual(x, y, w3, layer):
    m = x.shape[0]
    tm = min(m, ROW_TILE)
    kdim = y.shape[1]
    tk = min(kdim, 1024)
    return pl.pallas_call(
        _proj_res_body,
        grid=(m // tm, kdim // tk),
        in_specs=[
            pl.BlockSpec((tm, D_MODEL), lambda i, k: (i, 0)),
            pl.BlockSpec((tm, tk), lambda i, k: (i, k)),
            pl.BlockSpec((None, tk, D_MODEL), lambda i, k: (layer, k, 0)),
        ],
        out_specs=pl.BlockSpec((tm, D_MODEL), lambda i, k: (i, 0)),
        out_shape=jax.ShapeDtypeStruct((m, D_MODEL), F32),
        compiler_params=_cp(("parallel", "arbitrary"), 48),
        name="proj_residual",
    )(x, y, w3)


def _mem_kv_body(x_ref, g_ref, w_ref, k_ref, v_ref):
    xn = _bf(_rms(x_ref[...]) * g_ref[...])
    kv = _dot(xn, _bf(w_ref[...]))
    k_ref[...] = kv[:, :D_MODEL]
    v_ref[...] = kv[:, D_MODEL:]


def _mem_kv(mem2d, g3, w_xkv):
    m = mem2d.shape[0]
    tm = 512
    shp = jax.ShapeDtypeStruct((DEPTH, m, D_MODEL), F32)
    return pl.pallas_call(
        _mem_kv_body,
        grid=(DEPTH, m // tm),
        in_specs=[
            pl.BlockSpec((tm, D_MODEL), lambda l, i: (i, 0)),
            pl.BlockSpec((None, 1, D_MODEL), lambda l, i: (l, 0, 0)),
            pl.BlockSpec((None, D_MODEL, 2 * D_MODEL), lambda l, i: (l, 0, 0)),
        ],
        out_specs=[pl.BlockSpec((None, tm, D_MODEL), lambda l, i: (l, i, 0))] * 2,
        out_shape=[shp, shp],
        compiler_params=_cp(("arbitrary", "arbitrary"), 48),
        name="mem_kv",
    )(mem2d, g3, w_xkv)


def _xattn_prompt_body(x_ref, g_ref, wq_ref, wo_ref, k_ref, v_ref, o_ref):
    x = x_ref[...]
    xn = _bf(_rms(x) * g_ref[...])
    q = _dot(xn, _bf(wq_ref[...]))
    k = _bf(k_ref[...])
    v = _bf(v_ref[...])
    outs = []
    for h in range(X_HEADS):
        sl = slice(h * X_HEAD_DIM, (h + 1) * X_HEAD_DIM)
        s = _dot_nt(_bf(q[:, sl]), k[:, sl]) * (X_HEAD_DIM ** -0.5)
        e = jnp.exp(s - jnp.max(s, axis=-1, keepdims=True))
        p = e / jnp.sum(e, axis=-1, keepdims=True)
        outs.append(_bf(_dot(_bf(p), v[:, sl])))
    o_ref[...] = x + _dot(jnp.concatenate(outs, axis=1), _bf(wo_ref[...]))


def _xattn_prompt(x, g3, w_xq, w_xo, memk, memv, layer):
    tq = 512
    nq = SEQ // tq
    return pl.pallas_call(
        _xattn_prompt_body,
        grid=(BATCH, nq),
        in_specs=[
            pl.BlockSpec((tq, D_MODEL), lambda b, j: (b * nq + j, 0)),
            pl.BlockSpec((None, 1, D_MODEL), lambda b, j: (layer, 0, 0)),
            pl.BlockSpec((None, D_MODEL, D_MODEL), lambda b, j: (layer, 0, 0)),
            pl.BlockSpec((None, D_MODEL, D_MODEL), lambda b, j: (layer, 0, 0)),
            pl.BlockSpec((None, MEM_LEN, D_MODEL), lambda b, j: (layer, b, 0)),
            pl.BlockSpec((None, MEM_LEN, D_MODEL), lambda b, j: (layer, b, 0)),
        ],
        out_specs=pl.BlockSpec((tq, D_MODEL), lambda b, j: (b * nq + j, 0)),
        out_shape=jax.ShapeDtypeStruct((BATCH * SEQ, D_MODEL), F32),
        compiler_params=_cp(("parallel", "arbitrary"), 48),
        name="xattn_prompt",
    )(x, g3, w_xq, w_xo, memk, memv)


def _xattn_sample_body(q_ref, k_ref, v_ref, o_ref):
    for b in range(XB):
        s = jnp.sum(k_ref[b] * q_ref[b][None], axis=-1, keepdims=True) * (X_HEAD_DIM ** -0.5)
        e = jnp.exp(s - jnp.max(s, axis=0, keepdims=True))
        p = e / jnp.sum(e, axis=0, keepdims=True)
        o_ref[b] = jnp.sum(p * v_ref[b], axis=0)


def _xattn_sample(q, cache_k, cache_v, layer):
    blk = pl.BlockSpec((None, XB, MEM_LEN, X_HEADS, X_HEAD_DIM), lambda i: (layer, i, 0, 0, 0))
    qo = pl.BlockSpec((XB, X_HEADS, X_HEAD_DIM), lambda i: (i, 0, 0))
    return pl.pallas_call(
        _xattn_sample_body,
        grid=(DEC_BATCH // XB,),
        in_specs=[qo, blk, blk],
        out_specs=qo,
        out_shape=jax.ShapeDtypeStruct((DEC_BATCH, X_HEADS, X_HEAD_DIM), F32),
        compiler_params=_cp(("parallel",), 48),
        name="xattn_sample",
    )(q, cache_k, cache_v)


def _ret_log_gamma(h):
    return math.log1p(-(2.0 ** (-5.0 - h)))


def _ab_prompt_body(z_ref, rv_ref, rg_ref, xbc_ref, rq_ref, rk_ref, dt_ref, alog_ref, dsk_ref, nrm_ref,
                    y_ref, h_ref, s_ref):
    c = pl.program_id(1)

    @pl.when(c == 0)
    def _():
        h_ref[...] = jnp.zeros_like(h_ref)
        s_ref[...] = jnp.zeros_like(s_ref)

    xs = xbc_ref[:, :SSD_D_INNER].astype(F32)
    bm = xbc_ref[:, SSD_D_INNER:SSD_D_INNER + GROUP_W]
    cm = xbc_ref[:, SSD_D_INNER + GROUP_W:]

    dt = dt_ref[...]
    da = dt * (-jnp.exp(alog_ref[...]))
    cs = _cumsum_rows(da)
    cs_t = cs.T
    row = lax.broadcasted_iota(jnp.int32, (CHUNK, CHUNK), 0)
    col = lax.broadcasted_iota(jnp.int32, (CHUNK, CHUNK), 1)
    tri = row >= col
    lo = col < SSD_HEAD_DIM
    for g in range(SSD_GROUPS):
        gs = slice(g * GROUP_W, (g + 1) * GROUP_W)
        ns = slice(g * SSD_D_STATE, (g + 1) * SSD_D_STATE)
        cmg = cm[:, ns]
        bmg = bm[:, ns]
        att = _dot_nt(cmg, bmg)
        hprev = h_ref[gs, :]
        yint = _dot_nt(cmg, _bf(hprev))
        ys, wxs, css = [], [], []
        for j in range(HEADS_PER_GROUP // 2):
            h0 = g * HEADS_PER_GROUP + 2 * j
            cb0 = _lane_bcast(cs, h0)
            cb1 = _lane_bcast(cs, h0 + 1)
            cs_p = jnp.where(lo, cb0, cb1)
            dt_p = jnp.where(lo, _lane_bcast(dt, h0), _lane_bcast(dt, h0 + 1))
            off = g * GROUP_W + j * LANES
            xdt = xs[:, off:off + LANES] * dt_p
            d0 = jnp.exp(jnp.where(tri, cb0 - cs_t[h0:h0 + 1, :], -jnp.inf))
            d1 = jnp.exp(jnp.where(tri, cb1 - cs_t[h0 + 1:h0 + 2, :], -jnp.inf))
            yy = _dot(jnp.concatenate([_bf(att * d0), _bf(att * d1)], axis=0), _bf(xdt))
            ys.append(jnp.where(lo, yy[:CHUNK], yy[CHUNK:]) + yint[:, j * LANES:(j + 1) * LANES] * jnp.exp(cs_p))
            wxs.append(_bf(xdt * jnp.exp(cs_p[CHUNK - 1:CHUNK, :] - cs_p)))
            css.append(cs_p)
        cs_g = jnp.concatenate(css, axis=1)
        last_t = jnp.broadcast_to(cs_g[CHUNK - 1:CHUNK, :], (CHUNK, GROUP_W)).T
        h_ref[gs, :] = hprev * jnp.exp(last_t) + _dot_tn(jnp.concatenate(wxs, axis=1), bmg)
        yg = jnp.concatenate(ys, axis=1)
        yg = (yg + xs[:, gs] * dsk_ref[:, gs]) * z_ref[:, gs].astype(F32)
        y_ref[:, gs] = _bf(_rms(yg) * nrm_ref[:, gs])

    tcol = row.astype(F32)
    diff = tcol - col.astype(F32)
    for h in range(RET_HEADS):
        lg = _ret_log_gamma(h)
        ks = slice(h * RET_DK, (h + 1) * RET_DK)
        vs = slice(h * RET_DV, (h + 1) * RET_DV)
        qb = rq_ref[:, ks]
        kb = rk_ref[:, ks]
        decay = jnp.exp(jnp.where(tri, diff * lg, -jnp.inf))
        att = _dot_nt(qb, kb) * decay
        vb = rv_ref[:, vs]
        s_prev = s_ref[h]
        inner = jnp.exp((tcol + 1.0) * lg)
        r = _dot(_bf(att), vb) + _dot(qb, _bf(s_prev)) * jnp.concatenate([inner] * (RET_DV // LANES), axis=1)
        tail_w = jnp.exp((CHUNK - 1.0 - tcol) * lg)
        kt = _bf(kb.astype(F32) * jnp.concatenate([tail_w] * (RET_DK // LANES), axis=1))
        s_ref[h] = s_prev * math.exp(CHUNK * lg) + _dot_tn(kt, vb)
        os = slice(SSD_D_INNER + h * RET_DV, SSD_D_INNER + (h + 1) * RET_DV)
        y_ref[:, os] = _bf(rg_ref[:, vs].astype(F32) * _rms(r))


def _mixer_ab_prompt(p_gates, p_xqk, p_dt, a_log, d_skip_e, ssd_norm):
    nc = SEQ // CHUNK
    m = BATCH * SEQ

    def rowspec(width, cb):
        return pl.BlockSpec((CHUNK, width), lambda b, c: (b * nc + c, cb))

    def full(a):
        return pl.BlockSpec(a.shape, lambda b, c: (0,) * a.ndim)

    params = [a_log, d_skip_e, ssd_norm]
    return pl.pallas_call(
        _ab_prompt_body,
        grid=(BATCH, nc),
        in_specs=[rowspec(SSD_D_INNER, 0), rowspec(RET_V, 1), rowspec(RET_V, 2), rowspec(SSD_CONV_CH, 0),
                  rowspec(RET_QK, SSD_CONV_CH // RET_QK), rowspec(RET_QK, SSD_CONV_CH // RET_QK + 1),
                  rowspec(LANES, 0)] + [full(a) for a in params],
        out_specs=[
            pl.BlockSpec((CHUNK, SSD_D_INNER + RET_V), lambda b, c: (b * nc + c, 0)),
            pl.BlockSpec((None, SSD_D_INNER, SSD_D_STATE), lambda b, c: (b, 0, 0)),
            pl.BlockSpec((None, RET_HEADS, RET_DK, RET_DV), lambda b, c: (b, 0, 0, 0)),
        ],
        out_shape=[
            jax.ShapeDtypeStruct((m, SSD_D_INNER + RET_V), BF16),
            jax.ShapeDtypeStruct((BATCH, SSD_D_INNER, SSD_D_STATE), F32),
            jax.ShapeDtypeStruct((BATCH, RET_HEADS, RET_DK, RET_DV), F32),
        ],
        compiler_params=_cp(("parallel", "arbitrary"), 48),
        name="mixer_ab_prompt",
    )(p_gates, p_gates, p_gates, p_xqk, p_xqk, p_xqk, p_dt, *params)


def _c_prompt_body(q_ref, k_ref, v_ref, o_ref, gt_ref, ib_ref, fb_ref, nrm_ref,
                   h_ref, c_ref, n_ref, m_ref, m_s):
    c = pl.program_id(1)
    for s in range(CB):
        _c_chunk(c, q_ref.at[s], k_ref.at[s], v_ref.at[s], o_ref.at[s], gt_ref.at[s], ib_ref, fb_ref, nrm_ref,
                 h_ref.at[s], c_ref.at[s], n_ref.at[s], m_ref.at[s], m_s.at[s])


def _c_chunk(c, q_ref, k_ref, v_ref, o_ref, gt_ref, ib_ref, fb_ref, nrm_ref, h_ref, c_ref, n_ref, m_ref, m_s):
    @pl.when(c == 0)
    def _():
        c_ref[...] = jnp.zeros_like(c_ref)
        n_ref[...] = jnp.zeros_like(n_ref)
        m_s[...] = jnp.zeros_like(m_s)

    ipre = gt_ref[:, :LANES] + ib_ref[...]
    lf = -_softplus(-(gt_ref[:, LANES:] + fb_ref[...]))
    b = _cumsum_rows(lf)
    g = ipre - b
    g_t = g.T
    b_t = b.T
    row = lax.broadcasted_iota(jnp.int32, (CHUNK, CHUNK), 0)
    col = lax.broadcasted_iota(jnp.int32, (CHUNK, CHUNK), 1)
    tri = row >= col
    cmax = g_t
    sh = 1
    while sh < CHUNK:
        cmax = jnp.maximum(cmax, jnp.where(col >= sh, pltpu.roll(cmax, sh, axis=1), -jnp.inf))
        sh *= 2
    m_prev = m_s[...]
    mt_t = b_t + jnp.maximum(m_prev, cmax)
    mt = mt_t.T
    m_prev_c = m_prev.T
    inter = jnp.exp(b + m_prev_c - mt)
    emt = jnp.exp(-mt)
    wl = jnp.exp(g + b[CHUNK - 1:CHUNK, :] - mt[CHUNK - 1:CHUNK, :])
    bm = b - mt
    m_new = _lane_bcast(mt_t, CHUNK - 1)
    dp_t = jnp.exp(_lane_bcast(b_t, CHUNK - 1) + m_prev - m_new)
    m_s[...] = m_new
    m_ref[...] = m_new[0:SUBLANES, :]
    outs = []
    for h in range(M_HEADS):
        ks = slice(h * M_DK, (h + 1) * M_DK)
        vs = slice(h * M_DV, (h + 1) * M_DV)
        wgt = jnp.exp(jnp.where(tri, g_t[h:h + 1, :] + bm[:, h:h + 1], -jnp.inf))
        qh = q_ref[:, ks]
        kh = k_ref[:, ks] * (M_DK ** -0.5)
        qb = _bf(qh)
        vb = _bf(v_ref[:, vs])
        a = _dot_nt(qb, _bf(kh)) * wgt
        c_prev = c_ref[h]
        n_prev = n_ref[h:h + 1, :]
        ic = inter[:, h:h + 1]
        num = _dot(_bf(a), vb) + _dot(qb, _bf(c_prev)) * ic
        den = jnp.sum(a, axis=1, keepdims=True) + jnp.sum(qh * n_prev, axis=1, keepdims=True) * ic
        outs.append(num / jnp.maximum(jnp.abs(den), emt[:, h:h + 1]))
        kw = kh * wl[:, h:h + 1]
        dp_row = dp_t[h:h + 1, :]
        c_ref[h] = c_prev * jnp.concatenate([dp_row] * (M_DV // LANES), axis=1) + _dot_tn(_bf(kw), vb)
        n_ref[h:h + 1, :] = n_prev * dp_row + jnp.sum(kw, axis=0, keepdims=True)
    for h in range(M_HEADS):
        vs = slice(h * M_DV, (h + 1) * M_DV)
        h_ref[:, vs] = _bf(jax.nn.sigmoid(o_ref[:, vs]) * (_rms(outs[h]) * nrm_ref[:, vs]))


def _mixer_c_prompt(p_main, p_gate, i_bias, f_bias, norm_g):
    nc = SEQ // CHUNK

    def rowspec(width, cb):
        return pl.BlockSpec((CB, CHUNK, width), lambda b, c: (b, c, cb))

    def full(a):
        return pl.BlockSpec(a.shape, lambda b, c: (0,) * a.ndim)

    params = [i_bias, f_bias, norm_g]
    return pl.pallas_call(
        _c_prompt_body,
        grid=(BATCH // CB, nc),
        in_specs=[rowspec(M_QK, 0), rowspec(M_QK, 1), rowspec(M_V, 1), rowspec(M_V, 2), rowspec(2 * LANES, 0)]
        + [full(a) for a in params],
        out_specs=[
            pl.BlockSpec((CB, CHUNK, M_V), lambda b, c: (b, c, 0)),
            pl.BlockSpec((CB, M_HEADS, M_DK, M_DV), lambda b, c: (b, 0, 0, 0)),
            pl.BlockSpec((CB, M_HEADS, M_DK), lambda b, c: (b, 0, 0)),
            pl.BlockSpec((CB, SUBLANES, LANES), lambda b, c: (b, 0, 0)),
        ],
        out_shape=[
            jax.ShapeDtypeStruct((BATCH, SEQ, M_V), BF16),
            jax.ShapeDtypeStruct((BATCH, M_HEADS, M_DK, M_DV), F32),
            jax.ShapeDtypeStruct((BATCH, M_HEADS, M_DK), F32),
            jax.ShapeDtypeStruct((BATCH, SUBLANES, LANES), F32),
        ],
        scratch_shapes=[pltpu.VMEM((CB, CHUNK, LANES), F32)],
        compiler_params=_cp(("parallel", "arbitrary"), 48),
        name="mixer_c_prompt",
    )(p_main, p_main, p_main, p_main, p_gate, *params)


def _ab_sample_prep_body(xbc_ref, rq_ref, rk_ref, dt_ref, cst_ref, cos_ref, sin_ref,
                         cw_ref, cb_ref, dtb_ref, alog_ref,
                         conv_ref, xs_ref, xdt_ref, eda_ref, bm_ref, cm_ref, q_ref, k_ref, gam_ref):
    ch = SSD_CONV_CH
    u = xbc_ref[...]
    w = cw_ref[...]
    b0 = cst_ref[:, 0:ch]
    b1 = cst_ref[:, ch:2 * ch]
    b2 = cst_ref[:, 2 * ch:3 * ch]
    conv = cb_ref[...] + (((b0 * w[0:1, :] + b1 * w[1:2, :]) + b2 * w[2:3, :]) + u * w[3:4, :])
    conv_ref[:, 0:ch] = b1
    conv_ref[:, ch:2 * ch] = b2
    conv_ref[:, 2 * ch:3 * ch] = u
    xbc = _silu(conv)
    xs = xbc[:, :SSD_D_INNER]
    xs_ref[...] = xs
    bm_ref[...] = xbc[:, SSD_D_INNER:SSD_D_INNER + GROUP_W]
    cm_ref[...] = xbc[:, SSD_D_INNER + GROUP_W:]
    dt = _softplus(dt_ref[...] + dtb_ref[...])
    eda = jnp.exp(dt * (-jnp.exp(alog_ref[...])))
    xdt_ref[...] = xs * _pair_expand(dt, SSD_HEADS)
    eda_ref[...] = _pair_expand(eda, SSD_HEADS)
    cos = cos_ref[...]
    sin = sin_ref[...]
    for h in range(RET_HEADS):
        a = slice(h * RET_DK, h * RET_DK + LANES)
        b = slice(h * RET_DK + LANES, (h + 1) * RET_DK)
        q1, q2 = rq_ref[:, a], rq_ref[:, b]
        k1, k2 = rk_ref[:, a], rk_ref[:, b]
        q_ref[:, a] = q1 * cos - q2 * sin
        q_ref[:, b] = q1 * sin + q2 * cos
        k_ref[:, a] = (k1 * cos - k2 * sin) * (RET_DK ** -0.5)
        k_ref[:, b] = (k1 * sin + k2 * cos) * (RET_DK ** -0.5)
        gam_ref[:, h * RET_DK:(h + 1) * RET_DK] = jnp.full((DEC_BATCH, RET_DK), math.exp(_ret_log_gamma(h)), F32)


def _ab_sample_prep(p_main, p_dt, conv_state, cos, sin, conv_w, conv_b, dt_bias, a_log):
    n = DEC_BATCH

    def colspec(width, cb):
        return pl.BlockSpec((n, width), lambda i: (0, cb))

    def full(a):
        return pl.BlockSpec(a.shape, lambda i: (0,) * a.ndim)

    small = [conv_state, cos, sin, conv_w, conv_b, dt_bias, a_log]

    def out(width):
        return jax.ShapeDtypeStruct((n, width), F32)

    widths = [(SSD_CONV - 1) * SSD_CONV_CH, SSD_D_INNER, SSD_D_INNER, SSD_D_INNER, GROUP_W, GROUP_W,
              RET_QK, RET_QK, RET_QK]
    return pl.pallas_call(
        _ab_sample_prep_body,
        grid=(1,),
        in_specs=[colspec(SSD_CONV_CH, 2), colspec(RET_QK, 9), colspec(RET_QK, 10), full(p_dt)]
        + [full(a) for a in small],
        out_specs=[pl.BlockSpec((n, wd), lambda i: (0, 0)) for wd in widths],
        out_shape=[out(wd) for wd in widths],
        compiler_params=_cp(("arbitrary",), 48),
        name="ab_sample_prep",
    )(p_main, p_main, p_main, p_dt, *small)


def _ssd_state_body(eda_ref, xdt_ref, bm_ref, cm_ref, h_ref, ho_ref, y_ref):
    eda_t = _pad_t(eda_ref[...])
    xdt_t = _pad_t(xdt_ref[...])
    lane = lax.broadcasted_iota(jnp.int32, (GROUP_W, LANES), 1)
    ycols = jnp.zeros((GROUP_W, LANES), F32)
    for b in range(SB):
        hn = h_ref[b] * eda_t[:, b:b + 1] + xdt_t[:, b:b + 1] * bm_ref[b:b + 1, :]
        ho_ref[b] = hn
        ycols = jnp.where(lane == b, jnp.sum(hn * cm_ref[b:b + 1, :], axis=1, keepdims=True), ycols)
    y_ref[...] = ycols.T[0:SB, :]


def _ssd_state(eda, xdt, bm, cm, state):
    vec = pl.BlockSpec((SB, GROUP_W), lambda i, g: (i, g))
    bc = pl.BlockSpec((SB, SSD_D_STATE), lambda i, g: (i, g))
    st = pl.BlockSpec((SB, GROUP_W, SSD_D_STATE), lambda i, g: (i, g, 0))
    return pl.pallas_call(
        _ssd_state_body,
        grid=(DEC_BATCH // SB, SSD_GROUPS),
        in_specs=[vec, vec, bc, bc, st],
        out_specs=[st, vec],
        out_shape=[jax.ShapeDtypeStruct(state.shape, F32), jax.ShapeDtypeStruct((DEC_BATCH, SSD_D_INNER), F32)],
        compiler_params=_cp(("parallel", "arbitrary"), 48),
        name="ssd_state",
    )(eda, xdt, bm, cm, state)


def _outer_state_body(d_ref, k_ref, q_ref, v_ref, s_ref, so_ref, o_ref):
    d_t = _pad_t(d_ref[...])
    k_t = _pad_t(k_ref[...])
    q_t = _pad_t(q_ref[...])
    for b in range(SB):
        sn = s_ref[b] * d_t[:, b:b + 1] + k_t[:, b:b + 1] * v_ref[b:b + 1, :]
        so_ref[b] = sn
        o_ref[b:b + 1, :] = jnp.sum(sn * q_t[:, b:b + 1], axis=0, keepdims=True)


def _outer_state(d, k, q, v, state):
    _, nh, dk, dv = state.shape
    kv = pl.BlockSpec((SB, dk), lambda i, h: (i, h))
    vv = pl.BlockSpec((SB, dv), lambda i, h: (i, h))
    st = pl.BlockSpec((SB, None, dk, dv), lambda i, h: (i, h, 0, 0))
    return pl.pallas_call(
        _outer_state_body,
        grid=(DEC_BATCH // SB, nh),
        in_specs=[kv, kv, kv, vv, st],
        out_specs=[st, vv],
        out_shape=[jax.ShapeDtypeStruct(state.shape, F32), jax.ShapeDtypeStruct((DEC_BATCH, nh * dv), F32)],
        compiler_params=_cp(("parallel", "arbitrary"), 48),
        name="outer_state",
    )(d, k, q, v, state)


def _ab_sample_post_body(y_ref, xs_ref, z_ref, r_ref, rg_ref, dsk_ref, nrm_ref, o_ref):
    for g in range(SSD_GROUPS):
        gs = slice(g * GROUP_W, (g + 1) * GROUP_W)
        yg = (y_ref[:, gs] + xs_ref[:, gs] * dsk_ref[:, gs]) * _silu(z_ref[:, gs])
        o_ref[:, gs] = _bf(_rms(yg) * nrm_ref[:, gs])
    for h in range(RET_HEADS):
        vs = slice(h * RET_DV, (h + 1) * RET_DV)
        os = slice(SSD_D_INNER + h * RET_DV, SSD_D_INNER + (h + 1) * RET_DV)
        o_ref[:, os] = _bf(_silu(rg_ref[:, vs]) * _rms(r_ref[:, vs]))


def _ab_sample_post(y, xs, p_main, r, d_skip_e, ssd_norm):
    n = DEC_BATCH

    def full(a):
        return pl.BlockSpec(a.shape, lambda i: (0,) * a.ndim)

    return pl.pallas_call(
        _ab_sample_post_body,
        grid=(1,),
        in_specs=[full(y), full(xs), pl.BlockSpec((n, SSD_D_INNER), lambda i: (0, 0)), full(r),
                  pl.BlockSpec((n, RET_V), lambda i: (0, 2)), full(d_skip_e), full(ssd_norm)],
        out_specs=pl.BlockSpec((n, SSD_D_INNER + RET_V), lambda i: (0, 0)),
        out_shape=jax.ShapeDtypeStruct((n, SSD_D_INNER + RET_V), BF16),
        compiler_params=_cp(("arbitrary",), 48),
        name="ab_sample_post",
    )(y, xs, p_main, r, p_main, d_skip_e, ssd_norm)


def _c_sample_prep_body(q_ref, k_ref, gt_ref, n_ref, m_ref, ib_ref, fb_ref,
                        dpe_ref, kw_ref, nn_ref, mn_ref, dn_ref):
    ipre = gt_ref[:, :LANES] + ib_ref[...]
    lf = -_softplus(-(gt_ref[:, LANES:] + fb_ref[...]))
    m_prev = m_ref[...]
    mt = jnp.maximum(lf + m_prev, ipre)
    wgt = jnp.exp(ipre - mt)
    dp = jnp.exp(lf + m_prev - mt)
    emt = jnp.exp(-mt)
    mn_ref[...] = mt
    for h in range(M_HEADS):
        ks = slice(h * M_DK, (h + 1) * M_DK)
        dpe = _lane_bcast(dp, h)
        kw = k_ref[:, ks] * (M_DK ** -0.5) * _lane_bcast(wgt, h)
        nn = n_ref[:, ks] * dpe + kw
        den = jnp.sum(nn * q_ref[:, ks], axis=1, keepdims=True)
        dpe_ref[:, ks] = dpe
        kw_ref[:, ks] = kw
        nn_ref[:, ks] = nn
        dn_ref[:, h * M_DV:(h + 1) * M_DV] = jnp.broadcast_to(
            jnp.maximum(jnp.abs(den), emt[:, h:h + 1]), (DEC_BATCH, M_DV))


def _c_sample_prep(p_main, p_gate, n_state, m_state, i_bias, f_bias):
    n = DEC_BATCH

    def full(a):
        return pl.BlockSpec(a.shape, lambda i: (0,) * a.ndim)

    widths = [M_QK, M_QK, M_QK, LANES, M_V]
    return pl.pallas_call(
        _c_sample_prep_body,
        grid=(1,),
        in_specs=[pl.BlockSpec((n, M_QK), lambda i: (0, 0)), pl.BlockSpec((n, M_QK), lambda i: (0, 1)),
                  full(p_gate), full(n_state), full(m_state), full(i_bias), full(f_bias)],
        out_specs=[pl.BlockSpec((n, wd), lambda i: (0, 0)) for wd in widths],
        out_shape=[jax.ShapeDtypeStruct((n, wd), F32) for wd in widths],
        compiler_params=_cp(("arbitrary",), 48),
        name="c_sample_prep",
    )(p_main, p_main, p_gate, n_state, m_state, i_bias, f_bias)


def _c_sample_post_body(num_ref, dn_ref, o_ref, nrm_ref, h_ref):
    for h in range(M_HEADS):
        vs = slice(h * M_DV, (h + 1) * M_DV)
        hc = num_ref[:, vs] / dn_ref[:, vs]
        h_ref[:, vs] = _bf(jax.nn.sigmoid(o_ref[:, vs]) * (_rms(hc) * nrm_ref[:, vs]))


def _c_sample_post(num, den, p_main, norm_g):
    n = DEC_BATCH

    def full(a):
        return pl.BlockSpec(a.shape, lambda i: (0,) * a.ndim)

    return pl.pallas_call(
        _c_sample_post_body,
        grid=(1,),
        in_specs=[full(num), full(den), pl.BlockSpec((n, M_V), lambda i: (0, 2)), full(norm_g)],
        out_specs=pl.BlockSpec((n, M_V), lambda i: (0, 0)),
        out_shape=jax.ShapeDtypeStruct((n, M_V), BF16),
        compiler_params=_cp(("arbitrary",), 48),
        name="c_sample_post",
    )(num, den, p_main, norm_g)


def _rope_tables(pos):
    half = RET_DK // 2
    inv = jnp.exp(-math.log(ROPE_BASE) * jnp.arange(half, dtype=F32) / half)
    ang = pos.astype(F32)[:, None] * inv
    return jnp.cos(ang), jnp.sin(ang)


def _pad_lanes(v, width=LANES):
    return jnp.pad(v.reshape(1, -1), ((0, 0), (0, width - v.size)))


def kernel(x_prompt, x_sample, cache_mem_k, cache_mem_v, state_conv, state_ssm, state_ret, state_mlstm_c, state_mlstm_n, state_mlstm_m, mem_prompt, norm_ffn1, w_ffn1_in, w_ffn1_out, norm_mix, w_in_ab, ssd_conv_w, ssd_conv_b, ssd_dt_bias, ssd_a_log, ssd_d, ssd_norm, w_out_ab, w_in_c, mlstm_i_bias, mlstm_f_bias, mlstm_norm, w_out_c, norm_xattn, norm_mem, w_xq, w_xkv, w_xo, norm_ffn2, w_ffn2_in, w_ffn2_out, norm_final):
    g3 = lambda g: g.reshape(DEPTH, 1, D_MODEL)
    n_ffn1, n_mix, n_x, n_mem, n_ffn2 = g3(norm_ffn1), g3(norm_mix), g3(norm_xattn), g3(norm_mem), g3(norm_ffn2)

    def ffn_w(w_in, w_out):
        nf = D_FF // FF_TILE
        w_in5 = w_in.astype(BF16).reshape(DEPTH, D_MODEL, 2, nf, FF_TILE).transpose(0, 2, 3, 1, 4)
        return w_in5, w_out.astype(BF16).reshape(DEPTH, nf, FF_TILE, D_MODEL)

    f1_in, f1_out = ffn_w(w_ffn1_in, w_ffn1_out)
    f2_in, f2_out = ffn_w(w_ffn2_in, w_ffn2_out)
    w_out_ab, w_out_c, w_xq, w_xkv, w_xo = (w.astype(BF16) for w in (w_out_ab, w_out_c, w_xq, w_xkv, w_xo))

    wz, wxbc, wdt, wrq, wrk, wrv, wrg = jnp.split(w_in_ab[0], np_cumsum(AB_SIZES), axis=1)
    w_ab_main = jnp.concatenate([wz, wrv, wrg, wxbc, wrq, wrk], axis=1).astype(BF16)[None]
    w_ab_dt = jnp.pad(wdt, ((0, 0), (0, LANES - SSD_HEADS))).astype(BF16)[None]
    wq, wk, wv, wi, wf, wo = jnp.split(w_in_c[0], np_cumsum(C_SIZES), axis=1)
    w_c_main = jnp.concatenate([wq, wk, wv, wo], axis=1).astype(BF16)[None]
    gpad = ((0, 0), (0, LANES - M_HEADS))
    w_c_gate = jnp.concatenate([jnp.pad(wi, gpad), jnp.pad(wf, gpad)], axis=1).astype(BF16)[None]

    conv_w = ssd_conv_w[0]
    conv_b = ssd_conv_b.reshape(1, SSD_CONV_CH)
    dt_bias = _pad_lanes(ssd_dt_bias[0])
    a_log = _pad_lanes(ssd_a_log[0])
    d_skip_e = jnp.repeat(ssd_d[0], SSD_HEAD_DIM).reshape(1, SSD_D_INNER)
    s_norm = ssd_norm.reshape(1, SSD_D_INNER)
    i_bias = _pad_lanes(mlstm_i_bias[0])
    f_bias = _pad_lanes(mlstm_f_bias[0])
    m_norm = mlstm_norm.reshape(1, M_V)

    memk, memv = _mem_kv(mem_prompt.reshape(BATCH * MEM_LEN, D_MODEL), n_mem, w_xkv)
    cos_p, sin_p = _rope_tables(jnp.arange(SEQ))
    x = x_prompt.reshape(BATCH * SEQ, D_MODEL)
    x = _ffn(x, n_ffn1, f1_in, f1_out, 0)
    def col_blocks(w2d):
        return w2d.reshape(D_MODEL, -1, AB_TN).transpose(1, 0, 2)

    w_gates = col_blocks(w_ab_main[0, :, :AB_GATE_BLOCKS * AB_TN])
    w_xqk = col_blocks(w_ab_main[0, :, AB_GATE_BLOCKS * AB_TN:])
    p_gates, p_xqk, p_dt = _ab_inproj(x, n_mix, w_gates, w_xqk, w_ab_dt[0], dt_bias, conv_w, conv_b, cos_p, sin_p)
    ycat, ssm_p, ret_p = _mixer_ab_prompt(p_gates, p_xqk, p_dt, a_log, d_skip_e, s_norm)
    x_tail = x.reshape(BATCH, SEQ, D_MODEL)[:, SEQ - SSD_CONV:].reshape(BATCH * SSD_CONV, D_MODEL)
    conv_p = _norm_proj(x_tail, n_mix, 0, w_ab_main, 0, AB_TN, col0=AB_XBC0, n_out=SSD_CONV_CH)
    conv_p = conv_p.reshape(BATCH, SSD_CONV, SSD_CONV_CH)[:, 1:]
    x = _proj_residual(x, ycat, w_out_ab, 0)
    x = _xattn_prompt(x, n_x, w_xq, w_xo, memk, memv, 0)
    x = _ffn(x, n_ffn2, f2_in, f2_out, 0)
    x = _ffn(x, n_ffn1, f1_in, f1_out, 1)
    pc_main, pc_gate = _norm_proj(x, n_mix, 1, w_c_main, 0, 1024, w_c_gate)
    hout, mc_p, mn_p, mm_p = _mixer_c_prompt(pc_main.reshape(BATCH, SEQ, -1), pc_gate.reshape(BATCH, SEQ, -1),
                                             i_bias, f_bias, m_norm)
    x = _proj_residual(x, hout.reshape(BATCH * SEQ, M_V), w_out_c, 0)
    x = _xattn_prompt(x, n_x, w_xq, w_xo, memk, memv, 1)
    y_prompt = _ffn(x, n_ffn2, f2_in, f2_out, 1, norm_final).reshape(BATCH, SEQ, D_MODEL)

    cos_s, sin_s = _rope_tables(PAST_LEN + jnp.arange(1))

    def xattn_s(xs_, layer):
        q = _norm_proj(xs_, n_x, layer, w_xq, layer, 1024).reshape(DEC_BATCH, X_HEADS, X_HEAD_DIM)
        o = _xattn_sample(q, cache_mem_k, cache_mem_v, layer).reshape(DEC_BATCH, D_MODEL)
        return _proj_residual(xs_, o, w_xo, layer)

    xs_ = x_sample.reshape(DEC_BATCH, D_MODEL)
    xs_ = _ffn(xs_, n_ffn1, f1_in, f1_out, 0)
    sp_main, sp_dt = _norm_proj(xs_, n_mix, 0, w_ab_main, 0, 1024, w_ab_dt)
    conv_s, xs_c, xdt, eda, bm_s, cm_s, q_s, k_s, gam = _ab_sample_prep(
        sp_main, sp_dt, state_conv.reshape(DEC_BATCH, (SSD_CONV - 1) * SSD_CONV_CH), cos_s, sin_s,
        conv_w, conv_b, dt_bias, a_log)
    ssm_s, y_s = _ssd_state(eda, xdt, bm_s, cm_s, state_ssm.reshape(DEC_BATCH, SSD_D_INNER, SSD_D_STATE))
    ret_s, r_s = _outer_state(gam, k_s, q_s, sp_main[:, SSD_D_INNER:SSD_D_INNER + RET_V], state_ret[0])
    ycat_s = _ab_sample_post(y_s, xs_c, sp_main, r_s, d_skip_e, s_norm)
    xs_ = _proj_residual(xs_, ycat_s, w_out_ab, 0)
    xs_ = xattn_s(xs_, 0)
    xs_ = _ffn(xs_, n_ffn2, f2_in, f2_out, 0)
    xs_ = _ffn(xs_, n_ffn1, f1_in, f1_out, 1)
    sc_main, sc_gate = _norm_proj(xs_, n_mix, 1, w_c_main, 0, 1024, w_c_gate)
    m_in = jnp.pad(state_mlstm_m[0], ((0, 0), (0, LANES - M_HEADS)))
    dpe, kw, mn_s, mm_s, den = _c_sample_prep(sc_main, sc_gate, state_mlstm_n.reshape(DEC_BATCH, M_QK), m_in,
                                              i_bias, f_bias)
    mc_s, num = _outer_state(dpe, kw, sc_main[:, :M_QK], sc_main[:, 2 * M_QK:2 * M_QK + M_V], state_mlstm_c[0])
    hout_s = _c_sample_post(num, den, sc_main, m_norm)
    xs_ = _proj_residual(xs_, hout_s, w_out_c, 0)
    xs_ = xattn_s(xs_, 1)
    y_sample = _ffn(xs_, n_ffn2, f2_in, f2_out, 1, norm_final).reshape(DEC_BATCH, 1, D_MODEL)

    kv_shape = (DEPTH, BATCH, MEM_LEN, X_HEADS, X_HEAD_DIM)
    return (y_prompt, y_sample, memk.reshape(kv_shape), memv.reshape(kv_shape),
            conv_p.reshape(1, BATCH, SSD_CONV - 1, SSD_CONV_CH),
            conv_s.reshape(1, DEC_BATCH, SSD_CONV - 1, SSD_CONV_CH),
            ssm_p.reshape(1, BATCH, SSD_HEADS, SSD_HEAD_DIM, SSD_D_STATE),
            ssm_s.reshape(1, DEC_BATCH, SSD_HEADS, SSD_HEAD_DIM, SSD_D_STATE),
            ret_p[None], ret_s[None], mc_p[None], mc_s[None],
            mn_p[None], mn_s.reshape(1, DEC_BATCH, M_HEADS, M_DK),
            mm_p[:, :M_HEADS, 0][None], mm_s[:, :M_HEADS][None])


def np_cumsum(sizes):
    out, acc = [], 0
    for s in sizes[:-1]:
        acc += s
        out.append(acc)
    return out
```

```python
import functools
import math

import jax
import jax.numpy as jnp
from jax import lax
from jax.experimental import pallas as pl
from jax.experimental.pallas import tpu as pltpu

F32 = jnp.float32
BF16 = jnp.bfloat16
EPS = 1e-6

D_MODEL = 1024
BATCH = 8
SEQ = 2048
DEPTH = 2
DEC_BATCH = 128
PAST_LEN = 16384
CHUNK = 128
D_FF = 2816
SSD_D_INNER = 2 * D_MODEL
SSD_HEAD_DIM = 64
SSD_HEADS = SSD_D_INNER // SSD_HEAD_DIM
SSD_GROUPS = 4
SSD_D_STATE = 128
SSD_CONV = 4
SSD_CONV_CH = SSD_D_INNER + 2 * SSD_GROUPS * SSD_D_STATE
RET_HEADS = 4
RET_QK = D_MODEL
RET_V = 2 * D_MODEL
RET_DK = RET_QK // RET_HEADS
RET_DV = RET_V // RET_HEADS
ROPE_BASE = 10000.0
AB_SIZES = (SSD_D_INNER, SSD_CONV_CH, SSD_HEADS, RET_QK, RET_QK, RET_V, RET_V)
M_HEADS = 4
M_QK = D_MODEL // 2
M_V = D_MODEL
M_DK = M_QK // M_HEADS
M_DV = M_V // M_HEADS
C_SIZES = (M_QK, M_QK, M_V, M_HEADS, M_HEADS, M_V)
MEM_LEN = 256
X_HEADS = 4
X_HEAD_DIM = D_MODEL // X_HEADS

LANES = 128
SUBLANES = 8
GROUP_W = SSD_D_INNER // SSD_GROUPS
HEADS_PER_GROUP = SSD_HEADS // SSD_GROUPS
ROW_TILE = 1024
FF_TILE = 256
SB = 8
XB = 4
CB = 2


def _cp(sem, mib):
    return pltpu.CompilerParams(dimension_semantics=sem, vmem_limit_bytes=mib * 1024 * 1024)


def _bf(x):
    return x.astype(BF16)


def _dot(a, b):
    return jnp.dot(a, b, preferred_element_type=F32)


def _dot_nt(a, b):
    return lax.dot_general(a, b, (((1,), (1,)), ((), ())), preferred_element_type=F32)


def _dot_tn(a, b):
    return lax.dot_general(a, b, (((0,), (0,)), ((), ())), preferred_element_type=F32)


def _rms(x):
    return x * lax.rsqrt(jnp.mean(x * x, axis=-1, keepdims=True) + EPS)


def _silu(x):
    return x * jax.nn.sigmoid(x)


def _softplus(x):
    return jnp.maximum(x, 0.0) + jnp.log1p(jnp.exp(-jnp.abs(x)))


def _split3(x):
    hi = x.astype(BF16)
    r = x - hi.astype(F32)
    mid = r.astype(BF16)
    lo = (r - mid.astype(F32)).astype(BF16)
    return hi, mid, lo


def _cumsum_rows(x):
    n = x.shape[0]
    r = lax.broadcasted_iota(jnp.int32, (n, n), 0)
    c = lax.broadcasted_iota(jnp.int32, (n, n), 1)
    t = jnp.where(r >= c, 1.0, 0.0).astype(BF16)
    hi, mid, lo = _split3(x)
    return _dot(t, hi) + _dot(t, mid) + _dot(t, lo)


def _lane_bcast(x, h, width=LANES):
    return jnp.broadcast_to(x[:, h:h + 1], (x.shape[0], width))


def _pair_expand(x, n_heads):
    rows = x.shape[0]
    lo = lax.broadcasted_iota(jnp.int32, (rows, LANES), 1) < SSD_HEAD_DIM
    return jnp.concatenate(
        [jnp.where(lo, _lane_bcast(x, 2 * j), _lane_bcast(x, 2 * j + 1)) for j in range(n_heads // 2)], axis=1)


def _pad_t(x):
    pad = jnp.zeros((LANES - x.shape[0], x.shape[1]), F32)
    return jnp.concatenate([x, pad], axis=0).T


def _ffn_body(*refs, nf, final):
    if final:
        x_ref, g_ref, wi_ref, wo_ref, fg_ref, o_ref = refs
    else:
        x_ref, g_ref, wi_ref, wo_ref, o_ref = refs
    x = x_ref[...]
    xn = _bf(_rms(x) * g_ref[...])
    acc = None
    for f in range(nf):
        fs = slice(f * FF_TILE, (f + 1) * FF_TILE)
        g = _dot(xn, wi_ref[:, fs])
        u = _dot(xn, wi_ref[:, D_FF + f * FF_TILE:D_FF + (f + 1) * FF_TILE])
        t = _dot(_bf(_silu(g) * u), wo_ref[fs, :])
        acc = t if acc is None else acc + t
    y = x + 0.5 * acc
    if final:
        y = _rms(y) * fg_ref[...]
    o_ref[...] = y


def _ffn(x, g3, w_in, w_out, layer, final_g=None):
    m = x.shape[0]
    tm = min(m, ROW_TILE)
    nf = D_FF // FF_TILE
    once = pl.Buffered(1)
    in_specs = [
        pl.BlockSpec((tm, D_MODEL), lambda i: (i, 0)),
        pl.BlockSpec((None, 1, D_MODEL), lambda i: (layer, 0, 0)),
        pl.BlockSpec((None, D_MODEL, 2 * D_FF), lambda i: (layer, 0, 0), pipeline_mode=once),
        pl.BlockSpec((None, D_FF, D_MODEL), lambda i: (layer, 0, 0), pipeline_mode=once),
    ]
    args = [x, g3, w_in, w_out]
    if final_g is not None:
        in_specs.append(pl.BlockSpec((1, D_MODEL), lambda i: (0, 0)))
        args.append(final_g.reshape(1, D_MODEL))
    return pl.pallas_call(
        functools.partial(_ffn_body, nf=nf, final=final_g is not None),
        grid=(m // tm,),
        in_specs=in_specs,
        out_specs=pl.BlockSpec((tm, D_MODEL), lambda i: (i, 0)),
        out_shape=jax.ShapeDtypeStruct((m, D_MODEL), F32),
        compiler_params=_cp(("parallel",), 56),
        name="ffn",
    )(*args)


def _norm_proj_body(*refs, small):
    if small:
        x_ref, g_ref, w_ref, ws_ref, o_ref, os_ref, xn_ref = refs
    else:
        x_ref, g_ref, w_ref, o_ref, xn_ref = refs
    n = pl.program_id(1)

    @pl.when(n == 0)
    def _():
        xn = _bf(_rms(x_ref[...]) * g_ref[...])
        xn_ref[...] = xn
        if small:
            os_ref[...] = _dot(xn, _bf(ws_ref[...]))

    o_ref[...] = _dot(xn_ref[...], _bf(w_ref[...]))


def _norm_proj(x, g3, glayer, w3, wlayer, tn, w_small=None, col0=0, n_out=None):
    m = x.shape[0]
    tm = min(m, ROW_TILE)
    n_out = w3.shape[-1] if n_out is None else n_out
    in_specs = [
        pl.BlockSpec((tm, D_MODEL), lambda i, n: (i, 0)),
        pl.BlockSpec((None, 1, D_MODEL), lambda i, n: (glayer, 0, 0)),
        pl.BlockSpec((None, D_MODEL, tn), lambda i, n: (wlayer, 0, n + col0)),
    ]
    args = [x, g3, w3]
    out_specs = [pl.BlockSpec((tm, tn), lambda i, n: (i, n))]
    out_shape = [jax.ShapeDtypeStruct((m, n_out), F32)]
    if w_small is not None:
        ns = w_small.shape[-1]
        in_specs.append(pl.BlockSpec((None, D_MODEL, ns), lambda i, n: (0, 0, 0)))
        args.append(w_small)
        out_specs.append(pl.BlockSpec((tm, ns), lambda i, n: (i, 0)))
        out_shape.append(jax.ShapeDtypeStruct((m, ns), F32))
    res = pl.pallas_call(
        functools.partial(_norm_proj_body, small=w_small is not None),
        grid=(m // tm, n_out // tn),
        in_specs=in_specs,
        out_specs=out_specs,
        out_shape=out_shape,
        scratch_shapes=[pltpu.VMEM((tm, D_MODEL), BF16)],
        compiler_params=_cp(("parallel", "arbitrary"), 48),
        name="norm_proj",
    )(*args)
    return res if w_small is not None else res[0]


AB_TN = 1024
AB_XBC0 = (SSD_D_INNER + 2 * RET_V) // AB_TN
AB_RQ = AB_XBC0 + SSD_CONV_CH // AB_TN


AB_GATE_BLOCKS = AB_XBC0
AB_XQK_BLOCKS = SSD_CONV_CH // AB_TN + 2 * RET_QK // AB_TN
AB_ROW_TILE = 512


def _ab_gates_body(x_ref, g_ref, w_ref, o_ref):
    xn = _bf(_rms(x_ref[...]) * g_ref[...])
    nz = SSD_D_INNER // AB_TN
    nv = RET_V // AB_TN
    for n in range(AB_GATE_BLOCKS):
        cs = slice(n * AB_TN, (n + 1) * AB_TN)
        r = _dot(xn, w_ref[:, cs])
        o_ref[:, cs] = _bf(r) if nz <= n < nz + nv else _bf(_silu(r))


def _ab_xqk_body(x_ref, g_ref, w_ref, ws_ref, dtb_ref, cw_ref, cb_ref, cos_ref, sin_ref,
                 o_ref, dt_ref, xpad, rbuf, ybuf, carry, *, tiles_per_seq):
    i = pl.program_id(0)
    tm = x_ref.shape[0]
    nslab = tm // SUBLANES
    pitch = nslab + SUBLANES
    nxb = SSD_CONV_CH // AB_TN
    ntap = SSD_CONV - 1

    @pl.when(i == 0)
    def _():
        carry[...] = jnp.zeros_like(carry)

    g = g_ref[...]
    xn = _bf(_rms(x_ref[...]) * g)
    dt_ref[...] = _softplus(_dot(xn, ws_ref[...]) + dtb_ref[...])
    for lb in range(D_MODEL // LANES):
        for s in range(SUBLANES):
            xpad[lb, s * pitch:s * pitch + nslab, :] = x_ref[s * nslab:(s + 1) * nslab, lb * LANES:(lb + 1) * LANES]
    xp = jnp.concatenate(
        [jnp.concatenate([xpad[lb, pl.ds(v, SUBLANES, stride=pitch), :] for lb in range(D_MODEL // LANES)], axis=1)
         for v in range(nslab)], axis=0)
    xnp = _bf(_rms(xp) * g)
    seq_start = i % tiles_per_seq == 0
    first = lax.broadcasted_iota(jnp.int32, (SUBLANES, AB_TN), 0) == 0
    halo = ntap * SUBLANES
    for n in range(nxb):
        cs = slice(n * AB_TN, (n + 1) * AB_TN)
        rbuf[n, halo:halo + tm, :] = _dot(xnp, w_ref[:, cs])
        prev_rows = jnp.where(seq_start, 0.0, carry[n])
        for k in range(ntap):
            hi = rbuf[n, tm + k * SUBLANES:tm + (k + 1) * SUBLANES, :]
            rbuf[n, k * SUBLANES:(k + 1) * SUBLANES, :] = jnp.where(first, prev_rows[k:k + 1, :],
                                                                     pltpu.roll(hi, 1, axis=0))
            carry[n, k:k + 1, :] = hi[SUBLANES - 1:SUBLANES, :]
        w = cw_ref[:, cs]
        conv = cb_ref[:, cs] + rbuf[n, halo:halo + tm, :] * w[ntap:ntap + 1, :]
        for j in range(1, SSD_CONV):
            off = halo - j * SUBLANES
            conv = conv + rbuf[n, off:off + tm, :] * w[ntap - j:ntap - j + 1, :]
        y = _silu(conv)
        for lb in range(AB_TN // LANES):
            ls = slice(lb * LANES, (lb + 1) * LANES)
            for v in range(nslab):
                ybuf[n, lb, pl.ds(v, SUBLANES, stride=pitch), :] = y[v * SUBLANES:(v + 1) * SUBLANES, ls]
            for s in range(SUBLANES):
                o_ref[s * nslab:(s + 1) * nslab, n * AB_TN + ls.start:n * AB_TN + ls.stop] = _bf(
                    ybuf[n, lb, s * pitch:s * pitch + nslab, :])
    cos = cos_ref[...]
    sin = sin_ref[...]
    for n in range(nxb, AB_XQK_BLOCKS):
        r = _dot(xn, w_ref[:, n * AB_TN:(n + 1) * AB_TN])
        scale = 1.0 if n == nxb else RET_DK ** -0.5
        for h in range(AB_TN // RET_DK):
            a = slice(h * RET_DK, h * RET_DK + LANES)
            b = slice(h * RET_DK + LANES, (h + 1) * RET_DK)
            x1, x2 = r[:, a], r[:, b]
            o_ref[:, n * AB_TN + a.start:n * AB_TN + a.stop] = _bf((x1 * cos - x2 * sin) * scale)
            o_ref[:, n * AB_TN + b.start:n * AB_TN + b.stop] = _bf((x1 * sin + x2 * cos) * scale)


def _ab_inproj(x, g3, w_gates, w_xqk, w_dt, dt_bias, conv_w, conv_b, cos, sin):
    m = x.shape[0]
    tm = AB_ROW_TILE
    tps = SEQ // tm
    nxb = SSD_CONV_CH // AB_TN
    once = pl.Buffered(1)

    def full(a):
        return pl.BlockSpec(a.shape, lambda i: (0,) * a.ndim, pipeline_mode=once)

    xspec = pl.BlockSpec((tm, D_MODEL), lambda i: (i, 0))
    gspec = pl.BlockSpec((None, 1, D_MODEL), lambda i: (0, 0, 0))
    gates = pl.pallas_call(
        _ab_gates_body,
        grid=(m // tm,),
        in_specs=[xspec, gspec, full(w_gates)],
        out_specs=pl.BlockSpec((tm, AB_GATE_BLOCKS * AB_TN), lambda i: (i, 0)),
        out_shape=jax.ShapeDtypeStruct((m, AB_GATE_BLOCKS * AB_TN), BF16),
        compiler_params=_cp(("parallel",), 56),
        name="ab_gates",
    )(x, g3, w_gates)
    xqk, dt = pl.pallas_call(
        functools.partial(_ab_xqk_body, tiles_per_seq=tps),
        grid=(m // tm,),
        in_specs=[xspec, gspec, full(w_xqk), full(w_dt), full(dt_bias), full(conv_w), full(conv_b),
                  pl.BlockSpec((tm, LANES), lambda i: (i % tps, 0)),
                  pl.BlockSpec((tm, LANES), lambda i: (i % tps, 0))],
        out_specs=[pl.BlockSpec((tm, AB_XQK_BLOCKS * AB_TN), lambda i: (i, 0)),
                   pl.BlockSpec((tm, LANES), lambda i: (i, 0))],
        out_shape=[jax.ShapeDtypeStruct((m, AB_XQK_BLOCKS * AB_TN), BF16), jax.ShapeDtypeStruct((m, LANES), F32)],
        scratch_shapes=[pltpu.VMEM((D_MODEL // LANES, tm + SUBLANES * SUBLANES, LANES), F32),
                        pltpu.VMEM((nxb, (SSD_CONV - 1) * SUBLANES + tm, AB_TN), F32),
                        pltpu.VMEM((nxb, AB_TN // LANES, tm + SUBLANES * SUBLANES, LANES), F32),
                        pltpu.VMEM((nxb, SUBLANES, AB_TN), F32)],
        compiler_params=_cp(("arbitrary",), 56),
        name="ab_xqk",
    )(x, g3, w_xqk, w_dt, dt_bias, conv_w, conv_b, cos, sin)
    return gates, xqk, dt


def _proj_res_body(x_ref, y_ref, w_ref, o_ref):
    k = pl.program_id(1)

    @pl.when(k == 0)
    def _():
        o_ref[...] = x_ref[...]

    o_ref[...] += _dot(_bf(y_ref[...]), _bf(w_ref[...]))


def _proj_residual(x, y, w3, layer):
    m = x.shape[0]
    tm = min(m, ROW_TILE)
    kdim = y.shape[1]
    tk = min(kdim, 1024)
    return pl.pallas_call(
        _proj_res_body,
        grid=(m // tm, kdim // tk),
        in_specs=[
            pl.BlockSpec((tm, D_MODEL), lambda i, k: (i, 0)),
            pl.BlockSpec((tm, tk), lambda i, k: (i, k)),
            pl.BlockSpec((None, tk, D_MODEL), lambda i, k: (layer, k, 0)),
        ],
        out_specs=pl.BlockSpec((tm, D_MODEL), lambda i, k: (i, 0)),
        out_shape=jax.ShapeDtypeStruct((m, D_MODEL), F32),
        compiler_params=_cp(("parallel", "arbitrary"), 48),
        name="proj_residual",
    )(x, y, w3)


def _mem_kv_body(x_ref, g_ref, w_ref, k_ref, v_ref):
    xn = _bf(_rms(x_ref[...]) * g_ref[...])
    kv = _dot(xn, _bf(w_ref[...]))
    k_ref[...] = kv[:, :D_MODEL]
    v_ref[...] = kv[:, D_MODEL:]


def _mem_kv(mem2d, g3, w_xkv):
    m = mem2d.shape[0]
    tm = 512
    shp = jax.ShapeDtypeStruct((DEPTH, m, D_MODEL), F32)
    return pl.pallas_call(
        _mem_kv_body,
        grid=(DEPTH, m // tm),
        in_specs=[
            pl.BlockSpec((tm, D_MODEL), lambda l, i: (i, 0)),
            pl.BlockSpec((None, 1, D_MODEL), lambda l, i: (l, 0, 0)),
            pl.BlockSpec((None, D_MODEL, 2 * D_MODEL), lambda l, i: (l, 0, 0)),
        ],
        out_specs=[pl.BlockSpec((None, tm, D_MODEL), lambda l, i: (l, i, 0))] * 2,
        out_shape=[shp, shp],
        compiler_params=_cp(("arbitrary", "arbitrary"), 48),
        name="mem_kv",
    )(mem2d, g3, w_xkv)


def _xattn_prompt_body(x_ref, g_ref, wq_ref, wo_ref, k_ref, v_ref, o_ref):
    x = x_ref[...]
    xn = _bf(_rms(x) * g_ref[...])
    q = _dot(xn, _bf(wq_ref[...]))
    k = _bf(k_ref[...])
    v = _bf(v_ref[...])
    outs = []
    for h in range(X_HEADS):
        sl = slice(h * X_HEAD_DIM, (h + 1) * X_HEAD_DIM)
        s = _dot_nt(_bf(q[:, sl]), k[:, sl]) * (X_HEAD_DIM ** -0.5)
        e = jnp.exp(s - jnp.max(s, axis=-1, keepdims=True))
        p = e / jnp.sum(e, axis=-1, keepdims=True)
        outs.append(_bf(_dot(_bf(p), v[:, sl])))
    o_ref[...] = x + _dot(jnp.concatenate(outs, axis=1), _bf(wo_ref[...]))


def _xattn_prompt(x, g3, w_xq, w_xo, memk, memv, layer):
    tq = 512
    nq = SEQ // tq
    return pl.pallas_call(
        _xattn_prompt_body,
        grid=(BATCH, nq),
        in_specs=[
            pl.BlockSpec((tq, D_MODEL), lambda b, j: (b * nq + j, 0)),
            pl.BlockSpec((None, 1, D_MODEL), lambda b, j: (layer, 0, 0)),
            pl.BlockSpec((None, D_MODEL, D_MODEL), lambda b, j: (layer, 0, 0)),
            pl.BlockSpec((None, D_MODEL, D_MODEL), lambda b, j: (layer, 0, 0)),
            pl.BlockSpec((None, MEM_LEN, D_MODEL), lambda b, j: (layer, b, 0)),
            pl.BlockSpec((None, MEM_LEN, D_MODEL), lambda b, j: (layer, b, 0)),
        ],
        out_specs=pl.BlockSpec((tq, D_MODEL), lambda b, j: (b * nq + j, 0)),
        out_shape=jax.ShapeDtypeStruct((BATCH * SEQ, D_MODEL), F32),
        compiler_params=_cp(("parallel", "arbitrary"), 48),
        name="xattn_prompt",
    )(x, g3, w_xq, w_xo, memk, memv)


def _xattn_sample_body(q_ref, k_ref, v_ref, o_ref):
    for b in range(XB):
        s = jnp.sum(k_ref[b] * q_ref[b][None], axis=-1, keepdims=True) * (X_HEAD_DIM ** -0.5)
        e = jnp.exp(s - jnp.max(s, axis=0, keepdims=True))
        p = e / jnp.sum(e, axis=0, keepdims=True)
        o_ref[b] = jnp.sum(p * v_ref[b], axis=0)


def _xattn_sample(q, cache_k, cache_v, layer):
    blk = pl.BlockSpec((None, XB, MEM_LEN, X_HEADS, X_HEAD_DIM), lambda i: (layer, i, 0, 0, 0))
    qo = pl.BlockSpec((XB, X_HEADS, X_HEAD_DIM), lambda i: (i, 0, 0))
    return pl.pallas_call(
        _xattn_sample_body,
        grid=(DEC_BATCH // XB,),
        in_specs=[qo, blk, blk],
        out_specs=qo,
        out_shape=jax.ShapeDtypeStruct((DEC_BATCH, X_HEADS, X_HEAD_DIM), F32),
        compiler_params=_cp(("parallel",), 48),
        name="xattn_sample",
    )(q, cache_k, cache_v)


def _ret_log_gamma(h):
    return math.log1p(-(2.0 ** (-5.0 - h)))


def _ab_prompt_body(z_ref, rv_ref, rg_ref, xbc_ref, rq_ref, rk_ref, dt_ref, alog_ref, dsk_ref, nrm_ref,
                    y_ref, h_ref, s_ref):
    c = pl.program_id(1)

    @pl.when(c == 0)
    def _():
        h_ref[...] = jnp.zeros_like(h_ref)
        s_ref[...] = jnp.zeros_like(s_ref)

    xs = xbc_ref[:, :SSD_D_INNER].astype(F32)
    bm = xbc_ref[:, SSD_D_INNER:SSD_D_INNER + GROUP_W]
    cm = xbc_ref[:, SSD_D_INNER + GROUP_W:]

    dt = dt_ref[...]
    da = dt * (-jnp.exp(alog_ref[...]))
    cs = _cumsum_rows(da)
    cs_t = cs.T
    row = lax.broadcasted_iota(jnp.int32, (CHUNK, CHUNK), 0)
    col = lax.broadcasted_iota(jnp.int32, (CHUNK, CHUNK), 1)
    tri = row >= col
    lo = col < SSD_HEAD_DIM
    for g in range(SSD_GROUPS):
        gs = slice(g * GROUP_W, (g + 1) * GROUP_W)
        ns = slice(g * SSD_D_STATE, (g + 1) * SSD_D_STATE)
        cmg = cm[:, ns]
        bmg = bm[:, ns]
        att = _dot_nt(cmg, bmg)
        hprev = h_ref[gs, :]
        yint = _dot_nt(cmg, _bf(hprev))
        ys, wxs, css = [], [], []
        for j in range(HEADS_PER_GROUP // 2):
            h0 = g * HEADS_PER_GROUP + 2 * j
            cb0 = _lane_bcast(cs, h0)
            cb1 = _lane_bcast(cs, h0 + 1)
            cs_p = jnp.where(lo, cb0, cb1)
            dt_p = jnp.where(lo, _lane_bcast(dt, h0), _lane_bcast(dt, h0 + 1))
            off = g * GROUP_W + j * LANES
            xdt = xs[:, off:off + LANES] * dt_p
            d0 = jnp.exp(jnp.where(tri, cb0 - cs_t[h0:h0 + 1, :], -jnp.inf))
            d1 = jnp.exp(jnp.where(tri, cb1 - cs_t[h0 + 1:h0 + 2, :], -jnp.inf))
            yy = _dot(jnp.concatenate([_bf(att * d0), _bf(att * d1)], axis=0), _bf(xdt))
            ys.append(jnp.where(lo, yy[:CHUNK], yy[CHUNK:]) + yint[:, j * LANES:(j + 1) * LANES] * jnp.exp(cs_p))
            wxs.append(_bf(xdt * jnp.exp(cs_p[CHUNK - 1:CHUNK, :] - cs_p)))
            css.append(cs_p)
        cs_g = jnp.concatenate(css, axis=1)
        last_t = jnp.broadcast_to(cs_g[CHUNK - 1:CHUNK, :], (CHUNK, GROUP_W)).T
        h_ref[gs, :] = hprev * jnp.exp(last_t) + _dot_tn(jnp.concatenate(wxs, axis=1), bmg)
        yg = jnp.concatenate(ys, axis=1)
        yg = (yg + xs[:, gs] * dsk_ref[:, gs]) * z_ref[:, gs].astype(F32)
        y_ref[:, gs] = _bf(_rms(yg) * nrm_ref[:, gs])

    tcol = row.astype(F32)
    diff = tcol - col.astype(F32)
    for h in range(RET_HEADS):
        lg = _ret_log_gamma(h)
        ks = slice(h * RET_DK, (h + 1) * RET_DK)
        vs = slice(h * RET_DV, (h + 1) * RET_DV)
        qb = rq_ref[:, ks]
        kb = rk_ref[:, ks]
        decay = jnp.exp(jnp.where(tri, diff * lg, -jnp.inf))
        att = _dot_nt(qb, kb) * decay
        vb = rv_ref[:, vs]
        s_prev = s_ref[h]
        inner = jnp.exp((tcol + 1.0) * lg)
        r = _dot(_bf(att), vb) + _dot(qb, _bf(s_prev)) * jnp.concatenate([inner] * (RET_DV // LANES), axis=1)
        tail_w = jnp.exp((CHUNK - 1.0 - tcol) * lg)
        kt = _bf(kb.astype(F32) * jnp.concatenate([tail_w] * (RET_DK // LANES), axis=1))
        s_ref[h] = s_prev * math.exp(CHUNK * lg) + _dot_tn(kt, vb)
        os = slice(SSD_D_INNER + h * RET_DV, SSD_D_INNER + (h + 1) * RET_DV)
        y_ref[:, os] = _bf(rg_ref[:, vs].astype(F32) * _rms(r))


def _mixer_ab_prompt(p_gates, p_xqk, p_dt, a_log, d_skip_e, ssd_norm):
    nc = SEQ // CHUNK
    m = BATCH * SEQ

    def rowspec(width, cb):
        return pl.BlockSpec((CHUNK, width), lambda b, c: (b * nc + c, cb))

    def full(a):
        return pl.BlockSpec(a.shape, lambda b, c: (0,) * a.ndim)

    params = [a_log, d_skip_e, ssd_norm]
    return pl.pallas_call(
        _ab_prompt_body,
        grid=(BATCH, nc),
        in_specs=[rowspec(SSD_D_INNER, 0), rowspec(RET_V, 1), rowspec(RET_V, 2), rowspec(SSD_CONV_CH, 0),
                  rowspec(RET_QK, SSD_CONV_CH // RET_QK), rowspec(RET_QK, SSD_CONV_CH // RET_QK + 1),
                  rowspec(LANES, 0)] + [full(a) for a in params],
        out_specs=[
            pl.BlockSpec((CHUNK, SSD_D_INNER + RET_V), lambda b, c: (b * nc + c, 0)),
            pl.BlockSpec((None, SSD_D_INNER, SSD_D_STATE), lambda b, c: (b, 0, 0)),
            pl.BlockSpec((None, RET_HEADS, RET_DK, RET_DV), lambda b, c: (b, 0, 0, 0)),
        ],
        out_shape=[
            jax.ShapeDtypeStruct((m, SSD_D_INNER + RET_V), BF16),
            jax.ShapeDtypeStruct((BATCH, SSD_D_INNER, SSD_D_STATE), F32),
            jax.ShapeDtypeStruct((BATCH, RET_HEADS, RET_DK, RET_DV), F32),
        ],
        compiler_params=_cp(("parallel", "arbitrary"), 48),
        name="mixer_ab_prompt",
    )(p_gates, p_gates, p_gates, p_xqk, p_xqk, p_xqk, p_dt, *params)


def _c_prompt_body(q_ref, k_ref, v_ref, o_ref, gt_ref, ib_ref, fb_ref, nrm_ref,
                   h_ref, c_ref, n_ref, m_ref, m_s):
    @pl.when(pl.program_id(1) == 0)
    def _():
        c_ref[...] = jnp.zeros_like(c_ref)
        n_ref[...] = jnp.zeros_like(n_ref)
        m_s[...] = jnp.zeros_like(m_s)

    for s in range(CB):
        _c_chunk(q_ref.at[s], k_ref.at[s], v_ref.at[s], o_ref.at[s], gt_ref.at[s], ib_ref, fb_ref, nrm_ref,
                 h_ref.at[s], c_ref.at[s], n_ref.at[s], m_ref.at[s], m_s.at[s])


def _c_chunk(q_ref, k_ref, v_ref, o_ref, gt_ref, ib_ref, fb_ref, nrm_ref, h_ref, c_ref, n_ref, m_ref, m_s):
    ipre = gt_ref[:, :LANES] + ib_ref[...]
    lf = -_softplus(-(gt_ref[:, LANES:] + fb_ref[...]))
    b = _cumsum_rows(lf)
    g = ipre - b
    g_t = g.T
    b_t = b.T
    row = lax.broadcasted_iota(jnp.int32, (CHUNK, CHUNK), 0)
    col = lax.broadcasted_iota(jnp.int32, (CHUNK, CHUNK), 1)
    tri = row >= col
    cmax = g_t
    sh = 1
    while sh < CHUNK:
        cmax = jnp.maximum(cmax, jnp.where(col >= sh, pltpu.roll(cmax, sh, axis=1), -jnp.inf))
        sh *= 2
    m_prev = m_s[...]
    mt_t = b_t + jnp.maximum(m_prev, cmax)
    mt = mt_t.T
    m_prev_c = m_prev.T
    inter = jnp.exp(b + m_prev_c - mt)
    emt = jnp.exp(-mt)
    wl = jnp.exp(g + b[CHUNK - 1:CHUNK, :] - mt[CHUNK - 1:CHUNK, :])
    bm = b - mt
    m_new = _lane_bcast(mt_t, CHUNK - 1)
    dp_t = jnp.exp(_lane_bcast(b_t, CHUNK - 1) + m_prev - m_new)
    m_s[...] = m_new
    m_ref[...] = m_new[0:SUBLANES, :]
    outs = []
    for h in range(M_HEADS):
        ks = slice(h * M_DK, (h + 1) * M_DK)
        vs = slice(h * M_DV, (h + 1) * M_DV)
        wgt = jnp.exp(jnp.where(tri, g_t[h:h + 1, :] + bm[:, h:h + 1], -jnp.inf))
        qh = q_ref[:, ks]
        kh = k_ref[:, ks] * (M_DK ** -0.5)
        qb = _bf(qh)
        vb = _bf(v_ref[:, vs])
        a = _dot_nt(qb, _bf(kh)) * wgt
        c_prev = c_ref[h]
        n_prev = n_ref[h:h + 1, :]
        ic = inter[:, h:h + 1]
        num = _dot(_bf(a), vb) + _dot(qb, _bf(c_prev)) * ic
        den = jnp.sum(a, axis=1, keepdims=True) + jnp.sum(qh * n_prev, axis=1, keepdims=True) * ic
        outs.append(num / jnp.maximum(jnp.abs(den), emt[:, h:h + 1]))
        kw = kh * wl[:, h:h + 1]
        dp_row = dp_t[h:h + 1, :]
        c_ref[h] = c_prev * jnp.concatenate([dp_row] * (M_DV // LANES), axis=1) + _dot_tn(_bf(kw), vb)
        n_ref[h:h + 1, :] = n_prev * dp_row + jnp.sum(kw, axis=0, keepdims=True)
    for h in range(M_HEADS):
        vs = slice(h * M_DV, (h + 1) * M_DV)
        h_ref[:, vs] = _bf(jax.nn.sigmoid(o_ref[:, vs]) * (_rms(outs[h]) * nrm_ref[:, vs]))


def _mixer_c_prompt(p_main, p_gate, i_bias, f_bias, norm_g):
    nc = SEQ // CHUNK

    def rowspec(width, cb):
        return pl.BlockSpec((CB, CHUNK, width), lambda b, c: (b, c, cb))

    def full(a):
        return pl.BlockSpec(a.shape, lambda b, c: (0,) * a.ndim)

    params = [i_bias, f_bias, norm_g]
    return pl.pallas_call(
        _c_prompt_body,
        grid=(BATCH // CB, nc),
        in_specs=[rowspec(M_QK, 0), rowspec(M_QK, 1), rowspec(M_V, 1), rowspec(M_V, 2), rowspec(2 * LANES, 0)]
        + [full(a) for a in params],
        out_specs=[
            pl.BlockSpec((CB, CHUNK, M_V), lambda b, c: (b, c, 0)),
            pl.BlockSpec((CB, M_HEADS, M_DK, M_DV), lambda b, c: (b, 0, 0, 0)),
            pl.BlockSpec((CB, M_HEADS, M_DK), lambda b, c: (b, 0, 0)),
            pl.BlockSpec((CB, SUBLANES, LANES), lambda b, c: (b, 0, 0)),
        ],
        out_shape=[
            jax.ShapeDtypeStruct((BATCH, SEQ, M_V), BF16),
            jax.ShapeDtypeStruct((BATCH, M_HEADS, M_DK, M_DV), F32),
            jax.ShapeDtypeStruct((BATCH, M_HEADS, M_DK), F32),
            jax.ShapeDtypeStruct((BATCH, SUBLANES, LANES), F32),
        ],
        scratch_shapes=[pltpu.VMEM((CB, CHUNK, LANES), F32)],
        compiler_params=_cp(("parallel", "arbitrary"), 48),
        name="mixer_c_prompt",
    )(p_main, p_main, p_main, p_main, p_gate, *params)


def _ab_sample_prep_body(xbc_ref, rq_ref, rk_ref, dt_ref, cst_ref, cos_ref, sin_ref,
                         cw_ref, cb_ref, dtb_ref, alog_ref,
                         conv_ref, xs_ref, xdt_ref, eda_ref, bm_ref, cm_ref, q_ref, k_ref, gam_ref):
    ch = SSD_CONV_CH
    u = xbc_ref[...]
    w = cw_ref[...]
    b0 = cst_ref[:, 0:ch]
    b1 = cst_ref[:, ch:2 * ch]
    b2 = cst_ref[:, 2 * ch:3 * ch]
    conv = cb_ref[...] + (((b0 * w[0:1, :] + b1 * w[1:2, :]) + b2 * w[2:3, :]) + u * w[3:4, :])
    conv_ref[:, 0:ch] = b1
    conv_ref[:, ch:2 * ch] = b2
    conv_ref[:, 2 * ch:3 * ch] = u
    xbc = _silu(conv)
    xs = xbc[:, :SSD_D_INNER]
    xs_ref[...] = xs
    bm_ref[...] = xbc[:, SSD_D_INNER:SSD_D_INNER + GROUP_W]
    cm_ref[...] = xbc[:, SSD_D_INNER + GROUP_W:]
    dt = _softplus(dt_ref[...] + dtb_ref[...])
    eda = jnp.exp(dt * (-jnp.exp(alog_ref[...])))
    xdt_ref[...] = xs * _pair_expand(dt, SSD_HEADS)
    eda_ref[...] = _pair_expand(eda, SSD_HEADS)
    cos = cos_ref[...]
    sin = sin_ref[...]
    for h in range(RET_HEADS):
        a = slice(h * RET_DK, h * RET_DK + LANES)
        b = slice(h * RET_DK + LANES, (h + 1) * RET_DK)
        q1, q2 = rq_ref[:, a], rq_ref[:, b]
        k1, k2 = rk_ref[:, a], rk_ref[:, b]
        q_ref[:, a] = q1 * cos - q2 * sin
        q_ref[:, b] = q1 * sin + q2 * cos
        k_ref[:, a] = (k1 * cos - k2 * sin) * (RET_DK ** -0.5)
        k_ref[:, b] = (k1 * sin + k2 * cos) * (RET_DK ** -0.5)
        gam_ref[:, h * RET_DK:(h + 1) * RET_DK] = jnp.full((DEC_BATCH, RET_DK), math.exp(_ret_log_gamma(h)), F32)


def _ab_sample_prep(p_main, p_dt, conv_state, cos, sin, conv_w, conv_b, dt_bias, a_log):
    n = DEC_BATCH

    def colspec(width, cb):
        return pl.BlockSpec((n, width), lambda i: (0, cb))

    def full(a):
        return pl.BlockSpec(a.shape, lambda i: (0,) * a.ndim)

    small = [conv_state, cos, sin, conv_w, conv_b, dt_bias, a_log]

    def out(width):
        return jax.ShapeDtypeStruct((n, width), F32)

    widths = [(SSD_CONV - 1) * SSD_CONV_CH, SSD_D_INNER, SSD_D_INNER, SSD_D_INNER, GROUP_W, GROUP_W,
              RET_QK, RET_QK, RET_QK]
    return pl.pallas_call(
        _ab_sample_prep_body,
        grid=(1,),
        in_specs=[colspec(SSD_CONV_CH, 2), colspec(RET_QK, 9), colspec(RET_QK, 10), full(p_dt)]
        + [full(a) for a in small],
        out_specs=[pl.BlockSpec((n, wd), lambda i: (0, 0)) for wd in widths],
        out_shape=[out(wd) for wd in widths],
        compiler_params=_cp(("arbitrary",), 48),
        name="ab_sample_prep",
    )(p_main, p_main, p_main, p_dt, *small)


def _ssd_state_body(eda_ref, xdt_ref, bm_ref, cm_ref, h_ref, ho_ref, y_ref):
    eda_t = _pad_t(eda_ref[...])
    xdt_t = _pad_t(xdt_ref[...])
    cm = _bf(jnp.concatenate([cm_ref[...], jnp.zeros((SB, SSD_D_STATE), F32)], axis=0))
    for b in range(SB):
        hn = h_ref[b] * eda_t[:, b:b + 1] + xdt_t[:, b:b + 1] * bm_ref[b:b + 1, :]
        ho_ref[b] = hn
        y_ref[b:b + 1, :] = _dot_nt(cm, _bf(hn))[b:b + 1, :]


def _ssd_state(eda, xdt, bm, cm, state):
    vec = pl.BlockSpec((SB, GROUP_W), lambda i, g: (i, g))
    bc = pl.BlockSpec((SB, SSD_D_STATE), lambda i, g: (i, g))
    st = pl.BlockSpec((SB, GROUP_W, SSD_D_STATE), lambda i, g: (i, g, 0))
    return pl.pallas_call(
        _ssd_state_body,
        grid=(DEC_BATCH // SB, SSD_GROUPS),
        in_specs=[vec, vec, bc, bc, st],
        out_specs=[st, vec],
        out_shape=[jax.ShapeDtypeStruct(state.shape, F32), jax.ShapeDtypeStruct((DEC_BATCH, SSD_D_INNER), F32)],
        compiler_params=_cp(("parallel", "arbitrary"), 48),
        name="ssd_state",
    )(eda, xdt, bm, cm, state)


def _outer_state_body(d_ref, k_ref, q_ref, v_ref, s_ref, so_ref, o_ref):
    d_t = _pad_t(d_ref[...])
    k_t = _pad_t(k_ref[...])
    q_t = _pad_t(q_ref[...])
    for b in range(SB):
        sn = s_ref[b] * d_t[:, b:b + 1] + k_t[:, b:b + 1] * v_ref[b:b + 1, :]
        so_ref[b] = sn
        o_ref[b:b + 1, :] = jnp.sum(sn * q_t[:, b:b + 1], axis=0, keepdims=True)


def _outer_state(d, k, q, v, state):
    _, nh, dk, dv = state.shape
    kv = pl.BlockSpec((SB, dk), lambda i, h: (i, h))
    vv = pl.BlockSpec((SB, dv), lambda i, h: (i, h))
    st = pl.BlockSpec((SB, None, dk, dv), lambda i, h: (i, h, 0, 0))
    return pl.pallas_call(
        _outer_state_body,
        grid=(DEC_BATCH // SB, nh),
        in_specs=[kv, kv, kv, vv, st],
        out_specs=[st, vv],
        out_shape=[jax.ShapeDtypeStruct(state.shape, F32), jax.ShapeDtypeStruct((DEC_BATCH, nh * dv), F32)],
        compiler_params=_cp(("parallel", "arbitrary"), 48),
        name="outer_state",
    )(d, k, q, v, state)


def _ab_sample_post_body(y_ref, xs_ref, z_ref, r_ref, rg_ref, dsk_ref, nrm_ref, o_ref):
    for g in range(SSD_GROUPS):
        gs = slice(g * GROUP_W, (g + 1) * GROUP_W)
        yg = (y_ref[:, gs] + xs_ref[:, gs] * dsk_ref[:, gs]) * _silu(z_ref[:, gs])
        o_ref[:, gs] = _bf(_rms(yg) * nrm_ref[:, gs])
    for h in range(RET_HEADS):
        vs = slice(h * RET_DV, (h + 1) * RET_DV)
        os = slice(SSD_D_INNER + h * RET_DV, SSD_D_INNER + (h + 1) * RET_DV)
        o_ref[:, os] = _bf(_silu(rg_ref[:, vs]) * _rms(r_ref[:, vs]))


def _ab_sample_post(y, xs, p_main, r, d_skip_e, ssd_norm):
    n = DEC_BATCH

    def full(a):
        return pl.BlockSpec(a.shape, lambda i: (0,) * a.ndim)

    return pl.pallas_call(
        _ab_sample_post_body,
        grid=(1,),
        in_specs=[full(y), full(xs), pl.BlockSpec((n, SSD_D_INNER), lambda i: (0, 0)), full(r),
                  pl.BlockSpec((n, RET_V), lambda i: (0, 2)), full(d_skip_e), full(ssd_norm)],
        out_specs=pl.BlockSpec((n, SSD_D_INNER + RET_V), lambda i: (0, 0)),
        out_shape=jax.ShapeDtypeStruct((n, SSD_D_INNER + RET_V), BF16),
        compiler_params=_cp(("arbitrary",), 48),
        name="ab_sample_post",
    )(y, xs, p_main, r, p_main, d_skip_e, ssd_norm)


def _c_sample_prep_body(q_ref, k_ref, gt_ref, n_ref, m_ref, ib_ref, fb_ref,
                        dpe_ref, kw_ref, nn_ref, mn_ref, dn_ref):
    ipre = gt_ref[:, :LANES] + ib_ref[...]
    lf = -_softplus(-(gt_ref[:, LANES:] + fb_ref[...]))
    m_prev = m_ref[...]
    mt = jnp.maximum(lf + m_prev, ipre)
    wgt = jnp.exp(ipre - mt)
    dp = jnp.exp(lf + m_prev - mt)
    emt = jnp.exp(-mt)
    mn_ref[...] = mt
    for h in range(M_HEADS):
        ks = slice(h * M_DK, (h + 1) * M_DK)
        dpe = _lane_bcast(dp, h)
        kw = k_ref[:, ks] * (M_DK ** -0.5) * _lane_bcast(wgt, h)
        nn = n_ref[:, ks] * dpe + kw
        den = jnp.sum(nn * q_ref[:, ks], axis=1, keepdims=True)
        dpe_ref[:, ks] = dpe
        kw_ref[:, ks] = kw
        nn_ref[:, ks] = nn
        dn_ref[:, h * M_DV:(h + 1) * M_DV] = jnp.broadcast_to(
            jnp.maximum(jnp.abs(den), emt[:, h:h + 1]), (DEC_BATCH, M_DV))


def _c_sample_prep(p_main, p_gate, n_state, m_state, i_bias, f_bias):
    n = DEC_BATCH

    def full(a):
        return pl.BlockSpec(a.shape, lambda i: (0,) * a.ndim)

    widths = [M_QK, M_QK, M_QK, LANES, M_V]
    return pl.pallas_call(
        _c_sample_prep_body,
        grid=(1,),
        in_specs=[pl.BlockSpec((n, M_QK), lambda i: (0, 0)), pl.BlockSpec((n, M_QK), lambda i: (0, 1)),
                  full(p_gate), full(n_state), full(m_state), full(i_bias), full(f_bias)],
        out_specs=[pl.BlockSpec((n, wd), lambda i: (0, 0)) for wd in widths],
        out_shape=[jax.ShapeDtypeStruct((n, wd), F32) for wd in widths],
        compiler_params=_cp(("arbitrary",), 48),
        name="c_sample_prep",
    )(p_main, p_main, p_gate, n_state, m_state, i_bias, f_bias)


def _c_sample_post_body(num_ref, dn_ref, o_ref, nrm_ref, h_ref):
    for h in range(M_HEADS):
        vs = slice(h * M_DV, (h + 1) * M_DV)
        hc = num_ref[:, vs] / dn_ref[:, vs]
        h_ref[:, vs] = _bf(jax.nn.sigmoid(o_ref[:, vs]) * (_rms(hc) * nrm_ref[:, vs]))


def _c_sample_post(num, den, p_main, norm_g):
    n = DEC_BATCH

    def full(a):
        return pl.BlockSpec(a.shape, lambda i: (0,) * a.ndim)

    return pl.pallas_call(
        _c_sample_post_body,
        grid=(1,),
        in_specs=[full(num), full(den), pl.BlockSpec((n, M_V), lambda i: (0, 2)), full(norm_g)],
        out_specs=pl.BlockSpec((n, M_V), lambda i: (0, 0)),
        out_shape=jax.ShapeDtypeStruct((n, M_V), BF16),
        compiler_params=_cp(("arbitrary",), 48),
        name="c_sample_post",
    )(num, den, p_main, norm_g)


def _rope_tables(pos):
    half = RET_DK // 2
    inv = jnp.exp(-math.log(ROPE_BASE) * jnp.arange(half, dtype=F32) / half)
    ang = pos.astype(F32)[:, None] * inv
    return jnp.cos(ang), jnp.sin(ang)


def _pad_lanes(v, width=LANES):
    return jnp.pad(v.reshape(1, -1), ((0, 0), (0, width - v.size)))


def kernel(x_prompt, x_sample, cache_mem_k, cache_mem_v, state_conv, state_ssm, state_ret, state_mlstm_c, state_mlstm_n, state_mlstm_m, mem_prompt, norm_ffn1, w_ffn1_in, w_ffn1_out, norm_mix, w_in_ab, ssd_conv_w, ssd_conv_b, ssd_dt_bias, ssd_a_log, ssd_d, ssd_norm, w_out_ab, w_in_c, mlstm_i_bias, mlstm_f_bias, mlstm_norm, w_out_c, norm_xattn, norm_mem, w_xq, w_xkv, w_xo, norm_ffn2, w_ffn2_in, w_ffn2_out, norm_final):
    g3 = lambda g: g.reshape(DEPTH, 1, D_MODEL)
    n_ffn1, n_mix, n_x, n_mem, n_ffn2 = g3(norm_ffn1), g3(norm_mix), g3(norm_xattn), g3(norm_mem), g3(norm_ffn2)

    f1_in, f1_out, f2_in, f2_out = (w.astype(BF16) for w in (w_ffn1_in, w_ffn1_out, w_ffn2_in, w_ffn2_out))
    w_out_ab, w_out_c, w_xq, w_xkv, w_xo = (w.astype(BF16) for w in (w_out_ab, w_out_c, w_xq, w_xkv, w_xo))

    wz, wxbc, wdt, wrq, wrk, wrv, wrg = jnp.split(w_in_ab[0], np_cumsum(AB_SIZES), axis=1)
    w_ab_main = jnp.concatenate([wz, wrv, wrg, wxbc, wrq, wrk], axis=1).astype(BF16)[None]
    w_ab_dt = jnp.pad(wdt, ((0, 0), (0, LANES - SSD_HEADS))).astype(BF16)[None]
    wq, wk, wv, wi, wf, wo = jnp.split(w_in_c[0], np_cumsum(C_SIZES), axis=1)
    w_c_main = jnp.concatenate([wq, wk, wv, wo], axis=1).astype(BF16)[None]
    gpad = ((0, 0), (0, LANES - M_HEADS))
    w_c_gate = jnp.concatenate([jnp.pad(wi, gpad), jnp.pad(wf, gpad)], axis=1).astype(BF16)[None]

    conv_w = ssd_conv_w[0]
    conv_b = ssd_conv_b.reshape(1, SSD_CONV_CH)
    dt_bias = _pad_lanes(ssd_dt_bias[0])
    a_log = _pad_lanes(ssd_a_log[0])
    d_skip_e = jnp.repeat(ssd_d[0], SSD_HEAD_DIM).reshape(1, SSD_D_INNER)
    s_norm = ssd_norm.reshape(1, SSD_D_INNER)
    i_bias = _pad_lanes(mlstm_i_bias[0])
    f_bias = _pad_lanes(mlstm_f_bias[0])
    m_norm = mlstm_norm.reshape(1, M_V)

    memk, memv = _mem_kv(mem_prompt.reshape(BATCH * MEM_LEN, D_MODEL), n_mem, w_xkv)
    cos_p, sin_p = _rope_tables(jnp.arange(SEQ))
    x = x_prompt.reshape(BATCH * SEQ, D_MODEL)
    x = _ffn(x, n_ffn1, f1_in, f1_out, 0)
    w_gates = w_ab_main[0, :, :AB_GATE_BLOCKS * AB_TN]
    w_xqk = w_ab_main[0, :, AB_GATE_BLOCKS * AB_TN:]
    p_gates, p_xqk, p_dt = _ab_inproj(x, n_mix, w_gates, w_xqk, w_ab_dt[0], dt_bias, conv_w, conv_b, cos_p, sin_p)
    ycat, ssm_p, ret_p = _mixer_ab_prompt(p_gates, p_xqk, p_dt, a_log, d_skip_e, s_norm)
    x_tail = x.reshape(BATCH, SEQ, D_MODEL)[:, SEQ - SSD_CONV:].reshape(BATCH * SSD_CONV, D_MODEL)
    conv_p = _norm_proj(x_tail, n_mix, 0, w_ab_main, 0, AB_TN, col0=AB_XBC0, n_out=SSD_CONV_CH)
    conv_p = conv_p.reshape(BATCH, SSD_CONV, SSD_CONV_CH)[:, 1:]
    x = _proj_residual(x, ycat, w_out_ab, 0)
    x = _xattn_prompt(x, n_x, w_xq, w_xo, memk, memv, 0)
    x = _ffn(x, n_ffn2, f2_in, f2_out, 0)
    x = _ffn(x, n_ffn1, f1_in, f1_out, 1)
    pc_main, pc_gate = _norm_proj(x, n_mix, 1, w_c_main, 0, 1024, w_c_gate)
    hout, mc_p, mn_p, mm_p = _mixer_c_prompt(pc_main.reshape(BATCH, SEQ, -1), pc_gate.reshape(BATCH, SEQ, -1),
                                             i_bias, f_bias, m_norm)
    x = _proj_residual(x, hout.reshape(BATCH * SEQ, M_V), w_out_c, 0)
    x = _xattn_prompt(x, n_x, w_xq, w_xo, memk, memv, 1)
    y_prompt = _ffn(x, n_ffn2, f2_in, f2_out, 1, norm_final).reshape(BATCH, SEQ, D_MODEL)

    cos_s, sin_s = _rope_tables(PAST_LEN + jnp.arange(1))

    def xattn_s(xs_, layer):
        q = _norm_proj(xs_, n_x, layer, w_xq, layer, 1024).reshape(DEC_BATCH, X_HEADS, X_HEAD_DIM)
        o = _xattn_sample(q, cache_mem_k, cache_mem_v, layer).reshape(DEC_BATCH, D_MODEL)
        return _proj_residual(xs_, o, w_xo, layer)

    xs_ = x_sample.reshape(DEC_BATCH, D_MODEL)
    xs_ = _ffn(xs_, n_ffn1, f1_in, f1_out, 0)
    sp_main, sp_dt = _norm_proj(xs_, n_mix, 0, w_ab_main, 0, 1024, w_ab_dt)
    conv_s, xs_c, xdt, eda, bm_s, cm_s, q_s, k_s, gam = _ab_sample_prep(
        sp_main, sp_dt, state_conv.reshape(DEC_BATCH, (SSD_CONV - 1) * SSD_CONV_CH), cos_s, sin_s,
        conv_w, conv_b, dt_bias, a_log)
    ssm_s, y_s = _ssd_state(eda, xdt, bm_s, cm_s, state_ssm.reshape(DEC_BATCH, SSD_D_INNER, SSD_D_STATE))
    ret_s, r_s = _outer_state(gam, k_s, q_s, sp_main[:, SSD_D_INNER:SSD_D_INNER + RET_V], state_ret[0])
    ycat_s = _ab_sample_post(y_s, xs_c, sp_main, r_s, d_skip_e, s_norm)
    xs_ = _proj_residual(xs_, ycat_s, w_out_ab, 0)
    xs_ = xattn_s(xs_, 0)
    xs_ = _ffn(xs_, n_ffn2, f2_in, f2_out, 0)
    xs_ = _ffn(xs_, n_ffn1, f1_in, f1_out, 1)
    sc_main, sc_gate = _norm_proj(xs_, n_mix, 1, w_c_main, 0, 1024, w_c_gate)
    m_in = jnp.pad(state_mlstm_m[0], ((0, 0), (0, LANES - M_HEADS)))
    dpe, kw, mn_s, mm_s, den = _c_sample_prep(sc_main, sc_gate, state_mlstm_n.reshape(DEC_BATCH, M_QK), m_in,
                                              i_bias, f_bias)
    mc_s, num = _outer_state(dpe, kw, sc_main[:, :M_QK], sc_main[:, 2 * M_QK:2 * M_QK + M_V], state_mlstm_c[0])
    hout_s = _c_sample_post(num, den, sc_main, m_norm)
    xs_ = _proj_residual(xs_, hout_s, w_out_c, 0)
    xs_ = xattn_s(xs_, 1)
    y_sample = _ffn(xs_, n_ffn2, f2_in, f2_out, 1, norm_final).reshape(DEC_BATCH, 1, D_MODEL)

    kv_shape = (DEPTH, BATCH, MEM_LEN, X_HEADS, X_HEAD_DIM)
    return (y_prompt, y_sample, memk.reshape(kv_shape), memv.reshape(kv_shape),
            conv_p.reshape(1, BATCH, SSD_CONV - 1, SSD_CONV_CH),
            conv_s.reshape(1, DEC_BATCH, SSD_CONV - 1, SSD_CONV_CH),
            ssm_p.reshape(1, BATCH, SSD_HEADS, SSD_HEAD_DIM, SSD_D_STATE),
            ssm_s.reshape(1, DEC_BATCH, SSD_HEADS, SSD_HEAD_DIM, SSD_D_STATE),
            ret_p[None], ret_s[None], mc_p[None], mc_s[None],
            mn_p[None], mn_s.reshape(1, DEC_BATCH, M_HEADS, M_DK),
            mm_p[:, :M_HEADS, 0][None], mm_s[:, :M_HEADS][None])


def np_cumsum(sizes):
    out, acc = [], 0
    for s in sizes[:-1]:
        acc += s
        out.append(acc)
    return out
```

```python
import functools
import math

import jax
import jax.numpy as jnp
from jax import lax
from jax.experimental import pallas as pl
from jax.experimental.pallas import tpu as pltpu

F32 = jnp.float32
BF16 = jnp.bfloat16
EPS = 1e-6

D_MODEL = 1024
BATCH = 8
SEQ = 2048
DEPTH = 2
DEC_BATCH = 128
PAST_LEN = 16384
CHUNK = 128
D_FF = 2816
SSD_D_INNER = 2 * D_MODEL
SSD_HEAD_DIM = 64
SSD_HEADS = SSD_D_INNER // SSD_HEAD_DIM
SSD_GROUPS = 4
SSD_D_STATE = 128
SSD_CONV = 4
SSD_CONV_CH = SSD_D_INNER + 2 * SSD_GROUPS * SSD_D_STATE
RET_HEADS = 4
RET_QK = D_MODEL
RET_V = 2 * D_MODEL
RET_DK = RET_QK // RET_HEADS
RET_DV = RET_V // RET_HEADS
ROPE_BASE = 10000.0
AB_SIZES = (SSD_D_INNER, SSD_CONV_CH, SSD_HEADS, RET_QK, RET_QK, RET_V, RET_V)
M_HEADS = 4
M_QK = D_MODEL // 2
M_V = D_MODEL
M_DK = M_QK // M_HEADS
M_DV = M_V // M_HEADS
C_SIZES = (M_QK, M_QK, M_V, M_HEADS, M_HEADS, M_V)
MEM_LEN = 256
X_HEADS = 4
X_HEAD_DIM = D_MODEL // X_HEADS

LANES = 128
SUBLANES = 8
GROUP_W = SSD_D_INNER // SSD_GROUPS
HEADS_PER_GROUP = SSD_HEADS // SSD_GROUPS
ROW_TILE = 1024
FF_TILE = 256
SB = 8
XB = 4
CB = 2


def _cp(sem, mib):
    return pltpu.CompilerParams(dimension_semantics=sem, vmem_limit_bytes=mib * 1024 * 1024)


def _bf(x):
    return x.astype(BF16)


def _dot(a, b):
    return jnp.dot(a, b, preferred_element_type=F32)


def _dot_nt(a, b):
    return lax.dot_general(a, b, (((1,), (1,)), ((), ())), preferred_element_type=F32)


def _dot_tn(a, b):
    return lax.dot_general(a, b, (((0,), (0,)), ((), ())), preferred_element_type=F32)


def _rms(x):
    return x * lax.rsqrt(jnp.mean(x * x, axis=-1, keepdims=True) + EPS)


def _silu(x):
    return x * jax.nn.sigmoid(x)


def _softplus(x):
    return jnp.maximum(x, 0.0) + jnp.log1p(jnp.exp(-jnp.abs(x)))


def _split3(x):
    hi = x.astype(BF16)
    r = x - hi.astype(F32)
    mid = r.astype(BF16)
    lo = (r - mid.astype(F32)).astype(BF16)
    return hi, mid, lo


def _cumsum_rows(x):
    n = x.shape[0]
    r = lax.broadcasted_iota(jnp.int32, (n, n), 0)
    c = lax.broadcasted_iota(jnp.int32, (n, n), 1)
    t = jnp.where(r >= c, 1.0, 0.0).astype(BF16)
    hi, mid, lo = _split3(x)
    return _dot(t, hi) + _dot(t, mid) + _dot(t, lo)


def _lane_bcast(x, h, width=LANES):
    return jnp.broadcast_to(x[:, h:h + 1], (x.shape[0], width))


def _pair_expand(x, n_heads):
    rows = x.shape[0]
    lo = lax.broadcasted_iota(jnp.int32, (rows, LANES), 1) < SSD_HEAD_DIM
    return jnp.concatenate(
        [jnp.where(lo, _lane_bcast(x, 2 * j), _lane_bcast(x, 2 * j + 1)) for j in range(n_heads // 2)], axis=1)


def _pad_t(x):
    pad = jnp.zeros((LANES - x.shape[0], x.shape[1]), F32)
    return jnp.concatenate([x, pad], axis=0).T


def _ffn_body(*refs, nf, final):
    if final:
        x_ref, g_ref, wi_ref, wo_ref, fg_ref, o_ref = refs
    else:
        x_ref, g_ref, wi_ref, wo_ref, o_ref = refs
    x = x_ref[...]
    xn = _bf(_rms(x) * g_ref[...])
    acc = None
    for f in range(nf):
        fs = slice(f * FF_TILE, (f + 1) * FF_TILE)
        g = _dot(xn, wi_ref[:, fs])
        u = _dot(xn, wi_ref[:, D_FF + f * FF_TILE:D_FF + (f + 1) * FF_TILE])
        t = _dot(_bf(_silu(g) * u), wo_ref[fs, :])
        acc = t if acc is None else acc + t
    y = x + 0.5 * acc
    if final:
        y = _rms(y) * fg_ref[...]
    o_ref[...] = y


def _ffn(x, g3, w_in, w_out, layer, final_g=None):
    m = x.shape[0]
    tm = min(m, ROW_TILE)
    nf = D_FF // FF_TILE
    once = pl.Buffered(1)
    in_specs = [
        pl.BlockSpec((tm, D_MODEL), lambda i: (i, 0)),
        pl.BlockSpec((None, 1, D_MODEL), lambda i: (layer, 0, 0)),
        pl.BlockSpec((None, D_MODEL, 2 * D_FF), lambda i: (layer, 0, 0), pipeline_mode=once),
        pl.BlockSpec((None, D_FF, D_MODEL), lambda i: (layer, 0, 0), pipeline_mode=once),
    ]
    args = [x, g3, w_in, w_out]
    if final_g is not None:
        in_specs.append(pl.BlockSpec((1, D_MODEL), lambda i: (0, 0)))
        args.append(final_g.reshape(1, D_MODEL))
    return pl.pallas_call(
        functools.partial(_ffn_body, nf=nf, final=final_g is not None),
        grid=(m // tm,),
        in_specs=in_specs,
        out_specs=pl.BlockSpec((tm, D_MODEL), lambda i: (i, 0)),
        out_shape=jax.ShapeDtypeStruct((m, D_MODEL), F32),
        compiler_params=_cp(("parallel",), 56),
        name="ffn",
    )(*args)


def _norm_proj_body(*refs, small):
    if small:
        x_ref, g_ref, w_ref, ws_ref, o_ref, os_ref, xn_ref = refs
    else:
        x_ref, g_ref, w_ref, o_ref, xn_ref = refs
    n = pl.program_id(1)

    @pl.when(n == 0)
    def _():
        xn = _bf(_rms(x_ref[...]) * g_ref[...])
        xn_ref[...] = xn
        if small:
            os_ref[...] = _dot(xn, _bf(ws_ref[...]))

    o_ref[...] = _dot(xn_ref[...], _bf(w_ref[...]))


def _norm_proj(x, g3, glayer, w3, wlayer, tn, w_small=None, col0=0, n_out=None):
    m = x.shape[0]
    tm = min(m, ROW_TILE)
    n_out = w3.shape[-1] if n_out is None else n_out
    in_specs = [
        pl.BlockSpec((tm, D_MODEL), lambda i, n: (i, 0)),
        pl.BlockSpec((None, 1, D_MODEL), lambda i, n: (glayer, 0, 0)),
        pl.BlockSpec((None, D_MODEL, tn), lambda i, n: (wlayer, 0, n + col0)),
    ]
    args = [x, g3, w3]
    out_specs = [pl.BlockSpec((tm, tn), lambda i, n: (i, n))]
    out_shape = [jax.ShapeDtypeStruct((m, n_out), F32)]
    if w_small is not None:
        ns = w_small.shape[-1]
        in_specs.append(pl.BlockSpec((None, D_MODEL, ns), lambda i, n: (0, 0, 0)))
        args.append(w_small)
        out_specs.append(pl.BlockSpec((tm, ns), lambda i, n: (i, 0)))
        out_shape.append(jax.ShapeDtypeStruct((m, ns), F32))
    res = pl.pallas_call(
        functools.partial(_norm_proj_body, small=w_small is not None),
        grid=(m // tm, n_out // tn),
        in_specs=in_specs,
        out_specs=out_specs,
        out_shape=out_shape,
        scratch_shapes=[pltpu.VMEM((tm, D_MODEL), BF16)],
        compiler_params=_cp(("parallel", "arbitrary"), 48),
        name="norm_proj",
    )(*args)
    return res if w_small is not None else res[0]


AB_TN = 1024
AB_XBC0 = (SSD_D_INNER + 2 * RET_V) // AB_TN
AB_RQ = AB_XBC0 + SSD_CONV_CH // AB_TN


AB_GATE_BLOCKS = AB_XBC0
AB_XQK_BLOCKS = SSD_CONV_CH // AB_TN + 2 * RET_QK // AB_TN
AB_ROW_TILE = 512


def _ab_gates_body(x_ref, g_ref, w_ref, o_ref):
    xn = _bf(_rms(x_ref[...]) * g_ref[...])
    nz = SSD_D_INNER // AB_TN
    nv = RET_V // AB_TN
    for n in range(AB_GATE_BLOCKS):
        cs = slice(n * AB_TN, (n + 1) * AB_TN)
        r = _dot(xn, w_ref[:, cs])
        o_ref[:, cs] = _bf(r) if nz <= n < nz + nv else _bf(_silu(r))


def _ab_xqk_body(x_ref, g_ref, w_ref, ws_ref, dtb_ref, cw_ref, cb_ref, cos_ref, sin_ref,
                 o_ref, dt_ref, xpad, rbuf, ybuf, carry, *, tiles_per_seq):
    i = pl.program_id(0)
    tm = x_ref.shape[0]
    nslab = tm // SUBLANES
    pitch = nslab + SUBLANES
    nxb = SSD_CONV_CH // AB_TN
    ntap = SSD_CONV - 1

    @pl.when(i == 0)
    def _():
        carry[...] = jnp.zeros_like(carry)

    g = g_ref[...]
    xn = _bf(_rms(x_ref[...]) * g)
    dt_ref[...] = _softplus(_dot(xn, ws_ref[...]) + dtb_ref[...])
    for lb in range(D_MODEL // LANES):
        for s in range(SUBLANES):
            xpad[lb, s * pitch:s * pitch + nslab, :] = x_ref[s * nslab:(s + 1) * nslab, lb * LANES:(lb + 1) * LANES]
    xp = jnp.concatenate(
        [jnp.concatenate([xpad[lb, pl.ds(v, SUBLANES, stride=pitch), :] for lb in range(D_MODEL // LANES)], axis=1)
         for v in range(nslab)], axis=0)
    xnp = _bf(_rms(xp) * g)
    seq_start = i % tiles_per_seq == 0
    first = lax.broadcasted_iota(jnp.int32, (SUBLANES, AB_TN), 0) == 0
    halo = ntap * SUBLANES
    for n in range(nxb):
        cs = slice(n * AB_TN, (n + 1) * AB_TN)
        rbuf[n, halo:halo + tm, :] = _dot(xnp, w_ref[:, cs])
        prev_rows = jnp.where(seq_start, 0.0, carry[n])
        for k in range(ntap):
            hi = rbuf[n, tm + k * SUBLANES:tm + (k + 1) * SUBLANES, :]
            rbuf[n, k * SUBLANES:(k + 1) * SUBLANES, :] = jnp.where(first, prev_rows[k:k + 1, :],
                                                                     pltpu.roll(hi, 1, axis=0))
            carry[n, k:k + 1, :] = hi[SUBLANES - 1:SUBLANES, :]
        w = cw_ref[:, cs]
        conv = cb_ref[:, cs] + rbuf[n, halo:halo + tm, :] * w[ntap:ntap + 1, :]
        for j in range(1, SSD_CONV):
            off = halo - j * SUBLANES
            conv = conv + rbuf[n, off:off + tm, :] * w[ntap - j:ntap - j + 1, :]
        y = _silu(conv)
        for lb in range(AB_TN // LANES):
            ls = slice(lb * LANES, (lb + 1) * LANES)
            for v in range(nslab):
                ybuf[n, lb, pl.ds(v, SUBLANES, stride=pitch), :] = y[v * SUBLANES:(v + 1) * SUBLANES, ls]
            for s in range(SUBLANES):
                o_ref[s * nslab:(s + 1) * nslab, n * AB_TN + ls.start:n * AB_TN + ls.stop] = _bf(
                    ybuf[n, lb, s * pitch:s * pitch + nslab, :])
    cos = cos_ref[...]
    sin = sin_ref[...]
    for n in range(nxb, AB_XQK_BLOCKS):
        r = _dot(xn, w_ref[:, n * AB_TN:(n + 1) * AB_TN])
        scale = 1.0 if n == nxb else RET_DK ** -0.5
        for h in range(AB_TN // RET_DK):
            a = slice(h * RET_DK, h * RET_DK + LANES)
            b = slice(h * RET_DK + LANES, (h + 1) * RET_DK)
            x1, x2 = r[:, a], r[:, b]
            o_ref[:, n * AB_TN + a.start:n * AB_TN + a.stop] = _bf((x1 * cos - x2 * sin) * scale)
            o_ref[:, n * AB_TN + b.start:n * AB_TN + b.stop] = _bf((x1 * sin + x2 * cos) * scale)


def _ab_inproj(x, g3, w_gates, w_xqk, w_dt, dt_bias, conv_w, conv_b, cos, sin):
    m = x.shape[0]
    tm = AB_ROW_TILE
    tps = SEQ // tm
    nxb = SSD_CONV_CH // AB_TN
    once = pl.Buffered(1)

    def full(a):
        return pl.BlockSpec(a.shape, lambda i: (0,) * a.ndim, pipeline_mode=once)

    xspec = pl.BlockSpec((tm, D_MODEL), lambda i: (i, 0))
    gspec = pl.BlockSpec((None, 1, D_MODEL), lambda i: (0, 0, 0))
    gates = pl.pallas_call(
        _ab_gates_body,
        grid=(m // tm,),
        in_specs=[xspec, gspec, full(w_gates)],
        out_specs=pl.BlockSpec((tm, AB_GATE_BLOCKS * AB_TN), lambda i: (i, 0)),
        out_shape=jax.ShapeDtypeStruct((m, AB_GATE_BLOCKS * AB_TN), BF16),
        compiler_params=_cp(("parallel",), 56),
        name="ab_gates",
    )(x, g3, w_gates)
    xqk, dt = pl.pallas_call(
        functools.partial(_ab_xqk_body, tiles_per_seq=tps),
        grid=(m // tm,),
        in_specs=[xspec, gspec, full(w_xqk), full(w_dt), full(dt_bias), full(conv_w), full(conv_b),
                  pl.BlockSpec((tm, LANES), lambda i: (i % tps, 0)),
                  pl.BlockSpec((tm, LANES), lambda i: (i % tps, 0))],
        out_specs=[pl.BlockSpec((tm, AB_XQK_BLOCKS * AB_TN), lambda i: (i, 0)),
                   pl.BlockSpec((tm, LANES), lambda i: (i, 0))],
        out_shape=[jax.ShapeDtypeStruct((m, AB_XQK_BLOCKS * AB_TN), BF16), jax.ShapeDtypeStruct((m, LANES), F32)],
        scratch_shapes=[pltpu.VMEM((D_MODEL // LANES, tm + SUBLANES * SUBLANES, LANES), F32),
                        pltpu.VMEM((nxb, (SSD_CONV - 1) * SUBLANES + tm, AB_TN), F32),
                        pltpu.VMEM((nxb, AB_TN // LANES, tm + SUBLANES * SUBLANES, LANES), F32),
                        pltpu.VMEM((nxb, SUBLANES, AB_TN), F32)],
        compiler_params=_cp(("arbitrary",), 56),
        name="ab_xqk",
    )(x, g3, w_xqk, w_dt, dt_bias, conv_w, conv_b, cos, sin)
    return gates, xqk, dt


def _c_inproj_body(x_ref, g_ref, w_ref, wg_ref, o_ref, gt_ref):
    xn = _bf(_rms(x_ref[...]) * g_ref[...])
    gt_ref[...] = _dot(xn, wg_ref[...])
    o_ref[:, :M_QK] = _bf(_dot(xn, w_ref[:, :M_QK]))
    o_ref[:, M_QK:2 * M_QK] = _bf(_dot(xn, w_ref[:, M_QK:2 * M_QK]) * (M_DK ** -0.5))
    o_ref[:, 2 * M_QK:2 * M_QK + M_V] = _bf(_dot(xn, w_ref[:, 2 * M_QK:2 * M_QK + M_V]))
    o_ref[:, 2 * M_QK + M_V:] = _bf(jax.nn.sigmoid(_dot(xn, w_ref[:, 2 * M_QK + M_V:])))


def _c_inproj(x, g3, glayer, w_main, w_gate):
    m = x.shape[0]
    tm = ROW_TILE
    once = pl.Buffered(1)
    n_out = w_main.shape[-1]
    n_gate = w_gate.shape[-1]
    return pl.pallas_call(
        _c_inproj_body,
        grid=(m // tm,),
        in_specs=[
            pl.BlockSpec((tm, D_MODEL), lambda i: (i, 0)),
            pl.BlockSpec((None, 1, D_MODEL), lambda i: (glayer, 0, 0)),
            pl.BlockSpec((None, D_MODEL, n_out), lambda i: (0, 0, 0), pipeline_mode=once),
            pl.BlockSpec((None, D_MODEL, n_gate), lambda i: (0, 0, 0), pipeline_mode=once),
        ],
        out_specs=[pl.BlockSpec((tm, n_out), lambda i: (i, 0)), pl.BlockSpec((tm, n_gate), lambda i: (i, 0))],
        out_shape=[jax.ShapeDtypeStruct((m, n_out), BF16), jax.ShapeDtypeStruct((m, n_gate), F32)],
        compiler_params=_cp(("parallel",), 48),
        name="c_inproj",
    )(x, g3, w_main, w_gate)


def _proj_res_body(x_ref, y_ref, w_ref, o_ref):
    o_ref[...] = x_ref[...] + _dot(_bf(y_ref[...]), w_ref[...])


def _proj_residual(x, y, w3, layer):
    m = x.shape[0]
    tm = min(m, ROW_TILE)
    kdim = y.shape[1]
    return pl.pallas_call(
        _proj_res_body,
        grid=(m // tm,),
        in_specs=[
            pl.BlockSpec((tm, D_MODEL), lambda i: (i, 0)),
            pl.BlockSpec((tm, kdim), lambda i: (i, 0)),
            pl.BlockSpec((None, kdim, D_MODEL), lambda i: (layer, 0, 0), pipeline_mode=pl.Buffered(1)),
        ],
        out_specs=pl.BlockSpec((tm, D_MODEL), lambda i: (i, 0)),
        out_shape=jax.ShapeDtypeStruct((m, D_MODEL), F32),
        compiler_params=_cp(("parallel",), 56),
        name="proj_residual",
    )(x, y, w3)


def _mem_kv_body(x_ref, g_ref, w_ref, k_ref, v_ref):
    xn = _bf(_rms(x_ref[...]) * g_ref[...])
    kv = _dot(xn, _bf(w_ref[...]))
    k_ref[...] = kv[:, :D_MODEL]
    v_ref[...] = kv[:, D_MODEL:]


def _mem_kv(mem2d, g3, w_xkv):
    m = mem2d.shape[0]
    tm = 512
    shp = jax.ShapeDtypeStruct((DEPTH, m, D_MODEL), F32)
    return pl.pallas_call(
        _mem_kv_body,
        grid=(DEPTH, m // tm),
        in_specs=[
            pl.BlockSpec((tm, D_MODEL), lambda l, i: (i, 0)),
            pl.BlockSpec((None, 1, D_MODEL), lambda l, i: (l, 0, 0)),
            pl.BlockSpec((None, D_MODEL, 2 * D_MODEL), lambda l, i: (l, 0, 0)),
        ],
        out_specs=[pl.BlockSpec((None, tm, D_MODEL), lambda l, i: (l, i, 0))] * 2,
        out_shape=[shp, shp],
        compiler_params=_cp(("arbitrary", "arbitrary"), 48),
        name="mem_kv",
    )(mem2d, g3, w_xkv)


def _xattn_prompt_body(x_ref, g_ref, wq_ref, wo_ref, k_ref, v_ref, o_ref):
    x = x_ref[...]
    xn = _bf(_rms(x) * g_ref[...])
    q = _dot(xn, _bf(wq_ref[...]))
    k = _bf(k_ref[...])
    v = _bf(v_ref[...])
    outs = []
    for h in range(X_HEADS):
        sl = slice(h * X_HEAD_DIM, (h + 1) * X_HEAD_DIM)
        s = _dot_nt(_bf(q[:, sl]), k[:, sl]) * (X_HEAD_DIM ** -0.5)
        e = jnp.exp(s - jnp.max(s, axis=-1, keepdims=True))
        p = e / jnp.sum(e, axis=-1, keepdims=True)
        outs.append(_bf(_dot(_bf(p), v[:, sl])))
    o_ref[...] = x + _dot(jnp.concatenate(outs, axis=1), _bf(wo_ref[...]))


def _xattn_prompt(x, g3, w_xq, w_xo, memk, memv, layer):
    tq = 512
    nq = SEQ // tq
    return pl.pallas_call(
        _xattn_prompt_body,
        grid=(BATCH, nq),
        in_specs=[
            pl.BlockSpec((tq, D_MODEL), lambda b, j: (b * nq + j, 0)),
            pl.BlockSpec((None, 1, D_MODEL), lambda b, j: (layer, 0, 0)),
            pl.BlockSpec((None, D_MODEL, D_MODEL), lambda b, j: (layer, 0, 0)),
            pl.BlockSpec((None, D_MODEL, D_MODEL), lambda b, j: (layer, 0, 0)),
            pl.BlockSpec((None, MEM_LEN, D_MODEL), lambda b, j: (layer, b, 0)),
            pl.BlockSpec((None, MEM_LEN, D_MODEL), lambda b, j: (layer, b, 0)),
        ],
        out_specs=pl.BlockSpec((tq, D_MODEL), lambda b, j: (b * nq + j, 0)),
        out_shape=jax.ShapeDtypeStruct((BATCH * SEQ, D_MODEL), F32),
        compiler_params=_cp(("parallel", "arbitrary"), 48),
        name="xattn_prompt",
    )(x, g3, w_xq, w_xo, memk, memv)


def _xattn_sample_body(q_ref, k_ref, v_ref, o_ref):
    for b in range(XB):
        s = jnp.sum(k_ref[b] * q_ref[b][None], axis=-1, keepdims=True) * (X_HEAD_DIM ** -0.5)
        e = jnp.exp(s - jnp.max(s, axis=0, keepdims=True))
        p = e / jnp.sum(e, axis=0, keepdims=True)
        o_ref[b] = jnp.sum(p * v_ref[b], axis=0)


def _xattn_sample(q, cache_k, cache_v, layer):
    blk = pl.BlockSpec((None, XB, MEM_LEN, X_HEADS, X_HEAD_DIM), lambda i: (layer, i, 0, 0, 0))
    qo = pl.BlockSpec((XB, X_HEADS, X_HEAD_DIM), lambda i: (i, 0, 0))
    return pl.pallas_call(
        _xattn_sample_body,
        grid=(DEC_BATCH // XB,),
        in_specs=[qo, blk, blk],
        out_specs=qo,
        out_shape=jax.ShapeDtypeStruct((DEC_BATCH, X_HEADS, X_HEAD_DIM), F32),
        compiler_params=_cp(("parallel",), 48),
        name="xattn_sample",
    )(q, cache_k, cache_v)


def _ret_log_gamma(h):
    return math.log1p(-(2.0 ** (-5.0 - h)))


def _ab_prompt_body(z_ref, rv_ref, rg_ref, xbc_ref, rq_ref, rk_ref, dt_ref, alog_ref, dsk_ref, nrm_ref,
                    y_ref, h_ref, s_ref):
    c = pl.program_id(1)

    @pl.when(c == 0)
    def _():
        h_ref[...] = jnp.zeros_like(h_ref)
        s_ref[...] = jnp.zeros_like(s_ref)

    xs = xbc_ref[:, :SSD_D_INNER].astype(F32)
    bm = xbc_ref[:, SSD_D_INNER:SSD_D_INNER + GROUP_W]
    cm = xbc_ref[:, SSD_D_INNER + GROUP_W:]

    dt = dt_ref[...]
    da = dt * (-jnp.exp(alog_ref[...]))
    cs = _cumsum_rows(da)
    cs_t = cs.T
    row = lax.broadcasted_iota(jnp.int32, (CHUNK, CHUNK), 0)
    col = lax.broadcasted_iota(jnp.int32, (CHUNK, CHUNK), 1)
    tri = row >= col
    lo = col < SSD_HEAD_DIM
    for g in range(SSD_GROUPS):
        gs = slice(g * GROUP_W, (g + 1) * GROUP_W)
        ns = slice(g * SSD_D_STATE, (g + 1) * SSD_D_STATE)
        cmg = cm[:, ns]
        bmg = bm[:, ns]
        att = _dot_nt(cmg, bmg)
        hprev = h_ref[gs, :]
        yint = _dot_nt(cmg, _bf(hprev))
        ys, wxs, css = [], [], []
        for j in range(HEADS_PER_GROUP // 2):
            h0 = g * HEADS_PER_GROUP + 2 * j
            cb0 = _lane_bcast(cs, h0)
            cb1 = _lane_bcast(cs, h0 + 1)
            cs_p = jnp.where(lo, cb0, cb1)
            dt_p = jnp.where(lo, _lane_bcast(dt, h0), _lane_bcast(dt, h0 + 1))
            off = g * GROUP_W + j * LANES
            xdt = xs[:, off:off + LANES] * dt_p
            d0 = jnp.exp(jnp.where(tri, cb0 - cs_t[h0:h0 + 1, :], -jnp.inf))
            d1 = jnp.exp(jnp.where(tri, cb1 - cs_t[h0 + 1:h0 + 2, :], -jnp.inf))
            yy = _dot(jnp.concatenate([_bf(att * d0), _bf(att * d1)], axis=0), _bf(xdt))
            ys.append(jnp.where(lo, yy[:CHUNK], yy[CHUNK:]) + yint[:, j * LANES:(j + 1) * LANES] * jnp.exp(cs_p))
            wxs.append(_bf(xdt * jnp.exp(cs_p[CHUNK - 1:CHUNK, :] - cs_p)))
            css.append(cs_p)
        cs_g = jnp.concatenate(css, axis=1)
        last_t = jnp.broadcast_to(cs_g[CHUNK - 1:CHUNK, :], (CHUNK, GROUP_W)).T
        h_ref[gs, :] = hprev * jnp.exp(last_t) + _dot_tn(jnp.concatenate(wxs, axis=1), bmg)
        yg = jnp.concatenate(ys, axis=1)
        yg = (yg + xs[:, gs] * dsk_ref[:, gs]) * z_ref[:, gs].astype(F32)
        y_ref[:, gs] = _bf(_rms(yg) * nrm_ref[:, gs])

    tcol = row.astype(F32)
    diff = tcol - col.astype(F32)
    for h in range(RET_HEADS):
        lg = _ret_log_gamma(h)
        ks = slice(h * RET_DK, (h + 1) * RET_DK)
        vs = slice(h * RET_DV, (h + 1) * RET_DV)
        qb = rq_ref[:, ks]
        kb = rk_ref[:, ks]
        decay = jnp.exp(jnp.where(tri, diff * lg, -jnp.inf))
        att = _dot_nt(qb, kb) * decay
        vb = rv_ref[:, vs]
        s_prev = s_ref[h]
        inner = jnp.exp((tcol + 1.0) * lg)
        r = _dot(_bf(att), vb) + _dot(qb, _bf(s_prev)) * jnp.concatenate([inner] * (RET_DV // LANES), axis=1)
        tail_w = jnp.exp((CHUNK - 1.0 - tcol) * lg)
        kt = _bf(kb.astype(F32) * jnp.concatenate([tail_w] * (RET_DK // LANES), axis=1))
        s_ref[h] = s_prev * math.exp(CHUNK * lg) + _dot_tn(kt, vb)
        os = slice(SSD_D_INNER + h * RET_DV, SSD_D_INNER + (h + 1) * RET_DV)
        y_ref[:, os] = _bf(rg_ref[:, vs].astype(F32) * _rms(r))


def _mixer_ab_prompt(p_gates, p_xqk, p_dt, a_log, d_skip_e, ssd_norm):
    nc = SEQ // CHUNK
    m = BATCH * SEQ

    def rowspec(width, cb):
        return pl.BlockSpec((CHUNK, width), lambda b, c: (b * nc + c, cb))

    def full(a):
        return pl.BlockSpec(a.shape, lambda b, c: (0,) * a.ndim)

    params = [a_log, d_skip_e, ssd_norm]
    return pl.pallas_call(
        _ab_prompt_body,
        grid=(BATCH, nc),
        in_specs=[rowspec(SSD_D_INNER, 0), rowspec(RET_V, 1), rowspec(RET_V, 2), rowspec(SSD_CONV_CH, 0),
                  rowspec(RET_QK, SSD_CONV_CH // RET_QK), rowspec(RET_QK, SSD_CONV_CH // RET_QK + 1),
                  rowspec(LANES, 0)] + [full(a) for a in params],
        out_specs=[
            pl.BlockSpec((CHUNK, SSD_D_INNER + RET_V), lambda b, c: (b * nc + c, 0)),
            pl.BlockSpec((None, SSD_D_INNER, SSD_D_STATE), lambda b, c: (b, 0, 0)),
            pl.BlockSpec((None, RET_HEADS, RET_DK, RET_DV), lambda b, c: (b, 0, 0, 0)),
        ],
        out_shape=[
            jax.ShapeDtypeStruct((m, SSD_D_INNER + RET_V), BF16),
            jax.ShapeDtypeStruct((BATCH, SSD_D_INNER, SSD_D_STATE), F32),
            jax.ShapeDtypeStruct((BATCH, RET_HEADS, RET_DK, RET_DV), F32),
        ],
        compiler_params=_cp(("parallel", "arbitrary"), 48),
        name="mixer_ab_prompt",
    )(p_gates, p_gates, p_gates, p_xqk, p_xqk, p_xqk, p_dt, *params)


def _c_prompt_body(q_ref, k_ref, v_ref, o_ref, gt_ref, ib_ref, fb_ref, nrm_ref,
                   h_ref, c_ref, n_ref, m_ref, m_s):
    @pl.when(pl.program_id(1) == 0)
    def _():
        c_ref[...] = jnp.zeros_like(c_ref)
        n_ref[...] = jnp.zeros_like(n_ref)
        m_s[...] = jnp.zeros_like(m_s)

    row = lax.broadcasted_iota(jnp.int32, (CHUNK, CHUNK), 0)
    col = lax.broadcasted_iota(jnp.int32, (CHUNK, CHUNK), 1)
    tri = row >= col
    seqs = range(CB)
    ipre = [gt_ref[s, :, :LANES] + ib_ref[...] for s in seqs]
    lf = [-_softplus(-(gt_ref[s, :, LANES:] + fb_ref[...])) for s in seqs]
    b = [_cumsum_rows(lf[s]) for s in seqs]
    g = [ipre[s] - b[s] for s in seqs]
    g_t = [g[s].T for s in seqs]
    b_t = [b[s].T for s in seqs]
    cmax = list(g_t)
    sh = 1
    while sh < CHUNK:
        cmax = [jnp.maximum(cmax[s], jnp.where(col >= sh, pltpu.roll(cmax[s], sh, axis=1), -jnp.inf)) for s in seqs]
        sh *= 2
    m_prev = [m_s[s] for s in seqs]
    mt_t = [b_t[s] + jnp.maximum(m_prev[s], cmax[s]) for s in seqs]
    mt = [mt_t[s].T for s in seqs]
    m_prev_c = [m_prev[s].T for s in seqs]
    inter = [jnp.exp(b[s] + m_prev_c[s] - mt[s]) for s in seqs]
    emt = [jnp.exp(-mt[s]) for s in seqs]
    wl = [jnp.exp(g[s] + b[s][CHUNK - 1:CHUNK, :] - mt[s][CHUNK - 1:CHUNK, :]) for s in seqs]
    bm = [b[s] - mt[s] for s in seqs]
    m_new = [_lane_bcast(mt_t[s], CHUNK - 1) for s in seqs]
    dp_t = [jnp.exp(_lane_bcast(b_t[s], CHUNK - 1) + m_prev[s] - m_new[s]) for s in seqs]
    for s in seqs:
        m_s[s] = m_new[s]
        m_ref[s] = m_new[s][0:SUBLANES, :]
    for h in range(M_HEADS):
        ks = slice(h * M_DK, (h + 1) * M_DK)
        vs = slice(h * M_DV, (h + 1) * M_DV)
        for s in seqs:
            wgt = jnp.exp(jnp.where(tri, g_t[s][h:h + 1, :] + bm[s][:, h:h + 1], -jnp.inf))
            qb = q_ref[s, :, ks]
            kb = k_ref[s, :, ks]
            vb = v_ref[s, :, vs]
            a = _dot_nt(qb, kb) * wgt
            c_prev = c_ref[s, h]
            n_prev = n_ref[s, h:h + 1, :]
            ic = inter[s][:, h:h + 1]
            num = _dot(_bf(a), vb) + _dot(qb, _bf(c_prev)) * ic
            den = (jnp.sum(a, axis=1, keepdims=True)
                   + jnp.sum(qb.astype(F32) * n_prev, axis=1, keepdims=True) * ic)
            hc = num / jnp.maximum(jnp.abs(den), emt[s][:, h:h + 1])
            kw = kb.astype(F32) * wl[s][:, h:h + 1]
            dp_row = dp_t[s][h:h + 1, :]
            c_ref[s, h] = c_prev * jnp.concatenate([dp_row] * (M_DV // LANES), axis=1) + _dot_tn(_bf(kw), vb)
            n_ref[s, h:h + 1, :] = n_prev * dp_row + jnp.sum(kw, axis=0, keepdims=True)
            h_ref[s, :, vs] = _bf(o_ref[s, :, vs].astype(F32) * (_rms(hc) * nrm_ref[:, vs]))


def _mixer_c_prompt(p_main, p_gate, i_bias, f_bias, norm_g):
    nc = SEQ // CHUNK

    def rowspec(width, cb):
        return pl.BlockSpec((CB, CHUNK, width), lambda b, c: (b, c, cb))

    def full(a):
        return pl.BlockSpec(a.shape, lambda b, c: (0,) * a.ndim)

    params = [i_bias, f_bias, norm_g]
    return pl.pallas_call(
        _c_prompt_body,
        grid=(BATCH // CB, nc),
        in_specs=[rowspec(M_QK, 0), rowspec(M_QK, 1), rowspec(M_V, 1), rowspec(M_V, 2), rowspec(2 * LANES, 0)]
        + [full(a) for a in params],
        out_specs=[
            pl.BlockSpec((CB, CHUNK, M_V), lambda b, c: (b, c, 0)),
            pl.BlockSpec((CB, M_HEADS, M_DK, M_DV), lambda b, c: (b, 0, 0, 0)),
            pl.BlockSpec((CB, M_HEADS, M_DK), lambda b, c: (b, 0, 0)),
            pl.BlockSpec((CB, SUBLANES, LANES), lambda b, c: (b, 0, 0)),
        ],
        out_shape=[
            jax.ShapeDtypeStruct((BATCH, SEQ, M_V), BF16),
            jax.ShapeDtypeStruct((BATCH, M_HEADS, M_DK, M_DV), F32),
            jax.ShapeDtypeStruct((BATCH, M_HEADS, M_DK), F32),
            jax.ShapeDtypeStruct((BATCH, SUBLANES, LANES), F32),
        ],
        scratch_shapes=[pltpu.VMEM((CB, CHUNK, LANES), F32)],
        compiler_params=_cp(("parallel", "arbitrary"), 48),
        name="mixer_c_prompt",
    )(p_main, p_main, p_main, p_main, p_gate, *params)


def _ab_sample_prep_body(xbc_ref, rq_ref, rk_ref, dt_ref, cst_ref, cos_ref, sin_ref,
                         cw_ref, cb_ref, dtb_ref, alog_ref,
                         conv_ref, xs_ref, xdt_ref, eda_ref, bm_ref, cm_ref, q_ref, k_ref, gam_ref):
    ch = SSD_CONV_CH
    u = xbc_ref[...]
    w = cw_ref[...]
    b0 = cst_ref[:, 0:ch]
    b1 = cst_ref[:, ch:2 * ch]
    b2 = cst_ref[:, 2 * ch:3 * ch]
    conv = cb_ref[...] + (((b0 * w[0:1, :] + b1 * w[1:2, :]) + b2 * w[2:3, :]) + u * w[3:4, :])
    conv_ref[:, 0:ch] = b1
    conv_ref[:, ch:2 * ch] = b2
    conv_ref[:, 2 * ch:3 * ch] = u
    xbc = _silu(conv)
    xs = xbc[:, :SSD_D_INNER]
    xs_ref[...] = xs
    bm_ref[...] = xbc[:, SSD_D_INNER:SSD_D_INNER + GROUP_W]
    cm_ref[...] = xbc[:, SSD_D_INNER + GROUP_W:]
    dt = _softplus(dt_ref[...] + dtb_ref[...])
    eda = jnp.exp(dt * (-jnp.exp(alog_ref[...])))
    xdt_ref[...] = xs * _pair_expand(dt, SSD_HEADS)
    eda_ref[...] = _pair_expand(eda, SSD_HEADS)
    cos = cos_ref[...]
    sin = sin_ref[...]
    for h in range(RET_HEADS):
        a = slice(h * RET_DK, h * RET_DK + LANES)
        b = slice(h * RET_DK + LANES, (h + 1) * RET_DK)
        q1, q2 = rq_ref[:, a], rq_ref[:, b]
        k1, k2 = rk_ref[:, a], rk_ref[:, b]
        q_ref[:, a] = q1 * cos - q2 * sin
        q_ref[:, b] = q1 * sin + q2 * cos
        k_ref[:, a] = (k1 * cos - k2 * sin) * (RET_DK ** -0.5)
        k_ref[:, b] = (k1 * sin + k2 * cos) * (RET_DK ** -0.5)
        gam_ref[:, h * RET_DK:(h + 1) * RET_DK] = jnp.full((DEC_BATCH, RET_DK), math.exp(_ret_log_gamma(h)), F32)


def _ab_sample_prep(p_main, p_dt, conv_state, cos, sin, conv_w, conv_b, dt_bias, a_log):
    n = DEC_BATCH

    def colspec(width, cb):
        return pl.BlockSpec((n, width), lambda i: (0, cb))

    def full(a):
        return pl.BlockSpec(a.shape, lambda i: (0,) * a.ndim)

    small = [conv_state, cos, sin, conv_w, conv_b, dt_bias, a_log]

    def out(width):
        return jax.ShapeDtypeStruct((n, width), F32)

    widths = [(SSD_CONV - 1) * SSD_CONV_CH, SSD_D_INNER, SSD_D_INNER, SSD_D_INNER, GROUP_W, GROUP_W,
              RET_QK, RET_QK, RET_QK]
    return pl.pallas_call(
        _ab_sample_prep_body,
        grid=(1,),
        in_specs=[colspec(SSD_CONV_CH, 2), colspec(RET_QK, 9), colspec(RET_QK, 10), full(p_dt)]
        + [full(a) for a in small],
        out_specs=[pl.BlockSpec((n, wd), lambda i: (0, 0)) for wd in widths],
        out_shape=[out(wd) for wd in widths],
        compiler_params=_cp(("arbitrary",), 48),
        name="ab_sample_prep",
    )(p_main, p_main, p_main, p_dt, *small)


def _ssd_state_body(eda_ref, xdt_ref, bm_ref, cm_ref, h_ref, ho_ref, y_ref):
    eda_t = _pad_t(eda_ref[...])
    xdt_t = _pad_t(xdt_ref[...])
    cm = _bf(jnp.concatenate([cm_ref[...], jnp.zeros((SB, SSD_D_STATE), F32)], axis=0))
    for b in range(SB):
        hn = h_ref[b] * eda_t[:, b:b + 1] + xdt_t[:, b:b + 1] * bm_ref[b:b + 1, :]
        ho_ref[b] = hn
        y_ref[b:b + 1, :] = _dot_nt(cm, _bf(hn))[b:b + 1, :]


def _ssd_state(eda, xdt, bm, cm, state):
    vec = pl.BlockSpec((SB, GROUP_W), lambda i, g: (i, g))
    bc = pl.BlockSpec((SB, SSD_D_STATE), lambda i, g: (i, g))
    st = pl.BlockSpec((SB, GROUP_W, SSD_D_STATE), lambda i, g: (i, g, 0))
    return pl.pallas_call(
        _ssd_state_body,
        grid=(DEC_BATCH // SB, SSD_GROUPS),
        in_specs=[vec, vec, bc, bc, st],
        out_specs=[st, vec],
        out_shape=[jax.ShapeDtypeStruct(state.shape, F32), jax.ShapeDtypeStruct((DEC_BATCH, SSD_D_INNER), F32)],
        compiler_params=_cp(("parallel", "arbitrary"), 48),
        name="ssd_state",
    )(eda, xdt, bm, cm, state)


def _outer_state_body(d_ref, k_ref, q_ref, v_ref, s_ref, so_ref, o_ref):
    d_t = _pad_t(d_ref[...])
    k_t = _pad_t(k_ref[...])
    q_t = _pad_t(q_ref[...])
    for b in range(SB):
        sn = s_ref[b] * d_t[:, b:b + 1] + k_t[:, b:b + 1] * v_ref[b:b + 1, :]
        so_ref[b] = sn
        o_ref[b:b + 1, :] = jnp.sum(sn * q_t[:, b:b + 1], axis=0, keepdims=True)


def _outer_state(d, k, q, v, state):
    _, nh, dk, dv = state.shape
    kv = pl.BlockSpec((SB, dk), lambda i, h: (i, h))
    vv = pl.BlockSpec((SB, dv), lambda i, h: (i, h))
    st = pl.BlockSpec((SB, None, dk, dv), lambda i, h: (i, h, 0, 0))
    return pl.pallas_call(
        _outer_state_body,
        grid=(DEC_BATCH // SB, nh),
        in_specs=[kv, kv, kv, vv, st],
        out_specs=[st, vv],
        out_shape=[jax.ShapeDtypeStruct(state.shape, F32), jax.ShapeDtypeStruct((DEC_BATCH, nh * dv), F32)],
        compiler_params=_cp(("parallel", "arbitrary"), 48),
        name="outer_state",
    )(d, k, q, v, state)


def _ab_sample_post_body(y_ref, xs_ref, z_ref, r_ref, rg_ref, dsk_ref, nrm_ref, o_ref):
    for g in range(SSD_GROUPS):
        gs = slice(g * GROUP_W, (g + 1) * GROUP_W)
        yg = (y_ref[:, gs] + xs_ref[:, gs] * dsk_ref[:, gs]) * _silu(z_ref[:, gs])
        o_ref[:, gs] = _bf(_rms(yg) * nrm_ref[:, gs])
    for h in range(RET_HEADS):
        vs = slice(h * RET_DV, (h + 1) * RET_DV)
        os = slice(SSD_D_INNER + h * RET_DV, SSD_D_INNER + (h + 1) * RET_DV)
        o_ref[:, os] = _bf(_silu(rg_ref[:, vs]) * _rms(r_ref[:, vs]))


def _ab_sample_post(y, xs, p_main, r, d_skip_e, ssd_norm):
    n = DEC_BATCH

    def full(a):
        return pl.BlockSpec(a.shape, lambda i: (0,) * a.ndim)

    return pl.pallas_call(
        _ab_sample_post_body,
        grid=(1,),
        in_specs=[full(y), full(xs), pl.BlockSpec((n, SSD_D_INNER), lambda i: (0, 0)), full(r),
                  pl.BlockSpec((n, RET_V), lambda i: (0, 2)), full(d_skip_e), full(ssd_norm)],
        out_specs=pl.BlockSpec((n, SSD_D_INNER + RET_V), lambda i: (0, 0)),
        out_shape=jax.ShapeDtypeStruct((n, SSD_D_INNER + RET_V), BF16),
        compiler_params=_cp(("arbitrary",), 48),
        name="ab_sample_post",
    )(y, xs, p_main, r, p_main, d_skip_e, ssd_norm)


def _c_sample_prep_body(q_ref, k_ref, gt_ref, n_ref, m_ref, ib_ref, fb_ref,
                        dpe_ref, kw_ref, nn_ref, mn_ref, dn_ref):
    ipre = gt_ref[:, :LANES] + ib_ref[...]
    lf = -_softplus(-(gt_ref[:, LANES:] + fb_ref[...]))
    m_prev = m_ref[...]
    mt = jnp.maximum(lf + m_prev, ipre)
    wgt = jnp.exp(ipre - mt)
    dp = jnp.exp(lf + m_prev - mt)
    emt = jnp.exp(-mt)
    mn_ref[...] = mt
    for h in range(M_HEADS):
        ks = slice(h * M_DK, (h + 1) * M_DK)
        dpe = _lane_bcast(dp, h)
        kw = k_ref[:, ks] * (M_DK ** -0.5) * _lane_bcast(wgt, h)
        nn = n_ref[:, ks] * dpe + kw
        den = jnp.sum(nn * q_ref[:, ks], axis=1, keepdims=True)
        dpe_ref[:, ks] = dpe
        kw_ref[:, ks] = kw
        nn_ref[:, ks] = nn
        dn_ref[:, h * M_DV:(h + 1) * M_DV] = jnp.broadcast_to(
            jnp.maximum(jnp.abs(den), emt[:, h:h + 1]), (DEC_BATCH, M_DV))


def _c_sample_prep(p_main, p_gate, n_state, m_state, i_bias, f_bias):
    n = DEC_BATCH

    def full(a):
        return pl.BlockSpec(a.shape, lambda i: (0,) * a.ndim)

    widths = [M_QK, M_QK, M_QK, LANES, M_V]
    return pl.pallas_call(
        _c_sample_prep_body,
        grid=(1,),
        in_specs=[pl.BlockSpec((n, M_QK), lambda i: (0, 0)), pl.BlockSpec((n, M_QK), lambda i: (0, 1)),
                  full(p_gate), full(n_state), full(m_state), full(i_bias), full(f_bias)],
        out_specs=[pl.BlockSpec((n, wd), lambda i: (0, 0)) for wd in widths],
        out_shape=[jax.ShapeDtypeStruct((n, wd), F32) for wd in widths],
        compiler_params=_cp(("arbitrary",), 48),
        name="c_sample_prep",
    )(p_main, p_main, p_gate, n_state, m_state, i_bias, f_bias)


def _c_sample_post_body(num_ref, dn_ref, o_ref, nrm_ref, h_ref):
    for h in range(M_HEADS):
        vs = slice(h * M_DV, (h + 1) * M_DV)
        hc = num_ref[:, vs] / dn_ref[:, vs]
        h_ref[:, vs] = _bf(jax.nn.sigmoid(o_ref[:, vs]) * (_rms(hc) * nrm_ref[:, vs]))


def _c_sample_post(num, den, p_main, norm_g):
    n = DEC_BATCH

    def full(a):
        return pl.BlockSpec(a.shape, lambda i: (0,) * a.ndim)

    return pl.pallas_call(
        _c_sample_post_body,
        grid=(1,),
        in_specs=[full(num), full(den), pl.BlockSpec((n, M_V), lambda i: (0, 2)), full(norm_g)],
        out_specs=pl.BlockSpec((n, M_V), lambda i: (0, 0)),
        out_shape=jax.ShapeDtypeStruct((n, M_V), BF16),
        compiler_params=_cp(("arbitrary",), 48),
        name="c_sample_post",
    )(num, den, p_main, norm_g)


def _rope_tables(pos):
    half = RET_DK // 2
    inv = jnp.exp(-math.log(ROPE_BASE) * jnp.arange(half, dtype=F32) / half)
    ang = pos.astype(F32)[:, None] * inv
    return jnp.cos(ang), jnp.sin(ang)


def _pad_lanes(v, width=LANES):
    return jnp.pad(v.reshape(1, -1), ((0, 0), (0, width - v.size)))


def kernel(x_prompt, x_sample, cache_mem_k, cache_mem_v, state_conv, state_ssm, state_ret, state_mlstm_c, state_mlstm_n, state_mlstm_m, mem_prompt, norm_ffn1, w_ffn1_in, w_ffn1_out, norm_mix, w_in_ab, ssd_conv_w, ssd_conv_b, ssd_dt_bias, ssd_a_log, ssd_d, ssd_norm, w_out_ab, w_in_c, mlstm_i_bias, mlstm_f_bias, mlstm_norm, w_out_c, norm_xattn, norm_mem, w_xq, w_xkv, w_xo, norm_ffn2, w_ffn2_in, w_ffn2_out, norm_final):
    g3 = lambda g: g.reshape(DEPTH, 1, D_MODEL)
    n_ffn1, n_mix, n_x, n_mem, n_ffn2 = g3(norm_ffn1), g3(norm_mix), g3(norm_xattn), g3(norm_mem), g3(norm_ffn2)

    f1_in, f1_out, f2_in, f2_out = (w.astype(BF16) for w in (w_ffn1_in, w_ffn1_out, w_ffn2_in, w_ffn2_out))
    w_out_ab, w_out_c, w_xq, w_xkv, w_xo = (w.astype(BF16) for w in (w_out_ab, w_out_c, w_xq, w_xkv, w_xo))

    wz, wxbc, wdt, wrq, wrk, wrv, wrg = jnp.split(w_in_ab[0], np_cumsum(AB_SIZES), axis=1)
    w_ab_main = jnp.concatenate([wz, wrv, wrg, wxbc, wrq, wrk], axis=1).astype(BF16)[None]
    w_ab_dt = jnp.pad(wdt, ((0, 0), (0, LANES - SSD_HEADS))).astype(BF16)[None]
    wq, wk, wv, wi, wf, wo = jnp.split(w_in_c[0], np_cumsum(C_SIZES), axis=1)
    w_c_main = jnp.concatenate([wq, wk, wv, wo], axis=1).astype(BF16)[None]
    gpad = ((0, 0), (0, LANES - M_HEADS))
    w_c_gate = jnp.concatenate([jnp.pad(wi, gpad), jnp.pad(wf, gpad)], axis=1).astype(BF16)[None]

    conv_w = ssd_conv_w[0]
    conv_b = ssd_conv_b.reshape(1, SSD_CONV_CH)
    dt_bias = _pad_lanes(ssd_dt_bias[0])
    a_log = _pad_lanes(ssd_a_log[0])
    d_skip_e = jnp.repeat(ssd_d[0], SSD_HEAD_DIM).reshape(1, SSD_D_INNER)
    s_norm = ssd_norm.reshape(1, SSD_D_INNER)
    i_bias = _pad_lanes(mlstm_i_bias[0])
    f_bias = _pad_lanes(mlstm_f_bias[0])
    m_norm = mlstm_norm.reshape(1, M_V)

    memk, memv = _mem_kv(mem_prompt.reshape(BATCH * MEM_LEN, D_MODEL), n_mem, w_xkv)
    cos_p, sin_p = _rope_tables(jnp.arange(SEQ))
    x = x_prompt.reshape(BATCH * SEQ, D_MODEL)
    x = _ffn(x, n_ffn1, f1_in, f1_out, 0)
    w_gates = w_ab_main[0, :, :AB_GATE_BLOCKS * AB_TN]
    w_xqk = w_ab_main[0, :, AB_GATE_BLOCKS * AB_TN:]
    p_gates, p_xqk, p_dt = _ab_inproj(x, n_mix, w_gates, w_xqk, w_ab_dt[0], dt_bias, conv_w, conv_b, cos_p, sin_p)
    ycat, ssm_p, ret_p = _mixer_ab_prompt(p_gates, p_xqk, p_dt, a_log, d_skip_e, s_norm)
    x_tail = x.reshape(BATCH, SEQ, D_MODEL)[:, SEQ - SSD_CONV:].reshape(BATCH * SSD_CONV, D_MODEL)
    conv_p = _norm_proj(x_tail, n_mix, 0, w_ab_main, 0, AB_TN, col0=AB_XBC0, n_out=SSD_CONV_CH)
    conv_p = conv_p.reshape(BATCH, SSD_CONV, SSD_CONV_CH)[:, 1:]
    x = _proj_residual(x, ycat, w_out_ab, 0)
    x = _xattn_prompt(x, n_x, w_xq, w_xo, memk, memv, 0)
    x = _ffn(x, n_ffn2, f2_in, f2_out, 0)
    x = _ffn(x, n_ffn1, f1_in, f1_out, 1)
    pc_main, pc_gate = _c_inproj(x, n_mix, 1, w_c_main, w_c_gate)
    hout, mc_p, mn_p, mm_p = _mixer_c_prompt(pc_main.reshape(BATCH, SEQ, -1), pc_gate.reshape(BATCH, SEQ, -1),
                                             i_bias, f_bias, m_norm)
    x = _proj_residual(x, hout.reshape(BATCH * SEQ, M_V), w_out_c, 0)
    x = _xattn_prompt(x, n_x, w_xq, w_xo, memk, memv, 1)
    y_prompt = _ffn(x, n_ffn2, f2_in, f2_out, 1, norm_final).reshape(BATCH, SEQ, D_MODEL)

    cos_s, sin_s = _rope_tables(PAST_LEN + jnp.arange(1))

    def xattn_s(xs_, layer):
        q = _norm_proj(xs_, n_x, layer, w_xq, layer, 1024).reshape(DEC_BATCH, X_HEADS, X_HEAD_DIM)
        o = _xattn_sample(q, cache_mem_k, cache_mem_v, layer).reshape(DEC_BATCH, D_MODEL)
        return _proj_residual(xs_, o, w_xo, layer)

    xs_ = x_sample.reshape(DEC_BATCH, D_MODEL)
    xs_ = _ffn(xs_, n_ffn1, f1_in, f1_out, 0)
    sp_main, sp_dt = _norm_proj(xs_, n_mix, 0, w_ab_main, 0, 1024, w_ab_dt)
    conv_s, xs_c, xdt, eda, bm_s, cm_s, q_s, k_s, gam = _ab_sample_prep(
        sp_main, sp_dt, state_conv.reshape(DEC_BATCH, (SSD_CONV - 1) * SSD_CONV_CH), cos_s, sin_s,
        conv_w, conv_b, dt_bias, a_log)
    ssm_s, y_s = _ssd_state(eda, xdt, bm_s, cm_s, state_ssm.reshape(DEC_BATCH, SSD_D_INNER, SSD_D_STATE))
    ret_s, r_s = _outer_state(gam, k_s, q_s, sp_main[:, SSD_D_INNER:SSD_D_INNER + RET_V], state_ret[0])
    ycat_s = _ab_sample_post(y_s, xs_c, sp_main, r_s, d_skip_e, s_norm)
    xs_ = _proj_residual(xs_, ycat_s, w_out_ab, 0)
    xs_ = xattn_s(xs_, 0)
    xs_ = _ffn(xs_, n_ffn2, f2_in, f2_out, 0)
    xs_ = _ffn(xs_, n_ffn1, f1_in, f1_out, 1)
    sc_main, sc_gate = _norm_proj(xs_, n_mix, 1, w_c_main, 0, 1024, w_c_gate)
    m_in = jnp.pad(state_mlstm_m[0], ((0, 0), (0, LANES - M_HEADS)))
    dpe, kw, mn_s, mm_s, den = _c_sample_prep(sc_main, sc_gate, state_mlstm_n.reshape(DEC_BATCH, M_QK), m_in,
                                              i_bias, f_bias)
    mc_s, num = _outer_state(dpe, kw, sc_main[:, :M_QK], sc_main[:, 2 * M_QK:2 * M_QK + M_V], state_mlstm_c[0])
    hout_s = _c_sample_post(num, den, sc_main, m_norm)
    xs_ = _proj_residual(xs_, hout_s, w_out_c, 0)
    xs_ = xattn_s(xs_, 1)
    y_sample = _ffn(xs_, n_ffn2, f2_in, f2_out, 1, norm_final).reshape(DEC_BATCH, 1, D_MODEL)

    kv_shape = (DEPTH, BATCH, MEM_LEN, X_HEADS, X_HEAD_DIM)
    return (y_prompt, y_sample, memk.reshape(kv_shape), memv.reshape(kv_shape),
            conv_p.reshape(1, BATCH, SSD_CONV - 1, SSD_CONV_CH),
            conv_s.reshape(1, DEC_BATCH, SSD_CONV - 1, SSD_CONV_CH),
            ssm_p.reshape(1, BATCH, SSD_HEADS, SSD_HEAD_DIM, SSD_D_STATE),
            ssm_s.reshape(1, DEC_BATCH, SSD_HEADS, SSD_HEAD_DIM, SSD_D_STATE),
            ret_p[None], ret_s[None], mc_p[None], mc_s[None],
            mn_p[None], mn_s.reshape(1, DEC_BATCH, M_HEADS, M_DK),
            mm_p[:, :M_HEADS, 0][None], mm_s[:, :M_HEADS][None])


def np_cumsum(sizes):
    out, acc = [], 0
    for s in sizes[:-1]:
        acc += s
        out.append(acc)
    return out
```

```python
import functools
import math

import jax
import jax.numpy as jnp
from jax import lax
from jax.experimental import pallas as pl
from jax.experimental.pallas import tpu as pltpu

F32 = jnp.float32
BF16 = jnp.bfloat16
EPS = 1e-6

D_MODEL = 1024
BATCH = 8
SEQ = 2048
DEPTH = 2
DEC_BATCH = 128
PAST_LEN = 16384
CHUNK = 128
D_FF = 2816
SSD_D_INNER = 2 * D_MODEL
SSD_HEAD_DIM = 64
SSD_HEADS = SSD_D_INNER // SSD_HEAD_DIM
SSD_GROUPS = 4
SSD_D_STATE = 128
SSD_CONV = 4
SSD_CONV_CH = SSD_D_INNER + 2 * SSD_GROUPS * SSD_D_STATE
RET_HEADS = 4
RET_QK = D_MODEL
RET_V = 2 * D_MODEL
RET_DK = RET_QK // RET_HEADS
RET_DV = RET_V // RET_HEADS
ROPE_BASE = 10000.0
AB_SIZES = (SSD_D_INNER, SSD_CONV_CH, SSD_HEADS, RET_QK, RET_QK, RET_V, RET_V)
M_HEADS = 4
M_QK = D_MODEL // 2
M_V = D_MODEL
M_DK = M_QK // M_HEADS
M_DV = M_V // M_HEADS
C_SIZES = (M_QK, M_QK, M_V, M_HEADS, M_HEADS, M_V)
MEM_LEN = 256
X_HEADS = 4
X_HEAD_DIM = D_MODEL // X_HEADS

LANES = 128
SUBLANES = 8
GROUP_W = SSD_D_INNER // SSD_GROUPS
HEADS_PER_GROUP = SSD_HEADS // SSD_GROUPS
ROW_TILE = 1024
FF_TILE = 256
SSD_SB = 16
STATE_BLOCK_BYTES = 8 * 1024 * 1024
XB = 4
CB = 2


def _cp(sem, mib):
    return pltpu.CompilerParams(dimension_semantics=sem, vmem_limit_bytes=mib * 1024 * 1024)


def _bf(x):
    return x.astype(BF16)


def _dot(a, b):
    return jnp.dot(a, b, preferred_element_type=F32)


def _dot_nt(a, b):
    return lax.dot_general(a, b, (((1,), (1,)), ((), ())), preferred_element_type=F32)


def _dot_tn(a, b):
    return lax.dot_general(a, b, (((0,), (0,)), ((), ())), preferred_element_type=F32)


def _rms(x):
    return x * lax.rsqrt(jnp.mean(x * x, axis=-1, keepdims=True) + EPS)


def _silu(x):
    return x * jax.nn.sigmoid(x)


def _softplus(x):
    return jnp.maximum(x, 0.0) + jnp.log1p(jnp.exp(-jnp.abs(x)))


def _split3(x):
    hi = x.astype(BF16)
    r = x - hi.astype(F32)
    mid = r.astype(BF16)
    lo = (r - mid.astype(F32)).astype(BF16)
    return hi, mid, lo


def _cumsum_rows(x):
    n = x.shape[0]
    r = lax.broadcasted_iota(jnp.int32, (n, n), 0)
    c = lax.broadcasted_iota(jnp.int32, (n, n), 1)
    t = jnp.where(r >= c, 1.0, 0.0).astype(BF16)
    hi, mid, lo = _split3(x)
    return _dot(t, hi) + _dot(t, mid) + _dot(t, lo)


def _lane_bcast(x, h, width=LANES):
    return jnp.broadcast_to(x[:, h:h + 1], (x.shape[0], width))


def _pair_expand(x, n_heads):
    rows = x.shape[0]
    lo = lax.broadcasted_iota(jnp.int32, (rows, LANES), 1) < SSD_HEAD_DIM
    return jnp.concatenate(
        [jnp.where(lo, _lane_bcast(x, 2 * j), _lane_bcast(x, 2 * j + 1)) for j in range(n_heads // 2)], axis=1)


def _pad_t(x):
    pad = jnp.zeros((LANES - x.shape[0], x.shape[1]), F32)
    return jnp.concatenate([x, pad], axis=0).T


def _ffn_body(*refs, nf, final):
    if final:
        x_ref, g_ref, wi_ref, wo_ref, fg_ref, o_ref = refs
    else:
        x_ref, g_ref, wi_ref, wo_ref, o_ref = refs
    x = x_ref[...]
    xn = _bf(_rms(x) * g_ref[...])
    acc = None
    for f in range(nf):
        fs = slice(f * FF_TILE, (f + 1) * FF_TILE)
        g = _dot(xn, wi_ref[:, fs])
        u = _dot(xn, wi_ref[:, D_FF + f * FF_TILE:D_FF + (f + 1) * FF_TILE])
        t = _dot(_bf(_silu(g) * u), wo_ref[fs, :])
        acc = t if acc is None else acc + t
    y = x + 0.5 * acc
    if final:
        y = _rms(y) * fg_ref[...]
    o_ref[...] = y


def _ffn(x, g3, w_in, w_out, layer, final_g=None):
    m = x.shape[0]
    tm = min(m, ROW_TILE)
    nf = D_FF // FF_TILE
    once = pl.Buffered(1)
    in_specs = [
        pl.BlockSpec((tm, D_MODEL), lambda i: (i, 0)),
        pl.BlockSpec((None, 1, D_MODEL), lambda i: (layer, 0, 0)),
        pl.BlockSpec((None, D_MODEL, 2 * D_FF), lambda i: (layer, 0, 0), pipeline_mode=once),
        pl.BlockSpec((None, D_FF, D_MODEL), lambda i: (layer, 0, 0), pipeline_mode=once),
    ]
    args = [x, g3, w_in, w_out]
    if final_g is not None:
        in_specs.append(pl.BlockSpec((1, D_MODEL), lambda i: (0, 0)))
        args.append(final_g.reshape(1, D_MODEL))
    return pl.pallas_call(
        functools.partial(_ffn_body, nf=nf, final=final_g is not None),
        grid=(m // tm,),
        in_specs=in_specs,
        out_specs=pl.BlockSpec((tm, D_MODEL), lambda i: (i, 0)),
        out_shape=jax.ShapeDtypeStruct((m, D_MODEL), F32),
        compiler_params=_cp(("parallel",), 56),
        name="ffn",
    )(*args)


def _norm_proj_body(*refs, small):
    if small:
        x_ref, g_ref, w_ref, ws_ref, o_ref, os_ref, xn_ref = refs
    else:
        x_ref, g_ref, w_ref, o_ref, xn_ref = refs
    n = pl.program_id(1)

    @pl.when(n == 0)
    def _():
        xn = _bf(_rms(x_ref[...]) * g_ref[...])
        xn_ref[...] = xn
        if small:
            os_ref[...] = _dot(xn, _bf(ws_ref[...]))

    o_ref[...] = _dot(xn_ref[...], _bf(w_ref[...]))


def _norm_proj(x, g3, glayer, w3, wlayer, tn, w_small=None, col0=0, n_out=None):
    m = x.shape[0]
    tm = min(m, ROW_TILE)
    n_out = w3.shape[-1] if n_out is None else n_out
    in_specs = [
        pl.BlockSpec((tm, D_MODEL), lambda i, n: (i, 0)),
        pl.BlockSpec((None, 1, D_MODEL), lambda i, n: (glayer, 0, 0)),
        pl.BlockSpec((None, D_MODEL, tn), lambda i, n: (wlayer, 0, n + col0)),
    ]
    args = [x, g3, w3]
    out_specs = [pl.BlockSpec((tm, tn), lambda i, n: (i, n))]
    out_shape = [jax.ShapeDtypeStruct((m, n_out), F32)]
    if w_small is not None:
        ns = w_small.shape[-1]
        in_specs.append(pl.BlockSpec((None, D_MODEL, ns), lambda i, n: (0, 0, 0)))
        args.append(w_small)
        out_specs.append(pl.BlockSpec((tm, ns), lambda i, n: (i, 0)))
        out_shape.append(jax.ShapeDtypeStruct((m, ns), F32))
    res = pl.pallas_call(
        functools.partial(_norm_proj_body, small=w_small is not None),
        grid=(m // tm, n_out // tn),
        in_specs=in_specs,
        out_specs=out_specs,
        out_shape=out_shape,
        scratch_shapes=[pltpu.VMEM((tm, D_MODEL), BF16)],
        compiler_params=_cp(("parallel", "arbitrary"), 48),
        name="norm_proj",
    )(*args)
    return res if w_small is not None else res[0]


AB_TN = 1024
AB_XBC0 = (SSD_D_INNER + 2 * RET_V) // AB_TN
AB_RQ = AB_XBC0 + SSD_CONV_CH // AB_TN


AB_GATE_BLOCKS = AB_XBC0
AB_XQK_BLOCKS = SSD_CONV_CH // AB_TN + 2 * RET_QK // AB_TN
AB_ROW_TILE = 512


def _ab_gates_body(x_ref, g_ref, w_ref, o_ref):
    xn = _bf(_rms(x_ref[...]) * g_ref[...])
    nz = SSD_D_INNER // AB_TN
    nv = RET_V // AB_TN
    for n in range(AB_GATE_BLOCKS):
        cs = slice(n * AB_TN, (n + 1) * AB_TN)
        r = _dot(xn, w_ref[:, cs])
        o_ref[:, cs] = _bf(r) if nz <= n < nz + nv else _bf(_silu(r))


def _ab_xqk_body(x_ref, g_ref, w_ref, ws_ref, dtb_ref, cw_ref, cb_ref, cos_ref, sin_ref,
                 o_ref, dt_ref, xpad, rbuf, ybuf, carry, *, tiles_per_seq):
    i = pl.program_id(0)
    tm = x_ref.shape[0]
    nslab = tm // SUBLANES
    pitch = nslab + SUBLANES
    nxb = SSD_CONV_CH // AB_TN
    ntap = SSD_CONV - 1

    @pl.when(i == 0)
    def _():
        carry[...] = jnp.zeros_like(carry)

    g = g_ref[...]
    xn = _bf(_rms(x_ref[...]) * g)
    dt_ref[...] = _softplus(_dot(xn, ws_ref[...]) + dtb_ref[...])
    for lb in range(D_MODEL // LANES):
        for s in range(SUBLANES):
            xpad[lb, s * pitch:s * pitch + nslab, :] = x_ref[s * nslab:(s + 1) * nslab, lb * LANES:(lb + 1) * LANES]
    xp = jnp.concatenate(
        [jnp.concatenate([xpad[lb, pl.ds(v, SUBLANES, stride=pitch), :] for lb in range(D_MODEL // LANES)], axis=1)
         for v in range(nslab)], axis=0)
    xnp = _bf(_rms(xp) * g)
    seq_start = i % tiles_per_seq == 0
    first = lax.broadcasted_iota(jnp.int32, (SUBLANES, AB_TN), 0) == 0
    halo = ntap * SUBLANES
    for n in range(nxb):
        cs = slice(n * AB_TN, (n + 1) * AB_TN)
        rbuf[n, halo:halo + tm, :] = _dot(xnp, w_ref[:, cs])
        prev_rows = jnp.where(seq_start, 0.0, carry[n])
        for k in range(ntap):
            hi = rbuf[n, tm + k * SUBLANES:tm + (k + 1) * SUBLANES, :]
            rbuf[n, k * SUBLANES:(k + 1) * SUBLANES, :] = jnp.where(first, prev_rows[k:k + 1, :],
                                                                     pltpu.roll(hi, 1, axis=0))
            carry[n, k:k + 1, :] = hi[SUBLANES - 1:SUBLANES, :]
        w = cw_ref[:, cs]
        conv = cb_ref[:, cs] + rbuf[n, halo:halo + tm, :] * w[ntap:ntap + 1, :]
        for j in range(1, SSD_CONV):
            off = halo - j * SUBLANES
            conv = conv + rbuf[n, off:off + tm, :] * w[ntap - j:ntap - j + 1, :]
        y = _silu(conv)
        for lb in range(AB_TN // LANES):
            ls = slice(lb * LANES, (lb + 1) * LANES)
            for v in range(nslab):
                ybuf[n, lb, pl.ds(v, SUBLANES, stride=pitch), :] = y[v * SUBLANES:(v + 1) * SUBLANES, ls]
            for s in range(SUBLANES):
                o_ref[s * nslab:(s + 1) * nslab, n * AB_TN + ls.start:n * AB_TN + ls.stop] = _bf(
                    ybuf[n, lb, s * pitch:s * pitch + nslab, :])
    cos = cos_ref[...]
    sin = sin_ref[...]
    for n in range(nxb, AB_XQK_BLOCKS):
        r = _dot(xn, w_ref[:, n * AB_TN:(n + 1) * AB_TN])
        scale = 1.0 if n == nxb else RET_DK ** -0.5
        for h in range(AB_TN // RET_DK):
            a = slice(h * RET_DK, h * RET_DK + LANES)
            b = slice(h * RET_DK + LANES, (h + 1) * RET_DK)
            x1, x2 = r[:, a], r[:, b]
            o_ref[:, n * AB_TN + a.start:n * AB_TN + a.stop] = _bf((x1 * cos - x2 * sin) * scale)
            o_ref[:, n * AB_TN + b.start:n * AB_TN + b.stop] = _bf((x1 * sin + x2 * cos) * scale)


def _ab_inproj(x, g3, w_gates, w_xqk, w_dt, dt_bias, conv_w, conv_b, cos, sin):
    m = x.shape[0]
    tm = AB_ROW_TILE
    tps = SEQ // tm
    nxb = SSD_CONV_CH // AB_TN
    once = pl.Buffered(1)

    def full(a):
        return pl.BlockSpec(a.shape, lambda i: (0,) * a.ndim, pipeline_mode=once)

    xspec = pl.BlockSpec((tm, D_MODEL), lambda i: (i, 0))
    gspec = pl.BlockSpec((None, 1, D_MODEL), lambda i: (0, 0, 0))
    gates = pl.pallas_call(
        _ab_gates_body,
        grid=(m // tm,),
        in_specs=[xspec, gspec, full(w_gates)],
        out_specs=pl.BlockSpec((tm, AB_GATE_BLOCKS * AB_TN), lambda i: (i, 0)),
        out_shape=jax.ShapeDtypeStruct((m, AB_GATE_BLOCKS * AB_TN), BF16),
        compiler_params=_cp(("parallel",), 56),
        name="ab_gates",
    )(x, g3, w_gates)
    xqk, dt = pl.pallas_call(
        functools.partial(_ab_xqk_body, tiles_per_seq=tps),
        grid=(m // tm,),
        in_specs=[xspec, gspec, full(w_xqk), full(w_dt), full(dt_bias), full(conv_w), full(conv_b),
                  pl.BlockSpec((tm, LANES), lambda i: (i % tps, 0)),
                  pl.BlockSpec((tm, LANES), lambda i: (i % tps, 0))],
        out_specs=[pl.BlockSpec((tm, AB_XQK_BLOCKS * AB_TN), lambda i: (i, 0)),
                   pl.BlockSpec((tm, LANES), lambda i: (i, 0))],
        out_shape=[jax.ShapeDtypeStruct((m, AB_XQK_BLOCKS * AB_TN), BF16), jax.ShapeDtypeStruct((m, LANES), F32)],
        scratch_shapes=[pltpu.VMEM((D_MODEL // LANES, tm + SUBLANES * SUBLANES, LANES), F32),
                        pltpu.VMEM((nxb, (SSD_CONV - 1) * SUBLANES + tm, AB_TN), F32),
                        pltpu.VMEM((nxb, AB_TN // LANES, tm + SUBLANES * SUBLANES, LANES), F32),
                        pltpu.VMEM((nxb, SUBLANES, AB_TN), F32)],
        compiler_params=_cp(("arbitrary",), 56),
        name="ab_xqk",
    )(x, g3, w_xqk, w_dt, dt_bias, conv_w, conv_b, cos, sin)
    return gates, xqk, dt


def _c_inproj_body(x_ref, g_ref, w_ref, wg_ref, o_ref, gt_ref):
    xn = _bf(_rms(x_ref[...]) * g_ref[...])
    gt_ref[...] = _dot(xn, wg_ref[...])
    o_ref[:, :M_QK] = _bf(_dot(xn, w_ref[:, :M_QK]))
    o_ref[:, M_QK:2 * M_QK] = _bf(_dot(xn, w_ref[:, M_QK:2 * M_QK]) * (M_DK ** -0.5))
    o_ref[:, 2 * M_QK:2 * M_QK + M_V] = _bf(_dot(xn, w_ref[:, 2 * M_QK:2 * M_QK + M_V]))
    o_ref[:, 2 * M_QK + M_V:] = _bf(jax.nn.sigmoid(_dot(xn, w_ref[:, 2 * M_QK + M_V:])))


def _c_inproj(x, g3, glayer, w_main, w_gate):
    m = x.shape[0]
    tm = ROW_TILE
    once = pl.Buffered(1)
    n_out = w_main.shape[-1]
    n_gate = w_gate.shape[-1]
    return pl.pallas_call(
        _c_inproj_body,
        grid=(m // tm,),
        in_specs=[
            pl.BlockSpec((tm, D_MODEL), lambda i: (i, 0)),
            pl.BlockSpec((None, 1, D_MODEL), lambda i: (glayer, 0, 0)),
            pl.BlockSpec((None, D_MODEL, n_out), lambda i: (0, 0, 0), pipeline_mode=once),
            pl.BlockSpec((None, D_MODEL, n_gate), lambda i: (0, 0, 0), pipeline_mode=once),
        ],
        out_specs=[pl.BlockSpec((tm, n_out), lambda i: (i, 0)), pl.BlockSpec((tm, n_gate), lambda i: (i, 0))],
        out_shape=[jax.ShapeDtypeStruct((m, n_out), BF16), jax.ShapeDtypeStruct((m, n_gate), F32)],
        compiler_params=_cp(("parallel",), 48),
        name="c_inproj",
    )(x, g3, w_main, w_gate)


def _proj_res_body(x_ref, y_ref, w_ref, o_ref):
    o_ref[...] = x_ref[...] + _dot(_bf(y_ref[...]), w_ref[...])


def _proj_residual(x, y, w3, layer):
    m = x.shape[0]
    tm = min(m, ROW_TILE)
    kdim = y.shape[1]
    return pl.pallas_call(
        _proj_res_body,
        grid=(m // tm,),
        in_specs=[
            pl.BlockSpec((tm, D_MODEL), lambda i: (i, 0)),
            pl.BlockSpec((tm, kdim), lambda i: (i, 0)),
            pl.BlockSpec((None, kdim, D_MODEL), lambda i: (layer, 0, 0), pipeline_mode=pl.Buffered(1)),
        ],
        out_specs=pl.BlockSpec((tm, D_MODEL), lambda i: (i, 0)),
        out_shape=jax.ShapeDtypeStruct((m, D_MODEL), F32),
        compiler_params=_cp(("parallel",), 56),
        name="proj_residual",
    )(x, y, w3)


def _mem_kv_body(x_ref, g_ref, w_ref, k_ref, v_ref):
    xn = _bf(_rms(x_ref[...]) * g_ref[...])
    kv = _dot(xn, _bf(w_ref[...]))
    k_ref[...] = kv[:, :D_MODEL]
    v_ref[...] = kv[:, D_MODEL:]


def _mem_kv(mem2d, g3, w_xkv):
    m = mem2d.shape[0]
    tm = 512
    shp = jax.ShapeDtypeStruct((DEPTH, m, D_MODEL), F32)
    return pl.pallas_call(
        _mem_kv_body,
        grid=(DEPTH, m // tm),
        in_specs=[
            pl.BlockSpec((tm, D_MODEL), lambda l, i: (i, 0)),
            pl.BlockSpec((None, 1, D_MODEL), lambda l, i: (l, 0, 0)),
            pl.BlockSpec((None, D_MODEL, 2 * D_MODEL), lambda l, i: (l, 0, 0)),
        ],
        out_specs=[pl.BlockSpec((None, tm, D_MODEL), lambda l, i: (l, i, 0))] * 2,
        out_shape=[shp, shp],
        compiler_params=_cp(("arbitrary", "arbitrary"), 48),
        name="mem_kv",
    )(mem2d, g3, w_xkv)


def _xattn_prompt_body(x_ref, g_ref, wq_ref, wo_ref, k_ref, v_ref, o_ref):
    x = x_ref[...]
    xn = _bf(_rms(x) * g_ref[...])
    q = _dot(xn, _bf(wq_ref[...]))
    k = _bf(k_ref[...])
    v = _bf(v_ref[...])
    outs = []
    for h in range(X_HEADS):
        sl = slice(h * X_HEAD_DIM, (h + 1) * X_HEAD_DIM)
        s = _dot_nt(_bf(q[:, sl]), k[:, sl]) * (X_HEAD_DIM ** -0.5)
        e = jnp.exp(s - jnp.max(s, axis=-1, keepdims=True))
        p = e / jnp.sum(e, axis=-1, keepdims=True)
        outs.append(_bf(_dot(_bf(p), v[:, sl])))
    o_ref[...] = x + _dot(jnp.concatenate(outs, axis=1), _bf(wo_ref[...]))


def _xattn_prompt(x, g3, w_xq, w_xo, memk, memv, layer):
    tq = ROW_TILE
    nq = SEQ // tq
    return pl.pallas_call(
        _xattn_prompt_body,
        grid=(BATCH, nq),
        in_specs=[
            pl.BlockSpec((tq, D_MODEL), lambda b, j: (b * nq + j, 0)),
            pl.BlockSpec((None, 1, D_MODEL), lambda b, j: (layer, 0, 0)),
            pl.BlockSpec((None, D_MODEL, D_MODEL), lambda b, j: (layer, 0, 0)),
            pl.BlockSpec((None, D_MODEL, D_MODEL), lambda b, j: (layer, 0, 0)),
            pl.BlockSpec((None, MEM_LEN, D_MODEL), lambda b, j: (layer, b, 0)),
            pl.BlockSpec((None, MEM_LEN, D_MODEL), lambda b, j: (layer, b, 0)),
        ],
        out_specs=pl.BlockSpec((tq, D_MODEL), lambda b, j: (b * nq + j, 0)),
        out_shape=jax.ShapeDtypeStruct((BATCH * SEQ, D_MODEL), F32),
        compiler_params=_cp(("parallel", "arbitrary"), 48),
        name="xattn_prompt",
    )(x, g3, w_xq, w_xo, memk, memv)


def _xattn_sample_body(q_ref, k_ref, v_ref, o_ref):
    for b in range(XB):
        s = jnp.sum(k_ref[b] * (q_ref[b] * (X_HEAD_DIM ** -0.5))[None], axis=-1, keepdims=True)
        e = jnp.exp(s - jnp.max(s, axis=0, keepdims=True))
        o_ref[b] = jnp.sum(e * v_ref[b], axis=0) / jnp.sum(e, axis=0)


def _xattn_sample(q, cache_k, cache_v, layer):
    blk = pl.BlockSpec((None, XB, MEM_LEN, X_HEADS, X_HEAD_DIM), lambda i: (layer, i, 0, 0, 0))
    qo = pl.BlockSpec((XB, X_HEADS, X_HEAD_DIM), lambda i: (i, 0, 0))
    return pl.pallas_call(
        _xattn_sample_body,
        grid=(DEC_BATCH // XB,),
        in_specs=[qo, blk, blk],
        out_specs=qo,
        out_shape=jax.ShapeDtypeStruct((DEC_BATCH, X_HEADS, X_HEAD_DIM), F32),
        compiler_params=_cp(("parallel",), 48),
        name="xattn_sample",
    )(q, cache_k, cache_v)


def _ret_log_gamma(h):
    return math.log1p(-(2.0 ** (-5.0 - h)))


def _ab_prompt_body(z_ref, rv_ref, rg_ref, xbc_ref, rq_ref, rk_ref, dt_ref, alog_ref, dsk_ref, nrm_ref,
                    y_ref, h_ref, s_ref):
    c = pl.program_id(1)

    @pl.when(c == 0)
    def _():
        h_ref[...] = jnp.zeros_like(h_ref)
        s_ref[...] = jnp.zeros_like(s_ref)

    xs = xbc_ref[:, :SSD_D_INNER].astype(F32)
    bm = xbc_ref[:, SSD_D_INNER:SSD_D_INNER + GROUP_W]
    cm = xbc_ref[:, SSD_D_INNER + GROUP_W:]

    dt = dt_ref[...]
    da = dt * (-jnp.exp(alog_ref[...]))
    cs = _cumsum_rows(da)
    cs_t = cs.T
    row = lax.broadcasted_iota(jnp.int32, (CHUNK, CHUNK), 0)
    col = lax.broadcasted_iota(jnp.int32, (CHUNK, CHUNK), 1)
    tri = row >= col
    lo = col < SSD_HEAD_DIM
    for g in range(SSD_GROUPS):
        gs = slice(g * GROUP_W, (g + 1) * GROUP_W)
        ns = slice(g * SSD_D_STATE, (g + 1) * SSD_D_STATE)
        cmg = cm[:, ns]
        bmg = bm[:, ns]
        att = _dot_nt(cmg, bmg)
        hprev = h_ref[gs, :]
        yint = _dot_nt(cmg, _bf(hprev))
        ys, wxs, css = [], [], []
        for j in range(HEADS_PER_GROUP // 2):
            h0 = g * HEADS_PER_GROUP + 2 * j
            cb0 = _lane_bcast(cs, h0)
            cb1 = _lane_bcast(cs, h0 + 1)
            cs_p = jnp.where(lo, cb0, cb1)
            dt_p = jnp.where(lo, _lane_bcast(dt, h0), _lane_bcast(dt, h0 + 1))
            off = g * GROUP_W + j * LANES
            xdt = xs[:, off:off + LANES] * dt_p
            d0 = jnp.exp(jnp.where(tri, cb0 - cs_t[h0:h0 + 1, :], -jnp.inf))
            d1 = jnp.exp(jnp.where(tri, cb1 - cs_t[h0 + 1:h0 + 2, :], -jnp.inf))
            yy = _dot(jnp.concatenate([_bf(att * d0), _bf(att * d1)], axis=0), _bf(xdt))
            ys.append(jnp.where(lo, yy[:CHUNK], yy[CHUNK:]) + yint[:, j * LANES:(j + 1) * LANES] * jnp.exp(cs_p))
            wxs.append(_bf(xdt * jnp.exp(cs_p[CHUNK - 1:CHUNK, :] - cs_p)))
            css.append(cs_p)
        cs_g = jnp.concatenate(css, axis=1)
        last_t = jnp.broadcast_to(cs_g[CHUNK - 1:CHUNK, :], (CHUNK, GROUP_W)).T
        h_ref[gs, :] = hprev * jnp.exp(last_t) + _dot_tn(jnp.concatenate(wxs, axis=1), bmg)
        yg = jnp.concatenate(ys, axis=1)
        yg = (yg + xs[:, gs] * dsk_ref[:, gs]) * z_ref[:, gs].astype(F32)
        y_ref[:, gs] = _bf(_rms(yg) * nrm_ref[:, gs])

    tcol = row.astype(F32)
    diff = tcol - col.astype(F32)
    for h in range(RET_HEADS):
        lg = _ret_log_gamma(h)
        ks = slice(h * RET_DK, (h + 1) * RET_DK)
        vs = slice(h * RET_DV, (h + 1) * RET_DV)
        qb = rq_ref[:, ks]
        kb = rk_ref[:, ks]
        decay = jnp.exp(jnp.where(tri, diff * lg, -jnp.inf))
        att = _dot_nt(qb, kb) * decay
        vb = rv_ref[:, vs]
        s_prev = s_ref[h]
        inner = jnp.exp((tcol + 1.0) * lg)
        r = _dot(_bf(att), vb) + _dot(qb, _bf(s_prev)) * jnp.concatenate([inner] * (RET_DV // LANES), axis=1)
        tail_w = jnp.exp((CHUNK - 1.0 - tcol) * lg)
        kt = _bf(kb.astype(F32) * jnp.concatenate([tail_w] * (RET_DK // LANES), axis=1))
        s_ref[h] = s_prev * math.exp(CHUNK * lg) + _dot_tn(kt, vb)
        os = slice(SSD_D_INNER + h * RET_DV, SSD_D_INNER + (h + 1) * RET_DV)
        y_ref[:, os] = _bf(rg_ref[:, vs].astype(F32) * _rms(r))


def _mixer_ab_prompt(p_gates, p_xqk, p_dt, a_log, d_skip_e, ssd_norm):
    nc = SEQ // CHUNK
    m = BATCH * SEQ

    def rowspec(width, cb):
        return pl.BlockSpec((CHUNK, width), lambda b, c: (b * nc + c, cb))

    def full(a):
        return pl.BlockSpec(a.shape, lambda b, c: (0,) * a.ndim)

    params = [a_log, d_skip_e, ssd_norm]
    return pl.pallas_call(
        _ab_prompt_body,
        grid=(BATCH, nc),
        in_specs=[rowspec(SSD_D_INNER, 0), rowspec(RET_V, 1), rowspec(RET_V, 2), rowspec(SSD_CONV_CH, 0),
                  rowspec(RET_QK, SSD_CONV_CH // RET_QK), rowspec(RET_QK, SSD_CONV_CH // RET_QK + 1),
                  rowspec(LANES, 0)] + [full(a) for a in params],
        out_specs=[
            pl.BlockSpec((CHUNK, SSD_D_INNER + RET_V), lambda b, c: (b * nc + c, 0)),
            pl.BlockSpec((None, SSD_D_INNER, SSD_D_STATE), lambda b, c: (b, 0, 0)),
            pl.BlockSpec((None, RET_HEADS, RET_DK, RET_DV), lambda b, c: (b, 0, 0, 0)),
        ],
        out_shape=[
            jax.ShapeDtypeStruct((m, SSD_D_INNER + RET_V), BF16),
            jax.ShapeDtypeStruct((BATCH, SSD_D_INNER, SSD_D_STATE), F32),
            jax.ShapeDtypeStruct((BATCH, RET_HEADS, RET_DK, RET_DV), F32),
        ],
        compiler_params=_cp(("parallel", "arbitrary"), 48),
        name="mixer_ab_prompt",
    )(p_gates, p_gates, p_gates, p_xqk, p_xqk, p_xqk, p_dt, *params)


def _c_prompt_body(q_ref, k_ref, v_ref, o_ref, gt_ref, ib_ref, fb_ref, nrm_ref,
                   h_ref, c_ref, n_ref, m_ref, m_s):
    @pl.when(pl.program_id(1) == 0)
    def _():
        c_ref[...] = jnp.zeros_like(c_ref)
        n_ref[...] = jnp.zeros_like(n_ref)
        m_s[...] = jnp.zeros_like(m_s)

    row = lax.broadcasted_iota(jnp.int32, (CHUNK, CHUNK), 0)
    col = lax.broadcasted_iota(jnp.int32, (CHUNK, CHUNK), 1)
    tri = row >= col
    seqs = range(CB)
    ipre = [gt_ref[s, :, :LANES] + ib_ref[...] for s in seqs]
    lf = [-_softplus(-(gt_ref[s, :, LANES:] + fb_ref[...])) for s in seqs]
    b = [_cumsum_rows(lf[s]) for s in seqs]
    g = [ipre[s] - b[s] for s in seqs]
    g_t = [g[s].T for s in seqs]
    b_t = [b[s].T for s in seqs]
    cmax = list(g_t)
    sh = 1
    while sh < CHUNK:
        cmax = [jnp.maximum(cmax[s], jnp.where(col >= sh, pltpu.roll(cmax[s], sh, axis=1), -jnp.inf)) for s in seqs]
        sh *= 2
    m_prev = [m_s[s] for s in seqs]
    mt_t = [b_t[s] + jnp.maximum(m_prev[s], cmax[s]) for s in seqs]
    mt = [mt_t[s].T for s in seqs]
    m_prev_c = [m_prev[s].T for s in seqs]
    inter = [jnp.exp(b[s] + m_prev_c[s] - mt[s]) for s in seqs]
    emt = [jnp.exp(-mt[s]) for s in seqs]
    wl = [jnp.exp(g[s] + b[s][CHUNK - 1:CHUNK, :] - mt[s][CHUNK - 1:CHUNK, :]) for s in seqs]
    bm = [b[s] - mt[s] for s in seqs]
    m_new = [_lane_bcast(mt_t[s], CHUNK - 1) for s in seqs]
    dp_t = [jnp.exp(_lane_bcast(b_t[s], CHUNK - 1) + m_prev[s] - m_new[s]) for s in seqs]
    for s in seqs:
        m_s[s] = m_new[s]
        m_ref[s] = m_new[s][0:SUBLANES, :]
    for h in range(M_HEADS):
        ks = slice(h * M_DK, (h + 1) * M_DK)
        vs = slice(h * M_DV, (h + 1) * M_DV)
        for s in seqs:
            wgt = jnp.exp(jnp.where(tri, g_t[s][h:h + 1, :] + bm[s][:, h:h + 1], -jnp.inf))
            qb = q_ref[s, :, ks]
            kb = k_ref[s, :, ks]
            vb = v_ref[s, :, vs]
            a = _dot_nt(qb, kb) * wgt
            c_prev = c_ref[s, h]
            n_prev = n_ref[s, h:h + 1, :]
            ic = inter[s][:, h:h + 1]
            num = _dot(_bf(a), vb) + _dot(qb, _bf(c_prev)) * ic
            den = (jnp.sum(a, axis=1, keepdims=True)
                   + jnp.sum(qb.astype(F32) * n_prev, axis=1, keepdims=True) * ic)
            hc = num / jnp.maximum(jnp.abs(den), emt[s][:, h:h + 1])
            kw = kb.astype(F32) * wl[s][:, h:h + 1]
            dp_row = dp_t[s][h:h + 1, :]
            c_ref[s, h] = c_prev * jnp.concatenate([dp_row] * (M_DV // LANES), axis=1) + _dot_tn(_bf(kw), vb)
            n_ref[s, h:h + 1, :] = n_prev * dp_row + jnp.sum(kw, axis=0, keepdims=True)
            h_ref[s, :, vs] = _bf(o_ref[s, :, vs].astype(F32) * (_rms(hc) * nrm_ref[:, vs]))


def _mixer_c_prompt(p_main, p_gate, i_bias, f_bias, norm_g):
    nc = SEQ // CHUNK

    def rowspec(width, cb):
        return pl.BlockSpec((CB, CHUNK, width), lambda b, c: (b, c, cb))

    def full(a):
        return pl.BlockSpec(a.shape, lambda b, c: (0,) * a.ndim)

    params = [i_bias, f_bias, norm_g]
    return pl.pallas_call(
        _c_prompt_body,
        grid=(BATCH // CB, nc),
        in_specs=[rowspec(M_QK, 0), rowspec(M_QK, 1), rowspec(M_V, 1), rowspec(M_V, 2), rowspec(2 * LANES, 0)]
        + [full(a) for a in params],
        out_specs=[
            pl.BlockSpec((CB, CHUNK, M_V), lambda b, c: (b, c, 0)),
            pl.BlockSpec((CB, M_HEADS, M_DK, M_DV), lambda b, c: (b, 0, 0, 0)),
            pl.BlockSpec((CB, M_HEADS, M_DK), lambda b, c: (b, 0, 0)),
            pl.BlockSpec((CB, SUBLANES, LANES), lambda b, c: (b, 0, 0)),
        ],
        out_shape=[
            jax.ShapeDtypeStruct((BATCH, SEQ, M_V), BF16),
            jax.ShapeDtypeStruct((BATCH, M_HEADS, M_DK, M_DV), F32),
            jax.ShapeDtypeStruct((BATCH, M_HEADS, M_DK), F32),
            jax.ShapeDtypeStruct((BATCH, SUBLANES, LANES), F32),
        ],
        scratch_shapes=[pltpu.VMEM((CB, CHUNK, LANES), F32)],
        compiler_params=_cp(("parallel", "arbitrary"), 48),
        name="mixer_c_prompt",
    )(p_main, p_main, p_main, p_main, p_gate, *params)


def _ab_sample_prep_body(xbc_ref, rq_ref, rk_ref, dt_ref, cst_ref, cos_ref, sin_ref,
                         cw_ref, cb_ref, dtb_ref, alog_ref,
                         conv_ref, xs_ref, xdt_ref, eda_ref, bm_ref, cm_ref, q_ref, k_ref, gam_ref):
    ch = SSD_CONV_CH
    u = xbc_ref[...]
    w = cw_ref[...]
    b0 = cst_ref[:, 0:ch]
    b1 = cst_ref[:, ch:2 * ch]
    b2 = cst_ref[:, 2 * ch:3 * ch]
    conv = cb_ref[...] + (((b0 * w[0:1, :] + b1 * w[1:2, :]) + b2 * w[2:3, :]) + u * w[3:4, :])
    conv_ref[:, 0:ch] = b1
    conv_ref[:, ch:2 * ch] = b2
    conv_ref[:, 2 * ch:3 * ch] = u
    xbc = _silu(conv)
    xs = xbc[:, :SSD_D_INNER]
    xs_ref[...] = xs
    bm_ref[...] = xbc[:, SSD_D_INNER:SSD_D_INNER + GROUP_W]
    cm_ref[...] = xbc[:, SSD_D_INNER + GROUP_W:]
    dt = _softplus(dt_ref[...] + dtb_ref[...])
    eda = jnp.exp(dt * (-jnp.exp(alog_ref[...])))
    xdt_ref[...] = xs * _pair_expand(dt, SSD_HEADS)
    eda_ref[...] = _pair_expand(eda, SSD_HEADS)
    cos = cos_ref[...]
    sin = sin_ref[...]
    for h in range(RET_HEADS):
        a = slice(h * RET_DK, h * RET_DK + LANES)
        b = slice(h * RET_DK + LANES, (h + 1) * RET_DK)
        q1, q2 = rq_ref[:, a], rq_ref[:, b]
        k1, k2 = rk_ref[:, a], rk_ref[:, b]
        q_ref[:, a] = q1 * cos - q2 * sin
        q_ref[:, b] = q1 * sin + q2 * cos
        k_ref[:, a] = (k1 * cos - k2 * sin) * (RET_DK ** -0.5)
        k_ref[:, b] = (k1 * sin + k2 * cos) * (RET_DK ** -0.5)
        gam_ref[:, h * RET_DK:(h + 1) * RET_DK] = jnp.full((DEC_BATCH, RET_DK), math.exp(_ret_log_gamma(h)), F32)


def _ab_sample_prep(p_main, p_dt, conv_state, cos, sin, conv_w, conv_b, dt_bias, a_log):
    n = DEC_BATCH

    def colspec(width, cb):
        return pl.BlockSpec((n, width), lambda i: (0, cb))

    def full(a):
        return pl.BlockSpec(a.shape, lambda i: (0,) * a.ndim)

    small = [conv_state, cos, sin, conv_w, conv_b, dt_bias, a_log]

    def out(width):
        return jax.ShapeDtypeStruct((n, width), F32)

    widths = [(SSD_CONV - 1) * SSD_CONV_CH, SSD_D_INNER, SSD_D_INNER, SSD_D_INNER, GROUP_W, GROUP_W,
              RET_QK, RET_QK, RET_QK]
    return pl.pallas_call(
        _ab_sample_prep_body,
        grid=(1,),
        in_specs=[colspec(SSD_CONV_CH, 2), colspec(RET_QK, 9), colspec(RET_QK, 10), full(p_dt)]
        + [full(a) for a in small],
        out_specs=[pl.BlockSpec((n, wd), lambda i: (0, 0)) for wd in widths],
        out_shape=[out(wd) for wd in widths],
        compiler_params=_cp(("arbitrary",), 48),
        name="ab_sample_prep",
    )(p_main, p_main, p_main, p_dt, *small)


def _ssd_state_body(eda_ref, xdt_ref, bm_ref, cm_ref, h_ref, ho_ref, y_ref):
    eda_t = _pad_t(eda_ref[...])
    xdt_t = _pad_t(xdt_ref[...])
    cm = _bf(cm_ref[...])
    for b in range(SSD_SB):
        hn = h_ref[b] * eda_t[:, b:b + 1] + xdt_t[:, b:b + 1] * bm_ref[b:b + 1, :]
        ho_ref[b] = hn
        y_ref[b:b + 1, :] = _dot_nt(cm, _bf(hn))[b:b + 1, :]


def _ssd_state(eda, xdt, bm, cm, state):
    vec = pl.BlockSpec((SSD_SB, GROUP_W), lambda i, g: (i, g))
    bc = pl.BlockSpec((SSD_SB, SSD_D_STATE), lambda i, g: (i, g))
    st = pl.BlockSpec((SSD_SB, GROUP_W, SSD_D_STATE), lambda i, g: (i, g, 0))
    return pl.pallas_call(
        _ssd_state_body,
        grid=(DEC_BATCH // SSD_SB, SSD_GROUPS),
        in_specs=[vec, vec, bc, bc, st],
        out_specs=[st, vec],
        out_shape=[jax.ShapeDtypeStruct(state.shape, F32), jax.ShapeDtypeStruct((DEC_BATCH, SSD_D_INNER), F32)],
        compiler_params=_cp(("parallel", "arbitrary"), 48),
        name="ssd_state",
    )(eda, xdt, bm, cm, state)


def _outer_state_body(d_ref, k_ref, q_ref, v_ref, s_ref, so_ref, o_ref):
    d_t = _pad_t(d_ref[...])
    k_t = _pad_t(k_ref[...])
    q_t = _pad_t(q_ref[...])
    for b in range(d_ref.shape[0]):
        sn = s_ref[b] * d_t[:, b:b + 1] + k_t[:, b:b + 1] * v_ref[b:b + 1, :]
        so_ref[b] = sn
        o_ref[b:b + 1, :] = jnp.sum(sn * q_t[:, b:b + 1], axis=0, keepdims=True)


def _outer_state(d, k, q, v, state):
    _, nh, dk, dv = state.shape
    sb = STATE_BLOCK_BYTES // (dk * dv * 4)
    kv = pl.BlockSpec((sb, dk), lambda i, h: (i, h))
    vv = pl.BlockSpec((sb, dv), lambda i, h: (i, h))
    st = pl.BlockSpec((sb, None, dk, dv), lambda i, h: (i, h, 0, 0))
    return pl.pallas_call(
        _outer_state_body,
        grid=(DEC_BATCH // sb, nh),
        in_specs=[kv, kv, kv, vv, st],
        out_specs=[st, vv],
        out_shape=[jax.ShapeDtypeStruct(state.shape, F32), jax.ShapeDtypeStruct((DEC_BATCH, nh * dv), F32)],
        compiler_params=_cp(("parallel", "arbitrary"), 48),
        name="outer_state",
    )(d, k, q, v, state)


def _ab_sample_post_body(y_ref, xs_ref, z_ref, r_ref, rg_ref, dsk_ref, nrm_ref, o_ref):
    for g in range(SSD_GROUPS):
        gs = slice(g * GROUP_W, (g + 1) * GROUP_W)
        yg = (y_ref[:, gs] + xs_ref[:, gs] * dsk_ref[:, gs]) * _silu(z_ref[:, gs])
        o_ref[:, gs] = _bf(_rms(yg) * nrm_ref[:, gs])
    for h in range(RET_HEADS):
        vs = slice(h * RET_DV, (h + 1) * RET_DV)
        os = slice(SSD_D_INNER + h * RET_DV, SSD_D_INNER + (h + 1) * RET_DV)
        o_ref[:, os] = _bf(_silu(rg_ref[:, vs]) * _rms(r_ref[:, vs]))


def _ab_sample_post(y, xs, p_main, r, d_skip_e, ssd_norm):
    n = DEC_BATCH

    def full(a):
        return pl.BlockSpec(a.shape, lambda i: (0,) * a.ndim)

    return pl.pallas_call(
        _ab_sample_post_body,
        grid=(1,),
        in_specs=[full(y), full(xs), pl.BlockSpec((n, SSD_D_INNER), lambda i: (0, 0)), full(r),
                  pl.BlockSpec((n, RET_V), lambda i: (0, 2)), full(d_skip_e), full(ssd_norm)],
        out_specs=pl.BlockSpec((n, SSD_D_INNER + RET_V), lambda i: (0, 0)),
        out_shape=jax.ShapeDtypeStruct((n, SSD_D_INNER + RET_V), BF16),
        compiler_params=_cp(("arbitrary",), 48),
        name="ab_sample_post",
    )(y, xs, p_main, r, p_main, d_skip_e, ssd_norm)


def _c_sample_prep_body(q_ref, k_ref, gt_ref, n_ref, m_ref, ib_ref, fb_ref,
                        dpe_ref, kw_ref, nn_ref, mn_ref, dn_ref):
    ipre = gt_ref[:, :LANES] + ib_ref[...]
    lf = -_softplus(-(gt_ref[:, LANES:] + fb_ref[...]))
    m_prev = m_ref[...]
    mt = jnp.maximum(lf + m_prev, ipre)
    wgt = jnp.exp(ipre - mt)
    dp = jnp.exp(lf + m_prev - mt)
    emt = jnp.exp(-mt)
    mn_ref[...] = mt
    for h in range(M_HEADS):
        ks = slice(h * M_DK, (h + 1) * M_DK)
        dpe = _lane_bcast(dp, h)
        kw = k_ref[:, ks] * (M_DK ** -0.5) * _lane_bcast(wgt, h)
        nn = n_ref[:, ks] * dpe + kw
        den = jnp.sum(nn * q_ref[:, ks], axis=1, keepdims=True)
        dpe_ref[:, ks] = dpe
        kw_ref[:, ks] = kw
        nn_ref[:, ks] = nn
        dn_ref[:, h * M_DV:(h + 1) * M_DV] = jnp.broadcast_to(
            jnp.maximum(jnp.abs(den), emt[:, h:h + 1]), (DEC_BATCH, M_DV))


def _c_sample_prep(p_main, p_gate, n_state, m_state, i_bias, f_bias):
    n = DEC_BATCH

    def full(a):
        return pl.BlockSpec(a.shape, lambda i: (0,) * a.ndim)

    widths = [M_QK, M_QK, M_QK, LANES, M_V]
    return pl.pallas_call(
        _c_sample_prep_body,
        grid=(1,),
        in_specs=[pl.BlockSpec((n, M_QK), lambda i: (0, 0)), pl.BlockSpec((n, M_QK), lambda i: (0, 1)),
                  full(p_gate), full(n_state), full(m_state), full(i_bias), full(f_bias)],
        out_specs=[pl.BlockSpec((n, wd), lambda i: (0, 0)) for wd in widths],
        out_shape=[jax.ShapeDtypeStruct((n, wd), F32) for wd in widths],
        compiler_params=_cp(("arbitrary",), 48),
        name="c_sample_prep",
    )(p_main, p_main, p_gate, n_state, m_state, i_bias, f_bias)


def _c_sample_post_body(num_ref, dn_ref, o_ref, nrm_ref, h_ref):
    for h in range(M_HEADS):
        vs = slice(h * M_DV, (h + 1) * M_DV)
        hc = num_ref[:, vs] / dn_ref[:, vs]
        h_ref[:, vs] = _bf(jax.nn.sigmoid(o_ref[:, vs]) * (_rms(hc) * nrm_ref[:, vs]))


def _c_sample_post(num, den, p_main, norm_g):
    n = DEC_BATCH

    def full(a):
        return pl.BlockSpec(a.shape, lambda i: (0,) * a.ndim)

    return pl.pallas_call(
        _c_sample_post_body,
        grid=(1,),
        in_specs=[full(num), full(den), pl.BlockSpec((n, M_V), lambda i: (0, 2)), full(norm_g)],
        out_specs=pl.BlockSpec((n, M_V), lambda i: (0, 0)),
        out_shape=jax.ShapeDtypeStruct((n, M_V), BF16),
        compiler_params=_cp(("arbitrary",), 48),
        name="c_sample_post",
    )(num, den, p_main, norm_g)


def _rope_tables(pos):
    half = RET_DK // 2
    inv = jnp.exp(-math.log(ROPE_BASE) * jnp.arange(half, dtype=F32) / half)
    ang = pos.astype(F32)[:, None] * inv
    return jnp.cos(ang), jnp.sin(ang)


def _pad_lanes(v, width=LANES):
    return jnp.pad(v.reshape(1, -1), ((0, 0), (0, width - v.size)))


def kernel(x_prompt, x_sample, cache_mem_k, cache_mem_v, state_conv, state_ssm, state_ret, state_mlstm_c, state_mlstm_n, state_mlstm_m, mem_prompt, norm_ffn1, w_ffn1_in, w_ffn1_out, norm_mix, w_in_ab, ssd_conv_w, ssd_conv_b, ssd_dt_bias, ssd_a_log, ssd_d, ssd_norm, w_out_ab, w_in_c, mlstm_i_bias, mlstm_f_bias, mlstm_norm, w_out_c, norm_xattn, norm_mem, w_xq, w_xkv, w_xo, norm_ffn2, w_ffn2_in, w_ffn2_out, norm_final):
    g3 = lambda g: g.reshape(DEPTH, 1, D_MODEL)
    n_ffn1, n_mix, n_x, n_mem, n_ffn2 = g3(norm_ffn1), g3(norm_mix), g3(norm_xattn), g3(norm_mem), g3(norm_ffn2)

    f1_in, f1_out, f2_in, f2_out = (w.astype(BF16) for w in (w_ffn1_in, w_ffn1_out, w_ffn2_in, w_ffn2_out))
    w_out_ab, w_out_c, w_xq, w_xkv, w_xo = (w.astype(BF16) for w in (w_out_ab, w_out_c, w_xq, w_xkv, w_xo))

    wz, wxbc, wdt, wrq, wrk, wrv, wrg = jnp.split(w_in_ab[0], np_cumsum(AB_SIZES), axis=1)
    w_ab_main = jnp.concatenate([wz, wrv, wrg, wxbc, wrq, wrk], axis=1).astype(BF16)[None]
    w_ab_dt = jnp.pad(wdt, ((0, 0), (0, LANES - SSD_HEADS))).astype(BF16)[None]
    wq, wk, wv, wi, wf, wo = jnp.split(w_in_c[0], np_cumsum(C_SIZES), axis=1)
    w_c_main = jnp.concatenate([wq, wk, wv, wo], axis=1).astype(BF16)[None]
    gpad = ((0, 0), (0, LANES - M_HEADS))
    w_c_gate = jnp.concatenate([jnp.pad(wi, gpad), jnp.pad(wf, gpad)], axis=1).astype(BF16)[None]

    conv_w = ssd_conv_w[0]
    conv_b = ssd_conv_b.reshape(1, SSD_CONV_CH)
    dt_bias = _pad_lanes(ssd_dt_bias[0])
    a_log = _pad_lanes(ssd_a_log[0])
    d_skip_e = jnp.repeat(ssd_d[0], SSD_HEAD_DIM).reshape(1, SSD_D_INNER)
    s_norm = ssd_norm.reshape(1, SSD_D_INNER)
    i_bias = _pad_lanes(mlstm_i_bias[0])
    f_bias = _pad_lanes(mlstm_f_bias[0])
    m_norm = mlstm_norm.reshape(1, M_V)

    memk, memv = _mem_kv(mem_prompt.reshape(BATCH * MEM_LEN, D_MODEL), n_mem, w_xkv)
    cos_p, sin_p = _rope_tables(jnp.arange(SEQ))
    x = x_prompt.reshape(BATCH * SEQ, D_MODEL)
    x = _ffn(x, n_ffn1, f1_in, f1_out, 0)
    w_gates = w_ab_main[0, :, :AB_GATE_BLOCKS * AB_TN]
    w_xqk = w_ab_main[0, :, AB_GATE_BLOCKS * AB_TN:]
    p_gates, p_xqk, p_dt = _ab_inproj(x, n_mix, w_gates, w_xqk, w_ab_dt[0], dt_bias, conv_w, conv_b, cos_p, sin_p)
    ycat, ssm_p, ret_p = _mixer_ab_prompt(p_gates, p_xqk, p_dt, a_log, d_skip_e, s_norm)
    x_tail = x.reshape(BATCH, SEQ, D_MODEL)[:, SEQ - SSD_CONV:].reshape(BATCH * SSD_CONV, D_MODEL)
    conv_p = _norm_proj(x_tail, n_mix, 0, w_ab_main, 0, AB_TN, col0=AB_XBC0, n_out=SSD_CONV_CH)
    conv_p = conv_p.reshape(BATCH, SSD_CONV, SSD_CONV_CH)[:, 1:]
    x = _proj_residual(x, ycat, w_out_ab, 0)
    x = _xattn_prompt(x, n_x, w_xq, w_xo, memk, memv, 0)
    x = _ffn(x, n_ffn2, f2_in, f2_out, 0)
    x = _ffn(x, n_ffn1, f1_in, f1_out, 1)
    pc_main, pc_gate = _c_inproj(x, n_mix, 1, w_c_main, w_c_gate)
    hout, mc_p, mn_p, mm_p = _mixer_c_prompt(pc_main.reshape(BATCH, SEQ, -1), pc_gate.reshape(BATCH, SEQ, -1),
                                             i_bias, f_bias, m_norm)
    x = _proj_residual(x, hout.reshape(BATCH * SEQ, M_V), w_out_c, 0)
    x = _xattn_prompt(x, n_x, w_xq, w_xo, memk, memv, 1)
    y_prompt = _ffn(x, n_ffn2, f2_in, f2_out, 1, norm_final).reshape(BATCH, SEQ, D_MODEL)

    cos_s, sin_s = _rope_tables(PAST_LEN + jnp.arange(1))

    def xattn_s(xs_, layer):
        q = _norm_proj(xs_, n_x, layer, w_xq, layer, 1024).reshape(DEC_BATCH, X_HEADS, X_HEAD_DIM)
        o = _xattn_sample(q, cache_mem_k, cache_mem_v, layer).reshape(DEC_BATCH, D_MODEL)
        return _proj_residual(xs_, o, w_xo, layer)

    xs_ = x_sample.reshape(DEC_BATCH, D_MODEL)
    xs_ = _ffn(xs_, n_ffn1, f1_in, f1_out, 0)
    sp_main, sp_dt = _norm_proj(xs_, n_mix, 0, w_ab_main, 0, 1024, w_ab_dt)
    conv_s, xs_c, xdt, eda, bm_s, cm_s, q_s, k_s, gam = _ab_sample_prep(
        sp_main, sp_dt, state_conv.reshape(DEC_BATCH, (SSD_CONV - 1) * SSD_CONV_CH), cos_s, sin_s,
        conv_w, conv_b, dt_bias, a_log)
    ssm_s, y_s = _ssd_state(eda, xdt, bm_s, cm_s, state_ssm.reshape(DEC_BATCH, SSD_D_INNER, SSD_D_STATE))
    ret_s, r_s = _outer_state(gam, k_s, q_s, sp_main[:, SSD_D_INNER:SSD_D_INNER + RET_V], state_ret[0])
    ycat_s = _ab_sample_post(y_s, xs_c, sp_main, r_s, d_skip_e, s_norm)
    xs_ = _proj_residual(xs_, ycat_s, w_out_ab, 0)
    xs_ = xattn_s(xs_, 0)
    xs_ = _ffn(xs_, n_ffn2, f2_in, f2_out, 0)
    xs_ = _ffn(xs_, n_ffn1, f1_in, f1_out, 1)
    sc_main, sc_gate = _norm_proj(xs_, n_mix, 1, w_c_main, 0, 1024, w_c_gate)
    m_in = jnp.pad(state_mlstm_m[0], ((0, 0), (0, LANES - M_HEADS)))
    dpe, kw, mn_s, mm_s, den = _c_sample_prep(sc_main, sc_gate, state_mlstm_n.reshape(DEC_BATCH, M_QK), m_in,
                                              i_bias, f_bias)
    mc_s, num = _outer_state(dpe, kw, sc_main[:, :M_QK], sc_main[:, 2 * M_QK:2 * M_QK + M_V], state_mlstm_c[0])
    hout_s = _c_sample_post(num, den, sc_main, m_norm)
    xs_ = _proj_residual(xs_, hout_s, w_out_c, 0)
    xs_ = xattn_s(xs_, 1)
    y_sample = _ffn(xs_, n_ffn2, f2_in, f2_out, 1, norm_final).reshape(DEC_BATCH, 1, D_MODEL)

    kv_shape = (DEPTH, BATCH, MEM_LEN, X_HEADS, X_HEAD_DIM)
    return (y_prompt, y_sample, memk.reshape(kv_shape), memv.reshape(kv_shape),
            conv_p.reshape(1, BATCH, SSD_CONV - 1, SSD_CONV_CH),
            conv_s.reshape(1, DEC_BATCH, SSD_CONV - 1, SSD_CONV_CH),
            ssm_p.reshape(1, BATCH, SSD_HEADS, SSD_HEAD_DIM, SSD_D_STATE),
            ssm_s.reshape(1, DEC_BATCH, SSD_HEADS, SSD_HEAD_DIM, SSD_D_STATE),
            ret_p[None], ret_s[None], mc_p[None], mc_s[None],
            mn_p[None], mn_s.reshape(1, DEC_BATCH, M_HEADS, M_DK),
            mm_p[:, :M_HEADS, 0][None], mm_s[:, :M_HEADS][None])


def np_cumsum(sizes):
    out, acc = [], 0
    for s in sizes[:-1]:
        acc += s
        out.append(acc)
    return out
```

```python
import functools
import math

import jax
import jax.numpy as jnp
from jax import lax
from jax.experimental import pallas as pl
from jax.experimental.pallas import tpu as pltpu

F32 = jnp.float32
BF16 = jnp.bfloat16
EPS = 1e-6

D_MODEL = 1024
BATCH = 8
SEQ = 2048
DEPTH = 2
DEC_BATCH = 128
PAST_LEN = 16384
CHUNK = 128
D_FF = 2816
SSD_D_INNER = 2 * D_MODEL
SSD_HEAD_DIM = 64
SSD_HEADS = SSD_D_INNER // SSD_HEAD_DIM
SSD_GROUPS = 4
SSD_D_STATE = 128
SSD_CONV = 4
SSD_CONV_CH = SSD_D_INNER + 2 * SSD_GROUPS * SSD_D_STATE
RET_HEADS = 4
RET_QK = D_MODEL
RET_V = 2 * D_MODEL
RET_DK = RET_QK // RET_HEADS
RET_DV = RET_V // RET_HEADS
ROPE_BASE = 10000.0
AB_SIZES = (SSD_D_INNER, SSD_CONV_CH, SSD_HEADS, RET_QK, RET_QK, RET_V, RET_V)
M_HEADS = 4
M_QK = D_MODEL // 2
M_V = D_MODEL
M_DK = M_QK // M_HEADS
M_DV = M_V // M_HEADS
C_SIZES = (M_QK, M_QK, M_V, M_HEADS, M_HEADS, M_V)
MEM_LEN = 256
X_HEADS = 4
X_HEAD_DIM = D_MODEL // X_HEADS

LANES = 128
SUBLANES = 8
GROUP_W = SSD_D_INNER // SSD_GROUPS
HEADS_PER_GROUP = SSD_HEADS // SSD_GROUPS
ROW_TILE = 1024
FF_TILE = 256
SSD_SB = 16
STATE_BLOCK_BYTES = 8 * 1024 * 1024
XB = 4
CB = 2


def _cp(sem, mib):
    return pltpu.CompilerParams(dimension_semantics=sem, vmem_limit_bytes=mib * 1024 * 1024)


def _bf(x):
    return x.astype(BF16)


def _dot(a, b):
    return jnp.dot(a, b, preferred_element_type=F32)


def _dot_nt(a, b):
    return lax.dot_general(a, b, (((1,), (1,)), ((), ())), preferred_element_type=F32)


def _dot_tn(a, b):
    return lax.dot_general(a, b, (((0,), (0,)), ((), ())), preferred_element_type=F32)


def _rms(x):
    return x * lax.rsqrt(jnp.mean(x * x, axis=-1, keepdims=True) + EPS)


def _silu(x):
    return x * jax.nn.sigmoid(x)


def _softplus(x):
    return jnp.maximum(x, 0.0) + jnp.log1p(jnp.exp(-jnp.abs(x)))


def _split3(x):
    hi = x.astype(BF16)
    r = x - hi.astype(F32)
    mid = r.astype(BF16)
    lo = (r - mid.astype(F32)).astype(BF16)
    return hi, mid, lo


def _cumsum_rows(x):
    n = x.shape[0]
    r = lax.broadcasted_iota(jnp.int32, (n, n), 0)
    c = lax.broadcasted_iota(jnp.int32, (n, n), 1)
    t = jnp.where(r >= c, 1.0, 0.0).astype(BF16)
    hi, mid, lo = _split3(x)
    return _dot(t, hi) + _dot(t, mid) + _dot(t, lo)


def _lane_bcast(x, h, width=LANES):
    return jnp.broadcast_to(x[:, h:h + 1], (x.shape[0], width))


def _pair_expand(x, n_heads):
    rows = x.shape[0]
    lo = lax.broadcasted_iota(jnp.int32, (rows, LANES), 1) < SSD_HEAD_DIM
    return jnp.concatenate(
        [jnp.where(lo, _lane_bcast(x, 2 * j), _lane_bcast(x, 2 * j + 1)) for j in range(n_heads // 2)], axis=1)


def _pad_t(x):
    pad = jnp.zeros((LANES - x.shape[0], x.shape[1]), F32)
    return jnp.concatenate([x, pad], axis=0).T


def _ffn_body(*refs, nf, final, hosted):
    x_ref, g_ref, wi_ref, wo_ref = refs[:4]
    rest = list(refs[4:])
    fg_ref = rest.pop(0) if final else None
    if hosted:
        q_ref, k_ref, v_ref = rest[:3]
        o_ref, xo_ref = rest[3:]
        _xattn_sample_body(q_ref, k_ref, v_ref, xo_ref)
    else:
        o_ref, = rest
    x = x_ref[...]
    xn = _bf(_rms(x) * g_ref[...])
    acc = None
    for f in range(nf):
        fs = slice(f * FF_TILE, (f + 1) * FF_TILE)
        g = _dot(xn, wi_ref[:, fs])
        u = _dot(xn, wi_ref[:, D_FF + f * FF_TILE:D_FF + (f + 1) * FF_TILE])
        t = _dot(_bf(_silu(g) * u), wo_ref[fs, :])
        acc = t if acc is None else acc + t
    y = x + 0.5 * acc
    if final:
        y = _rms(y) * fg_ref[...]
    o_ref[...] = y


def _ffn(x, g3, w_in, w_out, layer, final_g=None, host=None):
    m = x.shape[0]
    tm = min(m, ROW_TILE) if host is None else m // (DEC_BATCH // XB)
    nf = D_FF // FF_TILE
    once = pl.Buffered(1)
    in_specs = [
        pl.BlockSpec((tm, D_MODEL), lambda i: (i, 0)),
        pl.BlockSpec((None, 1, D_MODEL), lambda i: (layer, 0, 0)),
        pl.BlockSpec((None, D_MODEL, 2 * D_FF), lambda i: (layer, 0, 0), pipeline_mode=once),
        pl.BlockSpec((None, D_FF, D_MODEL), lambda i: (layer, 0, 0), pipeline_mode=once),
    ]
    args = [x, g3, w_in, w_out]
    if final_g is not None:
        in_specs.append(pl.BlockSpec((1, D_MODEL), lambda i: (0, 0)))
        args.append(final_g.reshape(1, D_MODEL))
    out_specs = [pl.BlockSpec((tm, D_MODEL), lambda i: (i, 0))]
    out_shape = [jax.ShapeDtypeStruct((m, D_MODEL), F32)]
    if host is not None:
        q, cache_k, cache_v, cl = host
        blk = pl.BlockSpec((None, XB, MEM_LEN, X_HEADS, X_HEAD_DIM), lambda i: (cl, i, 0, 0, 0))
        qo = pl.BlockSpec((XB, X_HEADS, X_HEAD_DIM), lambda i: (i, 0, 0))
        in_specs += [qo, blk, blk]
        args += [q, cache_k, cache_v]
        out_specs.append(qo)
        out_shape.append(jax.ShapeDtypeStruct((DEC_BATCH, X_HEADS, X_HEAD_DIM), F32))
    res = pl.pallas_call(
        functools.partial(_ffn_body, nf=nf, final=final_g is not None, hosted=host is not None),
        grid=(m // tm,),
        in_specs=in_specs,
        out_specs=out_specs,
        out_shape=out_shape,
        compiler_params=_cp(("parallel",), 56),
        name="ffn",
    )(*args)
    return res if host is not None else res[0]


def _norm_proj_body(*refs, small):
    if small:
        x_ref, g_ref, w_ref, ws_ref, o_ref, os_ref, xn_ref = refs
    else:
        x_ref, g_ref, w_ref, o_ref, xn_ref = refs
    n = pl.program_id(1)

    @pl.when(n == 0)
    def _():
        xn = _bf(_rms(x_ref[...]) * g_ref[...])
        xn_ref[...] = xn
        if small:
            os_ref[...] = _dot(xn, _bf(ws_ref[...]))

    o_ref[...] = _dot(xn_ref[...], _bf(w_ref[...]))


def _norm_proj(x, g3, glayer, w3, wlayer, tn, w_small=None, col0=0, n_out=None):
    m = x.shape[0]
    tm = min(m, ROW_TILE)
    n_out = w3.shape[-1] if n_out is None else n_out
    in_specs = [
        pl.BlockSpec((tm, D_MODEL), lambda i, n: (i, 0)),
        pl.BlockSpec((None, 1, D_MODEL), lambda i, n: (glayer, 0, 0)),
        pl.BlockSpec((None, D_MODEL, tn), lambda i, n: (wlayer, 0, n + col0)),
    ]
    args = [x, g3, w3]
    out_specs = [pl.BlockSpec((tm, tn), lambda i, n: (i, n))]
    out_shape = [jax.ShapeDtypeStruct((m, n_out), F32)]
    if w_small is not None:
        ns = w_small.shape[-1]
        in_specs.append(pl.BlockSpec((None, D_MODEL, ns), lambda i, n: (0, 0, 0)))
        args.append(w_small)
        out_specs.append(pl.BlockSpec((tm, ns), lambda i, n: (i, 0)))
        out_shape.append(jax.ShapeDtypeStruct((m, ns), F32))
    res = pl.pallas_call(
        functools.partial(_norm_proj_body, small=w_small is not None),
        grid=(m // tm, n_out // tn),
        in_specs=in_specs,
        out_specs=out_specs,
        out_shape=out_shape,
        scratch_shapes=[pltpu.VMEM((tm, D_MODEL), BF16)],
        compiler_params=_cp(("parallel", "arbitrary"), 48),
        name="norm_proj",
    )(*args)
    return res if w_small is not None else res[0]


AB_TN = 1024
AB_XBC0 = (SSD_D_INNER + 2 * RET_V) // AB_TN
AB_RQ = AB_XBC0 + SSD_CONV_CH // AB_TN


AB_GATE_BLOCKS = AB_XBC0
AB_XQK_BLOCKS = SSD_CONV_CH // AB_TN + 2 * RET_QK // AB_TN
AB_ROW_TILE = 512


def _ab_gates_body(x_ref, g_ref, w_ref, o_ref):
    xn = _bf(_rms(x_ref[...]) * g_ref[...])
    nz = SSD_D_INNER // AB_TN
    nv = RET_V // AB_TN
    for n in range(AB_GATE_BLOCKS):
        cs = slice(n * AB_TN, (n + 1) * AB_TN)
        r = _dot(xn, w_ref[:, cs])
        o_ref[:, cs] = _bf(r) if nz <= n < nz + nv else _bf(_silu(r))


def _ab_xqk_body(x_ref, g_ref, w_ref, ws_ref, dtb_ref, cw_ref, cb_ref, cos_ref, sin_ref,
                 o_ref, dt_ref, xpad, rbuf, ybuf, carry, *, tiles_per_seq):
    i = pl.program_id(0)
    tm = x_ref.shape[0]
    nslab = tm // SUBLANES
    pitch = nslab + SUBLANES
    nxb = SSD_CONV_CH // AB_TN
    ntap = SSD_CONV - 1

    @pl.when(i == 0)
    def _():
        carry[...] = jnp.zeros_like(carry)

    g = g_ref[...]
    xn = _bf(_rms(x_ref[...]) * g)
    dt_ref[...] = _softplus(_dot(xn, ws_ref[...]) + dtb_ref[...])
    for lb in range(D_MODEL // LANES):
        for s in range(SUBLANES):
            xpad[lb, s * pitch:s * pitch + nslab, :] = x_ref[s * nslab:(s + 1) * nslab, lb * LANES:(lb + 1) * LANES]
    xp = jnp.concatenate(
        [jnp.concatenate([xpad[lb, pl.ds(v, SUBLANES, stride=pitch), :] for lb in range(D_MODEL // LANES)], axis=1)
         for v in range(nslab)], axis=0)
    xnp = _bf(_rms(xp) * g)
    seq_start = i % tiles_per_seq == 0
    first = lax.broadcasted_iota(jnp.int32, (SUBLANES, AB_TN), 0) == 0
    halo = ntap * SUBLANES
    for n in range(nxb):
        cs = slice(n * AB_TN, (n + 1) * AB_TN)
        rbuf[n, halo:halo + tm, :] = _dot(xnp, w_ref[:, cs])
        prev_rows = jnp.where(seq_start, 0.0, carry[n])
        for k in range(ntap):
            hi = rbuf[n, tm + k * SUBLANES:tm + (k + 1) * SUBLANES, :]
            rbuf[n, k * SUBLANES:(k + 1) * SUBLANES, :] = jnp.where(first, prev_rows[k:k + 1, :],
                                                                     pltpu.roll(hi, 1, axis=0))
            carry[n, k:k + 1, :] = hi[SUBLANES - 1:SUBLANES, :]
        w = cw_ref[:, cs]
        conv = cb_ref[:, cs] + rbuf[n, halo:halo + tm, :] * w[ntap:ntap + 1, :]
        for j in range(1, SSD_CONV):
            off = halo - j * SUBLANES
            conv = conv + rbuf[n, off:off + tm, :] * w[ntap - j:ntap - j + 1, :]
        y = _silu(conv)
        for lb in range(AB_TN // LANES):
            ls = slice(lb * LANES, (lb + 1) * LANES)
            for v in range(nslab):
                ybuf[n, lb, pl.ds(v, SUBLANES, stride=pitch), :] = y[v * SUBLANES:(v + 1) * SUBLANES, ls]
            for s in range(SUBLANES):
                o_ref[s * nslab:(s + 1) * nslab, n * AB_TN + ls.start:n * AB_TN + ls.stop] = _bf(
                    ybuf[n, lb, s * pitch:s * pitch + nslab, :])
    cos = cos_ref[...]
    sin = sin_ref[...]
    for n in range(nxb, AB_XQK_BLOCKS):
        r = _dot(xn, w_ref[:, n * AB_TN:(n + 1) * AB_TN])
        scale = 1.0 if n == nxb else RET_DK ** -0.5
        for h in range(AB_TN // RET_DK):
            a = slice(h * RET_DK, h * RET_DK + LANES)
            b = slice(h * RET_DK + LANES, (h + 1) * RET_DK)
            x1, x2 = r[:, a], r[:, b]
            o_ref[:, n * AB_TN + a.start:n * AB_TN + a.stop] = _bf((x1 * cos - x2 * sin) * scale)
            o_ref[:, n * AB_TN + b.start:n * AB_TN + b.stop] = _bf((x1 * sin + x2 * cos) * scale)


def _ab_inproj(x, g3, w_gates, w_xqk, w_dt, dt_bias, conv_w, conv_b, cos, sin):
    m = x.shape[0]
    tm = AB_ROW_TILE
    tps = SEQ // tm
    nxb = SSD_CONV_CH // AB_TN
    once = pl.Buffered(1)

    def full(a):
        return pl.BlockSpec(a.shape, lambda i: (0,) * a.ndim, pipeline_mode=once)

    xspec = pl.BlockSpec((tm, D_MODEL), lambda i: (i, 0))
    gspec = pl.BlockSpec((None, 1, D_MODEL), lambda i: (0, 0, 0))
    gates = pl.pallas_call(
        _ab_gates_body,
        grid=(m // tm,),
        in_specs=[xspec, gspec, full(w_gates)],
        out_specs=pl.BlockSpec((tm, AB_GATE_BLOCKS * AB_TN), lambda i: (i, 0)),
        out_shape=jax.ShapeDtypeStruct((m, AB_GATE_BLOCKS * AB_TN), BF16),
        compiler_params=_cp(("parallel",), 56),
        name="ab_gates",
    )(x, g3, w_gates)
    xqk, dt = pl.pallas_call(
        functools.partial(_ab_xqk_body, tiles_per_seq=tps),
        grid=(m // tm,),
        in_specs=[xspec, gspec, full(w_xqk), full(w_dt), full(dt_bias), full(conv_w), full(conv_b),
                  pl.BlockSpec((tm, LANES), lambda i: (i % tps, 0)),
                  pl.BlockSpec((tm, LANES), lambda i: (i % tps, 0))],
        out_specs=[pl.BlockSpec((tm, AB_XQK_BLOCKS * AB_TN), lambda i: (i, 0)),
                   pl.BlockSpec((tm, LANES), lambda i: (i, 0))],
        out_shape=[jax.ShapeDtypeStruct((m, AB_XQK_BLOCKS * AB_TN), BF16), jax.ShapeDtypeStruct((m, LANES), F32)],
        scratch_shapes=[pltpu.VMEM((D_MODEL // LANES, tm + SUBLANES * SUBLANES, LANES), F32),
                        pltpu.VMEM((nxb, (SSD_CONV - 1) * SUBLANES + tm, AB_TN), F32),
                        pltpu.VMEM((nxb, AB_TN // LANES, tm + SUBLANES * SUBLANES, LANES), F32),
                        pltpu.VMEM((nxb, SUBLANES, AB_TN), F32)],
        compiler_params=_cp(("arbitrary",), 56),
        name="ab_xqk",
    )(x, g3, w_xqk, w_dt, dt_bias, conv_w, conv_b, cos, sin)
    return gates, xqk, dt


def _c_inproj_body(x_ref, g_ref, w_ref, wg_ref, o_ref, gt_ref):
    xn = _bf(_rms(x_ref[...]) * g_ref[...])
    gt_ref[...] = _dot(xn, wg_ref[...])
    o_ref[:, :M_QK] = _bf(_dot(xn, w_ref[:, :M_QK]))
    o_ref[:, M_QK:2 * M_QK] = _bf(_dot(xn, w_ref[:, M_QK:2 * M_QK]) * (M_DK ** -0.5))
    o_ref[:, 2 * M_QK:2 * M_QK + M_V] = _bf(_dot(xn, w_ref[:, 2 * M_QK:2 * M_QK + M_V]))
    o_ref[:, 2 * M_QK + M_V:] = _bf(jax.nn.sigmoid(_dot(xn, w_ref[:, 2 * M_QK + M_V:])))


def _c_inproj(x, g3, glayer, w_main, w_gate):
    m = x.shape[0]
    tm = ROW_TILE
    once = pl.Buffered(1)
    n_out = w_main.shape[-1]
    n_gate = w_gate.shape[-1]
    return pl.pallas_call(
        _c_inproj_body,
        grid=(m // tm,),
        in_specs=[
            pl.BlockSpec((tm, D_MODEL), lambda i: (i, 0)),
            pl.BlockSpec((None, 1, D_MODEL), lambda i: (glayer, 0, 0)),
            pl.BlockSpec((None, D_MODEL, n_out), lambda i: (0, 0, 0), pipeline_mode=once),
            pl.BlockSpec((None, D_MODEL, n_gate), lambda i: (0, 0, 0), pipeline_mode=once),
        ],
        out_specs=[pl.BlockSpec((tm, n_out), lambda i: (i, 0)), pl.BlockSpec((tm, n_gate), lambda i: (i, 0))],
        out_shape=[jax.ShapeDtypeStruct((m, n_out), BF16), jax.ShapeDtypeStruct((m, n_gate), F32)],
        compiler_params=_cp(("parallel",), 48),
        name="c_inproj",
    )(x, g3, w_main, w_gate)


def _proj_res_body(x_ref, y_ref, w_ref, o_ref):
    o_ref[...] = x_ref[...] + _dot(_bf(y_ref[...]), w_ref[...])


def _proj_residual(x, y, w3, layer):
    m = x.shape[0]
    tm = min(m, ROW_TILE)
    kdim = y.shape[1]
    return pl.pallas_call(
        _proj_res_body,
        grid=(m // tm,),
        in_specs=[
            pl.BlockSpec((tm, D_MODEL), lambda i: (i, 0)),
            pl.BlockSpec((tm, kdim), lambda i: (i, 0)),
            pl.BlockSpec((None, kdim, D_MODEL), lambda i: (layer, 0, 0), pipeline_mode=pl.Buffered(1)),
        ],
        out_specs=pl.BlockSpec((tm, D_MODEL), lambda i: (i, 0)),
        out_shape=jax.ShapeDtypeStruct((m, D_MODEL), F32),
        compiler_params=_cp(("parallel",), 56),
        name="proj_residual",
    )(x, y, w3)


def _mem_kv_body(x_ref, g_ref, w_ref, k_ref, v_ref):
    xn = _bf(_rms(x_ref[...]) * g_ref[...])
    kv = _dot(xn, _bf(w_ref[...]))
    k_ref[...] = kv[:, :D_MODEL]
    v_ref[...] = kv[:, D_MODEL:]


def _mem_kv(mem2d, g3, w_xkv):
    m = mem2d.shape[0]
    tm = 512
    shp = jax.ShapeDtypeStruct((DEPTH, m, D_MODEL), F32)
    return pl.pallas_call(
        _mem_kv_body,
        grid=(DEPTH, m // tm),
        in_specs=[
            pl.BlockSpec((tm, D_MODEL), lambda l, i: (i, 0)),
            pl.BlockSpec((None, 1, D_MODEL), lambda l, i: (l, 0, 0)),
            pl.BlockSpec((None, D_MODEL, 2 * D_MODEL), lambda l, i: (l, 0, 0)),
        ],
        out_specs=[pl.BlockSpec((None, tm, D_MODEL), lambda l, i: (l, i, 0))] * 2,
        out_shape=[shp, shp],
        compiler_params=_cp(("arbitrary", "arbitrary"), 48),
        name="mem_kv",
    )(mem2d, g3, w_xkv)


def _xattn_prompt_body(x_ref, g_ref, wq_ref, wo_ref, k_ref, v_ref, o_ref):
    x = x_ref[...]
    xn = _bf(_rms(x) * g_ref[...])
    q = _dot(xn, _bf(wq_ref[...]))
    k = _bf(k_ref[...])
    v = _bf(v_ref[...])
    outs = []
    for h in range(X_HEADS):
        sl = slice(h * X_HEAD_DIM, (h + 1) * X_HEAD_DIM)
        s = _dot_nt(_bf(q[:, sl]), k[:, sl]) * (X_HEAD_DIM ** -0.5)
        e = jnp.exp(s - jnp.max(s, axis=-1, keepdims=True))
        p = e / jnp.sum(e, axis=-1, keepdims=True)
        outs.append(_bf(_dot(_bf(p), v[:, sl])))
    o_ref[...] = x + _dot(jnp.concatenate(outs, axis=1), _bf(wo_ref[...]))


def _xattn_prompt(x, g3, w_xq, w_xo, memk, memv, layer):
    tq = ROW_TILE
    nq = SEQ // tq
    return pl.pallas_call(
        _xattn_prompt_body,
        grid=(BATCH, nq),
        in_specs=[
            pl.BlockSpec((tq, D_MODEL), lambda b, j: (b * nq + j, 0)),
            pl.BlockSpec((None, 1, D_MODEL), lambda b, j: (layer, 0, 0)),
            pl.BlockSpec((None, D_MODEL, D_MODEL), lambda b, j: (layer, 0, 0)),
            pl.BlockSpec((None, D_MODEL, D_MODEL), lambda b, j: (layer, 0, 0)),
            pl.BlockSpec((None, MEM_LEN, D_MODEL), lambda b, j: (layer, b, 0)),
            pl.BlockSpec((None, MEM_LEN, D_MODEL), lambda b, j: (layer, b, 0)),
        ],
        out_specs=pl.BlockSpec((tq, D_MODEL), lambda b, j: (b * nq + j, 0)),
        out_shape=jax.ShapeDtypeStruct((BATCH * SEQ, D_MODEL), F32),
        compiler_params=_cp(("parallel", "arbitrary"), 48),
        name="xattn_prompt",
    )(x, g3, w_xq, w_xo, memk, memv)


def _xattn_sample_body(q_ref, k_ref, v_ref, o_ref):
    for b in range(XB):
        s = jnp.sum(k_ref[b] * (q_ref[b] * (X_HEAD_DIM ** -0.5))[None], axis=-1, keepdims=True)
        e = jnp.exp(s - jnp.max(s, axis=0, keepdims=True))
        o_ref[b] = jnp.sum(e * v_ref[b], axis=0) / jnp.sum(e, axis=0)


def _xattn_sample(q, cache_k, cache_v, layer):
    blk = pl.BlockSpec((None, XB, MEM_LEN, X_HEADS, X_HEAD_DIM), lambda i: (layer, i, 0, 0, 0))
    qo = pl.BlockSpec((XB, X_HEADS, X_HEAD_DIM), lambda i: (i, 0, 0))
    return pl.pallas_call(
        _xattn_sample_body,
        grid=(DEC_BATCH // XB,),
        in_specs=[qo, blk, blk],
        out_specs=qo,
        out_shape=jax.ShapeDtypeStruct((DEC_BATCH, X_HEADS, X_HEAD_DIM), F32),
        compiler_params=_cp(("parallel",), 48),
        name="xattn_sample",
    )(q, cache_k, cache_v)


def _ret_log_gamma(h):
    return math.log1p(-(2.0 ** (-5.0 - h)))


def _ab_prompt_body(z_ref, rv_ref, rg_ref, xbc_ref, rq_ref, rk_ref, dt_ref, alog_ref, dsk_ref, nrm_ref,
                    y_ref, h_ref, s_ref):
    c = pl.program_id(1)

    @pl.when(c == 0)
    def _():
        h_ref[...] = jnp.zeros_like(h_ref)
        s_ref[...] = jnp.zeros_like(s_ref)

    xs = xbc_ref[:, :SSD_D_INNER].astype(F32)
    bm = xbc_ref[:, SSD_D_INNER:SSD_D_INNER + GROUP_W]
    cm = xbc_ref[:, SSD_D_INNER + GROUP_W:]

    dt = dt_ref[...]
    da = dt * (-jnp.exp(alog_ref[...]))
    cs = _cumsum_rows(da)
    cs_t = cs.T
    row = lax.broadcasted_iota(jnp.int32, (CHUNK, CHUNK), 0)
    col = lax.broadcasted_iota(jnp.int32, (CHUNK, CHUNK), 1)
    tri = row >= col
    lo = col < SSD_HEAD_DIM
    for g in range(SSD_GROUPS):
        gs = slice(g * GROUP_W, (g + 1) * GROUP_W)
        ns = slice(g * SSD_D_STATE, (g + 1) * SSD_D_STATE)
        cmg = cm[:, ns]
        bmg = bm[:, ns]
        att = _dot_nt(cmg, bmg)
        hprev = h_ref[gs, :]
        yint = _dot_nt(cmg, _bf(hprev))
        ys, wxs, css = [], [], []
        for j in range(HEADS_PER_GROUP // 2):
            h0 = g * HEADS_PER_GROUP + 2 * j
            cb0 = _lane_bcast(cs, h0)
            cb1 = _lane_bcast(cs, h0 + 1)
            cs_p = jnp.where(lo, cb0, cb1)
            dt_p = jnp.where(lo, _lane_bcast(dt, h0), _lane_bcast(dt, h0 + 1))
            off = g * GROUP_W + j * LANES
            xdt = xs[:, off:off + LANES] * dt_p
            d0 = jnp.exp(jnp.where(tri, cb0 - cs_t[h0:h0 + 1, :], -jnp.inf))
            d1 = jnp.exp(jnp.where(tri, cb1 - cs_t[h0 + 1:h0 + 2, :], -jnp.inf))
            yy = _dot(jnp.concatenate([_bf(att * d0), _bf(att * d1)], axis=0), _bf(xdt))
            ys.append(jnp.where(lo, yy[:CHUNK], yy[CHUNK:]) + yint[:, j * LANES:(j + 1) * LANES] * jnp.exp(cs_p))
            wxs.append(_bf(xdt * jnp.exp(cs_p[CHUNK - 1:CHUNK, :] - cs_p)))
            css.append(cs_p)
        cs_g = jnp.concatenate(css, axis=1)
        last_t = jnp.broadcast_to(cs_g[CHUNK - 1:CHUNK, :], (CHUNK, GROUP_W)).T
        h_ref[gs, :] = hprev * jnp.exp(last_t) + _dot_tn(jnp.concatenate(wxs, axis=1), bmg)
        yg = jnp.concatenate(ys, axis=1)
        yg = (yg + xs[:, gs] * dsk_ref[:, gs]) * z_ref[:, gs].astype(F32)
        y_ref[:, gs] = _bf(_rms(yg) * nrm_ref[:, gs])

    tcol = row.astype(F32)
    diff = tcol - col.astype(F32)
    for h in range(RET_HEADS):
        lg = _ret_log_gamma(h)
        ks = slice(h * RET_DK, (h + 1) * RET_DK)
        vs = slice(h * RET_DV, (h + 1) * RET_DV)
        qb = rq_ref[:, ks]
        kb = rk_ref[:, ks]
        decay = jnp.exp(jnp.where(tri, diff * lg, -jnp.inf))
        att = _dot_nt(qb, kb) * decay
        vb = rv_ref[:, vs]
        s_prev = s_ref[h]
        inner = jnp.exp((tcol + 1.0) * lg)
        r = _dot(_bf(att), vb) + _dot(qb, _bf(s_prev)) * jnp.concatenate([inner] * (RET_DV // LANES), axis=1)
        tail_w = jnp.exp((CHUNK - 1.0 - tcol) * lg)
        kt = _bf(kb.astype(F32) * jnp.concatenate([tail_w] * (RET_DK // LANES), axis=1))
        s_ref[h] = s_prev * math.exp(CHUNK * lg) + _dot_tn(kt, vb)
        os = slice(SSD_D_INNER + h * RET_DV, SSD_D_INNER + (h + 1) * RET_DV)
        y_ref[:, os] = _bf(rg_ref[:, vs].astype(F32) * _rms(r))


def _mixer_ab_prompt(p_gates, p_xqk, p_dt, a_log, d_skip_e, ssd_norm):
    nc = SEQ // CHUNK
    m = BATCH * SEQ

    def rowspec(width, cb):
        return pl.BlockSpec((CHUNK, width), lambda b, c: (b * nc + c, cb))

    def full(a):
        return pl.BlockSpec(a.shape, lambda b, c: (0,) * a.ndim)

    params = [a_log, d_skip_e, ssd_norm]
    return pl.pallas_call(
        _ab_prompt_body,
        grid=(BATCH, nc),
        in_specs=[rowspec(SSD_D_INNER, 0), rowspec(RET_V, 1), rowspec(RET_V, 2), rowspec(SSD_CONV_CH, 0),
                  rowspec(RET_QK, SSD_CONV_CH // RET_QK), rowspec(RET_QK, SSD_CONV_CH // RET_QK + 1),
                  rowspec(LANES, 0)] + [full(a) for a in params],
        out_specs=[
            pl.BlockSpec((CHUNK, SSD_D_INNER + RET_V), lambda b, c: (b * nc + c, 0)),
            pl.BlockSpec((None, SSD_D_INNER, SSD_D_STATE), lambda b, c: (b, 0, 0)),
            pl.BlockSpec((None, RET_HEADS, RET_DK, RET_DV), lambda b, c: (b, 0, 0, 0)),
        ],
        out_shape=[
            jax.ShapeDtypeStruct((m, SSD_D_INNER + RET_V), BF16),
            jax.ShapeDtypeStruct((BATCH, SSD_D_INNER, SSD_D_STATE), F32),
            jax.ShapeDtypeStruct((BATCH, RET_HEADS, RET_DK, RET_DV), F32),
        ],
        compiler_params=_cp(("parallel", "arbitrary"), 48),
        name="mixer_ab_prompt",
    )(p_gates, p_gates, p_gates, p_xqk, p_xqk, p_xqk, p_dt, *params)


def _c_prompt_body(q_ref, k_ref, v_ref, o_ref, gt_ref, ib_ref, fb_ref, nrm_ref,
                   h_ref, c_ref, n_ref, m_ref, m_s):
    @pl.when(pl.program_id(1) == 0)
    def _():
        c_ref[...] = jnp.zeros_like(c_ref)
        n_ref[...] = jnp.zeros_like(n_ref)
        m_s[...] = jnp.zeros_like(m_s)

    row = lax.broadcasted_iota(jnp.int32, (CHUNK, CHUNK), 0)
    col = lax.broadcasted_iota(jnp.int32, (CHUNK, CHUNK), 1)
    tri = row >= col
    seqs = range(CB)
    ipre = [gt_ref[s, :, :LANES] + ib_ref[...] for s in seqs]
    lf = [-_softplus(-(gt_ref[s, :, LANES:] + fb_ref[...])) for s in seqs]
    b = [_cumsum_rows(lf[s]) for s in seqs]
    g = [ipre[s] - b[s] for s in seqs]
    g_t = [g[s].T for s in seqs]
    b_t = [b[s].T for s in seqs]
    cmax = list(g_t)
    sh = 1
    while sh < CHUNK:
        cmax = [jnp.maximum(cmax[s], jnp.where(col >= sh, pltpu.roll(cmax[s], sh, axis=1), -jnp.inf)) for s in seqs]
        sh *= 2
    m_prev = [m_s[s] for s in seqs]
    mt_t = [b_t[s] + jnp.maximum(m_prev[s], cmax[s]) for s in seqs]
    mt = [mt_t[s].T for s in seqs]
    m_prev_c = [m_prev[s].T for s in seqs]
    inter = [jnp.exp(b[s] + m_prev_c[s] - mt[s]) for s in seqs]
    emt = [jnp.exp(-mt[s]) for s in seqs]
    wl = [jnp.exp(g[s] + b[s][CHUNK - 1:CHUNK, :] - mt[s][CHUNK - 1:CHUNK, :]) for s in seqs]
    bm = [b[s] - mt[s] for s in seqs]
    m_new = [_lane_bcast(mt_t[s], CHUNK - 1) for s in seqs]
    dp_t = [jnp.exp(_lane_bcast(b_t[s], CHUNK - 1) + m_prev[s] - m_new[s]) for s in seqs]
    for s in seqs:
        m_s[s] = m_new[s]
        m_ref[s] = m_new[s][0:SUBLANES, :]
    for h in range(M_HEADS):
        ks = slice(h * M_DK, (h + 1) * M_DK)
        vs = slice(h * M_DV, (h + 1) * M_DV)
        for s in seqs:
            wgt = jnp.exp(jnp.where(tri, g_t[s][h:h + 1, :] + bm[s][:, h:h + 1], -jnp.inf))
            qb = q_ref[s, :, ks]
            kb = k_ref[s, :, ks]
            vb = v_ref[s, :, vs]
            a = _dot_nt(qb, kb) * wgt
            c_prev = c_ref[s, h]
            n_prev = n_ref[s, h:h + 1, :]
            ic = inter[s][:, h:h + 1]
            num = _dot(_bf(a), vb) + _dot(qb, _bf(c_prev)) * ic
            den = (jnp.sum(a, axis=1, keepdims=True)
                   + jnp.sum(qb.astype(F32) * n_prev, axis=1, keepdims=True) * ic)
            hc = num / jnp.maximum(jnp.abs(den), emt[s][:, h:h + 1])
            kw = kb.astype(F32) * wl[s][:, h:h + 1]
            dp_row = dp_t[s][h:h + 1, :]
            c_ref[s, h] = c_prev * jnp.concatenate([dp_row] * (M_DV // LANES), axis=1) + _dot_tn(_bf(kw), vb)
            n_ref[s, h:h + 1, :] = n_prev * dp_row + jnp.sum(kw, axis=0, keepdims=True)
            h_ref[s, :, vs] = _bf(o_ref[s, :, vs].astype(F32) * (_rms(hc) * nrm_ref[:, vs]))


def _mixer_c_prompt(p_main, p_gate, i_bias, f_bias, norm_g):
    nc = SEQ // CHUNK

    def rowspec(width, cb):
        return pl.BlockSpec((CB, CHUNK, width), lambda b, c: (b, c, cb))

    def full(a):
        return pl.BlockSpec(a.shape, lambda b, c: (0,) * a.ndim)

    params = [i_bias, f_bias, norm_g]
    return pl.pallas_call(
        _c_prompt_body,
        grid=(BATCH // CB, nc),
        in_specs=[rowspec(M_QK, 0), rowspec(M_QK, 1), rowspec(M_V, 1), rowspec(M_V, 2), rowspec(2 * LANES, 0)]
        + [full(a) for a in params],
        out_specs=[
            pl.BlockSpec((CB, CHUNK, M_V), lambda b, c: (b, c, 0)),
            pl.BlockSpec((CB, M_HEADS, M_DK, M_DV), lambda b, c: (b, 0, 0, 0)),
            pl.BlockSpec((CB, M_HEADS, M_DK), lambda b, c: (b, 0, 0)),
            pl.BlockSpec((CB, SUBLANES, LANES), lambda b, c: (b, 0, 0)),
        ],
        out_shape=[
            jax.ShapeDtypeStruct((BATCH, SEQ, M_V), BF16),
            jax.ShapeDtypeStruct((BATCH, M_HEADS, M_DK, M_DV), F32),
            jax.ShapeDtypeStruct((BATCH, M_HEADS, M_DK), F32),
            jax.ShapeDtypeStruct((BATCH, SUBLANES, LANES), F32),
        ],
        scratch_shapes=[pltpu.VMEM((CB, CHUNK, LANES), F32)],
        compiler_params=_cp(("parallel", "arbitrary"), 48),
        name="mixer_c_prompt",
    )(p_main, p_main, p_main, p_main, p_gate, *params)


def _ab_sample_prep_body(xbc_ref, rq_ref, rk_ref, dt_ref, cst_ref, cos_ref, sin_ref,
                         cw_ref, cb_ref, dtb_ref, alog_ref,
                         conv_ref, xs_ref, xdt_ref, eda_ref, bm_ref, cm_ref, q_ref, k_ref, gam_ref):
    ch = SSD_CONV_CH
    u = xbc_ref[...]
    w = cw_ref[...]
    b0 = cst_ref[:, 0:ch]
    b1 = cst_ref[:, ch:2 * ch]
    b2 = cst_ref[:, 2 * ch:3 * ch]
    conv = cb_ref[...] + (((b0 * w[0:1, :] + b1 * w[1:2, :]) + b2 * w[2:3, :]) + u * w[3:4, :])
    conv_ref[:, 0:ch] = b1
    conv_ref[:, ch:2 * ch] = b2
    conv_ref[:, 2 * ch:3 * ch] = u
    xbc = _silu(conv)
    xs = xbc[:, :SSD_D_INNER]
    xs_ref[...] = xs
    bm_ref[...] = xbc[:, SSD_D_INNER:SSD_D_INNER + GROUP_W]
    cm_ref[...] = xbc[:, SSD_D_INNER + GROUP_W:]
    dt = _softplus(dt_ref[...] + dtb_ref[...])
    eda = jnp.exp(dt * (-jnp.exp(alog_ref[...])))
    xdt_ref[...] = xs * _pair_expand(dt, SSD_HEADS)
    eda_ref[...] = _pair_expand(eda, SSD_HEADS)
    cos = cos_ref[...]
    sin = sin_ref[...]
    for h in range(RET_HEADS):
        a = slice(h * RET_DK, h * RET_DK + LANES)
        b = slice(h * RET_DK + LANES, (h + 1) * RET_DK)
        q1, q2 = rq_ref[:, a], rq_ref[:, b]
        k1, k2 = rk_ref[:, a], rk_ref[:, b]
        q_ref[:, a] = q1 * cos - q2 * sin
        q_ref[:, b] = q1 * sin + q2 * cos
        k_ref[:, a] = (k1 * cos - k2 * sin) * (RET_DK ** -0.5)
        k_ref[:, b] = (k1 * sin + k2 * cos) * (RET_DK ** -0.5)
        gam_ref[:, h * RET_DK:(h + 1) * RET_DK] = jnp.full((DEC_BATCH, RET_DK), math.exp(_ret_log_gamma(h)), F32)


def _ab_sample_prep(p_main, p_dt, conv_state, cos, sin, conv_w, conv_b, dt_bias, a_log):
    n = DEC_BATCH

    def colspec(width, cb):
        return pl.BlockSpec((n, width), lambda i: (0, cb))

    def full(a):
        return pl.BlockSpec(a.shape, lambda i: (0,) * a.ndim)

    small = [conv_state, cos, sin, conv_w, conv_b, dt_bias, a_log]

    def out(width):
        return jax.ShapeDtypeStruct((n, width), F32)

    widths = [(SSD_CONV - 1) * SSD_CONV_CH, SSD_D_INNER, SSD_D_INNER, SSD_D_INNER, GROUP_W, GROUP_W,
              RET_QK, RET_QK, RET_QK]
    return pl.pallas_call(
        _ab_sample_prep_body,
        grid=(1,),
        in_specs=[colspec(SSD_CONV_CH, 2), colspec(RET_QK, 9), colspec(RET_QK, 10), full(p_dt)]
        + [full(a) for a in small],
        out_specs=[pl.BlockSpec((n, wd), lambda i: (0, 0)) for wd in widths],
        out_shape=[out(wd) for wd in widths],
        compiler_params=_cp(("arbitrary",), 48),
        name="ab_sample_prep",
    )(p_main, p_main, p_main, p_dt, *small)


def _ssd_state_body(eda_ref, xdt_ref, bm_ref, cm_ref, h_ref, ho_ref, y_ref):
    eda_t = _pad_t(eda_ref[...])
    xdt_t = _pad_t(xdt_ref[...])
    cm = _bf(cm_ref[...])
    for b in range(SSD_SB):
        hn = h_ref[b] * eda_t[:, b:b + 1] + xdt_t[:, b:b + 1] * bm_ref[b:b + 1, :]
        ho_ref[b] = hn
        y_ref[b:b + 1, :] = _dot_nt(cm, _bf(hn))[b:b + 1, :]


def _ssd_state(eda, xdt, bm, cm, state):
    vec = pl.BlockSpec((SSD_SB, GROUP_W), lambda i, g: (i, g))
    bc = pl.BlockSpec((SSD_SB, SSD_D_STATE), lambda i, g: (i, g))
    st = pl.BlockSpec((SSD_SB, GROUP_W, SSD_D_STATE), lambda i, g: (i, g, 0))
    return pl.pallas_call(
        _ssd_state_body,
        grid=(DEC_BATCH // SSD_SB, SSD_GROUPS),
        in_specs=[vec, vec, bc, bc, st],
        out_specs=[st, vec],
        out_shape=[jax.ShapeDtypeStruct(state.shape, F32), jax.ShapeDtypeStruct((DEC_BATCH, SSD_D_INNER), F32)],
        compiler_params=_cp(("parallel", "arbitrary"), 48),
        name="ssd_state",
    )(eda, xdt, bm, cm, state)


def _outer_state_body(d_ref, k_ref, q_ref, v_ref, s_ref, so_ref, o_ref):
    d_t = _pad_t(d_ref[...])
    k_t = _pad_t(k_ref[...])
    q_t = _pad_t(q_ref[...])
    for b in range(d_ref.shape[0]):
        sn = s_ref[b] * d_t[:, b:b + 1] + k_t[:, b:b + 1] * v_ref[b:b + 1, :]
        so_ref[b] = sn
        o_ref[b:b + 1, :] = jnp.sum(sn * q_t[:, b:b + 1], axis=0, keepdims=True)


def _outer_state(d, k, q, v, state):
    _, nh, dk, dv = state.shape
    sb = STATE_BLOCK_BYTES // (dk * dv * 4)
    kv = pl.BlockSpec((sb, dk), lambda i, h: (i, h))
    vv = pl.BlockSpec((sb, dv), lambda i, h: (i, h))
    st = pl.BlockSpec((sb, None, dk, dv), lambda i, h: (i, h, 0, 0))
    return pl.pallas_call(
        _outer_state_body,
        grid=(DEC_BATCH // sb, nh),
        in_specs=[kv, kv, kv, vv, st],
        out_specs=[st, vv],
        out_shape=[jax.ShapeDtypeStruct(state.shape, F32), jax.ShapeDtypeStruct((DEC_BATCH, nh * dv), F32)],
        compiler_params=_cp(("parallel", "arbitrary"), 48),
        name="outer_state",
    )(d, k, q, v, state)


def _ab_sample_post_body(y_ref, xs_ref, z_ref, r_ref, rg_ref, dsk_ref, nrm_ref, o_ref):
    for g in range(SSD_GROUPS):
        gs = slice(g * GROUP_W, (g + 1) * GROUP_W)
        yg = (y_ref[:, gs] + xs_ref[:, gs] * dsk_ref[:, gs]) * _silu(z_ref[:, gs])
        o_ref[:, gs] = _bf(_rms(yg) * nrm_ref[:, gs])
    for h in range(RET_HEADS):
        vs = slice(h * RET_DV, (h + 1) * RET_DV)
        os = slice(SSD_D_INNER + h * RET_DV, SSD_D_INNER + (h + 1) * RET_DV)
        o_ref[:, os] = _bf(_silu(rg_ref[:, vs]) * _rms(r_ref[:, vs]))


def _ab_sample_post(y, xs, p_main, r, d_skip_e, ssd_norm):
    n = DEC_BATCH

    def full(a):
        return pl.BlockSpec(a.shape, lambda i: (0,) * a.ndim)

    return pl.pallas_call(
        _ab_sample_post_body,
        grid=(1,),
        in_specs=[full(y), full(xs), pl.BlockSpec((n, SSD_D_INNER), lambda i: (0, 0)), full(r),
                  pl.BlockSpec((n, RET_V), lambda i: (0, 2)), full(d_skip_e), full(ssd_norm)],
        out_specs=pl.BlockSpec((n, SSD_D_INNER + RET_V), lambda i: (0, 0)),
        out_shape=jax.ShapeDtypeStruct((n, SSD_D_INNER + RET_V), BF16),
        compiler_params=_cp(("arbitrary",), 48),
        name="ab_sample_post",
    )(y, xs, p_main, r, p_main, d_skip_e, ssd_norm)


def _c_sample_prep_body(q_ref, k_ref, gt_ref, n_ref, m_ref, ib_ref, fb_ref,
                        dpe_ref, kw_ref, nn_ref, mn_ref, dn_ref):
    ipre = gt_ref[:, :LANES] + ib_ref[...]
    lf = -_softplus(-(gt_ref[:, LANES:] + fb_ref[...]))
    m_prev = m_ref[...]
    mt = jnp.maximum(lf + m_prev, ipre)
    wgt = jnp.exp(ipre - mt)
    dp = jnp.exp(lf + m_prev - mt)
    emt = jnp.exp(-mt)
    mn_ref[...] = mt
    for h in range(M_HEADS):
        ks = slice(h * M_DK, (h + 1) * M_DK)
        dpe = _lane_bcast(dp, h)
        kw = k_ref[:, ks] * (M_DK ** -0.5) * _lane_bcast(wgt, h)
        nn = n_ref[:, ks] * dpe + kw
        den = jnp.sum(nn * q_ref[:, ks], axis=1, keepdims=True)
        dpe_ref[:, ks] = dpe
        kw_ref[:, ks] = kw
        nn_ref[:, ks] = nn
        dn_ref[:, h * M_DV:(h + 1) * M_DV] = jnp.broadcast_to(
            jnp.maximum(jnp.abs(den), emt[:, h:h + 1]), (DEC_BATCH, M_DV))


def _c_sample_prep(p_main, p_gate, n_state, m_state, i_bias, f_bias):
    n = DEC_BATCH

    def full(a):
        return pl.BlockSpec(a.shape, lambda i: (0,) * a.ndim)

    widths = [M_QK, M_QK, M_QK, LANES, M_V]
    return pl.pallas_call(
        _c_sample_prep_body,
        grid=(1,),
        in_specs=[pl.BlockSpec((n, M_QK), lambda i: (0, 0)), pl.BlockSpec((n, M_QK), lambda i: (0, 1)),
                  full(p_gate), full(n_state), full(m_state), full(i_bias), full(f_bias)],
        out_specs=[pl.BlockSpec((n, wd), lambda i: (0, 0)) for wd in widths],
        out_shape=[jax.ShapeDtypeStruct((n, wd), F32) for wd in widths],
        compiler_params=_cp(("arbitrary",), 48),
        name="c_sample_prep",
    )(p_main, p_main, p_gate, n_state, m_state, i_bias, f_bias)


def _c_sample_post_body(num_ref, dn_ref, o_ref, nrm_ref, h_ref):
    for h in range(M_HEADS):
        vs = slice(h * M_DV, (h + 1) * M_DV)
        hc = num_ref[:, vs] / dn_ref[:, vs]
        h_ref[:, vs] = _bf(jax.nn.sigmoid(o_ref[:, vs]) * (_rms(hc) * nrm_ref[:, vs]))


def _c_sample_post(num, den, p_main, norm_g):
    n = DEC_BATCH

    def full(a):
        return pl.BlockSpec(a.shape, lambda i: (0,) * a.ndim)

    return pl.pallas_call(
        _c_sample_post_body,
        grid=(1,),
        in_specs=[full(num), full(den), pl.BlockSpec((n, M_V), lambda i: (0, 2)), full(norm_g)],
        out_specs=pl.BlockSpec((n, M_V), lambda i: (0, 0)),
        out_shape=jax.ShapeDtypeStruct((n, M_V), BF16),
        compiler_params=_cp(("arbitrary",), 48),
        name="c_sample_post",
    )(num, den, p_main, norm_g)


def _rope_tables(pos):
    half = RET_DK // 2
    inv = jnp.exp(-math.log(ROPE_BASE) * jnp.arange(half, dtype=F32) / half)
    ang = pos.astype(F32)[:, None] * inv
    return jnp.cos(ang), jnp.sin(ang)


def _pad_lanes(v, width=LANES):
    return jnp.pad(v.reshape(1, -1), ((0, 0), (0, width - v.size)))


def kernel(x_prompt, x_sample, cache_mem_k, cache_mem_v, state_conv, state_ssm, state_ret, state_mlstm_c, state_mlstm_n, state_mlstm_m, mem_prompt, norm_ffn1, w_ffn1_in, w_ffn1_out, norm_mix, w_in_ab, ssd_conv_w, ssd_conv_b, ssd_dt_bias, ssd_a_log, ssd_d, ssd_norm, w_out_ab, w_in_c, mlstm_i_bias, mlstm_f_bias, mlstm_norm, w_out_c, norm_xattn, norm_mem, w_xq, w_xkv, w_xo, norm_ffn2, w_ffn2_in, w_ffn2_out, norm_final):
    g3 = lambda g: g.reshape(DEPTH, 1, D_MODEL)
    n_ffn1, n_mix, n_x, n_mem, n_ffn2 = g3(norm_ffn1), g3(norm_mix), g3(norm_xattn), g3(norm_mem), g3(norm_ffn2)

    f1_in, f1_out, f2_in, f2_out = (w.astype(BF16) for w in (w_ffn1_in, w_ffn1_out, w_ffn2_in, w_ffn2_out))
    w_out_ab, w_out_c, w_xq, w_xkv, w_xo = (w.astype(BF16) for w in (w_out_ab, w_out_c, w_xq, w_xkv, w_xo))

    wz, wxbc, wdt, wrq, wrk, wrv, wrg = jnp.split(w_in_ab[0], np_cumsum(AB_SIZES), axis=1)
    w_ab_main = jnp.concatenate([wz, wrv, wrg, wxbc, wrq, wrk], axis=1).astype(BF16)[None]
    w_ab_dt = jnp.pad(wdt, ((0, 0), (0, LANES - SSD_HEADS))).astype(BF16)[None]
    wq, wk, wv, wi, wf, wo = jnp.split(w_in_c[0], np_cumsum(C_SIZES), axis=1)
    w_c_main = jnp.concatenate([wq, wk, wv, wo], axis=1).astype(BF16)[None]
    gpad = ((0, 0), (0, LANES - M_HEADS))
    w_c_gate = jnp.concatenate([jnp.pad(wi, gpad), jnp.pad(wf, gpad)], axis=1).astype(BF16)[None]

    conv_w = ssd_conv_w[0]
    conv_b = ssd_conv_b.reshape(1, SSD_CONV_CH)
    dt_bias = _pad_lanes(ssd_dt_bias[0])
    a_log = _pad_lanes(ssd_a_log[0])
    d_skip_e = jnp.repeat(ssd_d[0], SSD_HEAD_DIM).reshape(1, SSD_D_INNER)
    s_norm = ssd_norm.reshape(1, SSD_D_INNER)
    i_bias = _pad_lanes(mlstm_i_bias[0])
    f_bias = _pad_lanes(mlstm_f_bias[0])
    m_norm = mlstm_norm.reshape(1, M_V)

    memk, memv = _mem_kv(mem_prompt.reshape(BATCH * MEM_LEN, D_MODEL), n_mem, w_xkv)
    cos_p, sin_p = _rope_tables(jnp.arange(SEQ))
    x = x_prompt.reshape(BATCH * SEQ, D_MODEL)

    cos_s, sin_s = _rope_tables(PAST_LEN + jnp.arange(1))
    xs_ = x_sample.reshape(DEC_BATCH, D_MODEL)
    xs_ = _ffn(xs_, n_ffn1, f1_in, f1_out, 0)
    sp_main, sp_dt = _norm_proj(xs_, n_mix, 0, w_ab_main, 0, 1024, w_ab_dt)
    conv_s, xs_c, xdt, eda, bm_s, cm_s, q_s, k_s, gam = _ab_sample_prep(
        sp_main, sp_dt, state_conv.reshape(DEC_BATCH, (SSD_CONV - 1) * SSD_CONV_CH), cos_s, sin_s,
        conv_w, conv_b, dt_bias, a_log)
    ssm_s, y_s = _ssd_state(eda, xdt, bm_s, cm_s, state_ssm.reshape(DEC_BATCH, SSD_D_INNER, SSD_D_STATE))
    ret_s, r_s = _outer_state(gam, k_s, q_s, sp_main[:, SSD_D_INNER:SSD_D_INNER + RET_V], state_ret[0])
    ycat_s = _ab_sample_post(y_s, xs_c, sp_main, r_s, d_skip_e, s_norm)
    xs_ = _proj_residual(xs_, ycat_s, w_out_ab, 0)
    q_x0 = _norm_proj(xs_, n_x, 0, w_xq, 0, 1024).reshape(DEC_BATCH, X_HEADS, X_HEAD_DIM)

    x, o_x0 = _ffn(x, n_ffn1, f1_in, f1_out, 0, host=(q_x0, cache_mem_k, cache_mem_v, 0))

    xs_ = _proj_residual(xs_, o_x0.reshape(DEC_BATCH, D_MODEL), w_xo, 0)
    xs_ = _ffn(xs_, n_ffn2, f2_in, f2_out, 0)
    xs_ = _ffn(xs_, n_ffn1, f1_in, f1_out, 1)
    sc_main, sc_gate = _norm_proj(xs_, n_mix, 1, w_c_main, 0, 1024, w_c_gate)
    m_in = jnp.pad(state_mlstm_m[0], ((0, 0), (0, LANES - M_HEADS)))
    dpe, kw, mn_s, mm_s, den = _c_sample_prep(sc_main, sc_gate, state_mlstm_n.reshape(DEC_BATCH, M_QK), m_in,
                                              i_bias, f_bias)
    mc_s, num = _outer_state(dpe, kw, sc_main[:, :M_QK], sc_main[:, 2 * M_QK:2 * M_QK + M_V], state_mlstm_c[0])
    hout_s = _c_sample_post(num, den, sc_main, m_norm)
    xs_ = _proj_residual(xs_, hout_s, w_out_c, 0)
    q_x1 = _norm_proj(xs_, n_x, 1, w_xq, 1, 1024).reshape(DEC_BATCH, X_HEADS, X_HEAD_DIM)

    w_gates =w_ab_main[0, :, :AB_GATE_BLOCKS * AB_TN]
    w_xqk = w_ab_main[0, :, AB_GATE_BLOCKS * AB_TN:]
    p_gates, p_xqk, p_dt = _ab_inproj(x, n_mix, w_gates, w_xqk, w_ab_dt[0], dt_bias, conv_w, conv_b, cos_p, sin_p)
    ycat, ssm_p, ret_p = _mixer_ab_prompt(p_gates, p_xqk, p_dt, a_log, d_skip_e, s_norm)
    x_tail = x.reshape(BATCH, SEQ, D_MODEL)[:, SEQ - SSD_CONV:].reshape(BATCH * SSD_CONV, D_MODEL)
    conv_p = _norm_proj(x_tail, n_mix, 0, w_ab_main, 0, AB_TN, col0=AB_XBC0, n_out=SSD_CONV_CH)
    conv_p = conv_p.reshape(BATCH, SSD_CONV, SSD_CONV_CH)[:, 1:]
    x = _proj_residual(x, ycat, w_out_ab, 0)
    x = _xattn_prompt(x, n_x, w_xq, w_xo, memk, memv, 0)
    x, o_x1 = _ffn(x, n_ffn2, f2_in, f2_out, 0, host=(q_x1, cache_mem_k, cache_mem_v, 1))
    xs_ = _proj_residual(xs_, o_x1.reshape(DEC_BATCH, D_MODEL), w_xo, 1)
    y_sample = _ffn(xs_, n_ffn2, f2_in, f2_out, 1, norm_final).reshape(DEC_BATCH, 1, D_MODEL)
    x = _ffn(x, n_ffn1, f1_in, f1_out, 1)
    pc_main, pc_gate = _c_inproj(x, n_mix, 1, w_c_main, w_c_gate)
    hout, mc_p, mn_p, mm_p = _mixer_c_prompt(pc_main.reshape(BATCH, SEQ, -1), pc_gate.reshape(BATCH, SEQ, -1),
                                             i_bias, f_bias, m_norm)
    x = _proj_residual(x, hout.reshape(BATCH * SEQ, M_V), w_out_c, 0)
    x = _xattn_prompt(x, n_x, w_xq, w_xo, memk, memv, 1)
    y_prompt = _ffn(x, n_ffn2, f2_in, f2_out, 1, norm_final).reshape(BATCH, SEQ, D_MODEL)

    kv_shape = (DEPTH, BATCH, MEM_LEN, X_HEADS, X_HEAD_DIM)
    return (y_prompt, y_sample, memk.reshape(kv_shape), memv.reshape(kv_shape),
            conv_p.reshape(1, BATCH, SSD_CONV - 1, SSD_CONV_CH),
            conv_s.reshape(1, DEC_BATCH, SSD_CONV - 1, SSD_CONV_CH),
            ssm_p.reshape(1, BATCH, SSD_HEADS, SSD_HEAD_DIM, SSD_D_STATE),
            ssm_s.reshape(1, DEC_BATCH, SSD_HEADS, SSD_HEAD_DIM, SSD_D_STATE),
            ret_p[None], ret_s[None], mc_p[None], mc_s[None],
            mn_p[None], mn_s.reshape(1, DEC_BATCH, M_HEADS, M_DK),
            mm_p[:, :M_HEADS, 0][None], mm_s[:, :M_HEADS][None])


def np_cumsum(sizes):
    out, acc = [], 0
    for s in sizes[:-1]:
        acc += s
        out.append(acc)
    return out
```

```python
import functools
import math

import jax
import jax.numpy as jnp
from jax import lax
from jax.experimental import pallas as pl
from jax.experimental.pallas import tpu as pltpu

F32 = jnp.float32
BF16 = jnp.bfloat16
EPS = 1e-6

D_MODEL = 1024
BATCH = 8
SEQ = 2048
DEPTH = 2
DEC_BATCH = 128
PAST_LEN = 16384
CHUNK = 128
D_FF = 2816
SSD_D_INNER = 2 * D_MODEL
SSD_HEAD_DIM = 64
SSD_HEADS = SSD_D_INNER // SSD_HEAD_DIM
SSD_GROUPS = 4
SSD_D_STATE = 128
SSD_CONV = 4
SSD_CONV_CH = SSD_D_INNER + 2 * SSD_GROUPS * SSD_D_STATE
RET_HEADS = 4
RET_QK = D_MODEL
RET_V = 2 * D_MODEL
RET_DK = RET_QK // RET_HEADS
RET_DV = RET_V // RET_HEADS
ROPE_BASE = 10000.0
AB_SIZES = (SSD_D_INNER, SSD_CONV_CH, SSD_HEADS, RET_QK, RET_QK, RET_V, RET_V)
M_HEADS = 4
M_QK = D_MODEL // 2
M_V = D_MODEL
M_DK = M_QK // M_HEADS
M_DV = M_V // M_HEADS
C_SIZES = (M_QK, M_QK, M_V, M_HEADS, M_HEADS, M_V)
MEM_LEN = 256
X_HEADS = 4
X_HEAD_DIM = D_MODEL // X_HEADS

LANES = 128
SUBLANES = 8
GROUP_W = SSD_D_INNER // SSD_GROUPS
HEADS_PER_GROUP = SSD_HEADS // SSD_GROUPS
ROW_TILE = 1024
FF_TILE = 256
SSD_SB = 16
STATE_BLOCK_BYTES = 8 * 1024 * 1024
XB = 4
CB = 2


def _cp(sem, mib):
    return pltpu.CompilerParams(dimension_semantics=sem, vmem_limit_bytes=mib * 1024 * 1024)


def _bf(x):
    return x.astype(BF16)


def _dot(a, b):
    return jnp.dot(a, b, preferred_element_type=F32)


def _dot_nt(a, b):
    return lax.dot_general(a, b, (((1,), (1,)), ((), ())), preferred_element_type=F32)


def _dot_tn(a, b):
    return lax.dot_general(a, b, (((0,), (0,)), ((), ())), preferred_element_type=F32)


def _rms(x):
    return x * lax.rsqrt(jnp.mean(x * x, axis=-1, keepdims=True) + EPS)


def _silu(x):
    return x * jax.nn.sigmoid(x)


def _softplus(x):
    return jnp.maximum(x, 0.0) + jnp.log1p(jnp.exp(-jnp.abs(x)))


def _split3(x):
    hi = x.astype(BF16)
    r = x - hi.astype(F32)
    mid = r.astype(BF16)
    lo = (r - mid.astype(F32)).astype(BF16)
    return hi, mid, lo


def _cumsum_rows(x):
    n = x.shape[0]
    r = lax.broadcasted_iota(jnp.int32, (n, n), 0)
    c = lax.broadcasted_iota(jnp.int32, (n, n), 1)
    t = jnp.where(r >= c, 1.0, 0.0).astype(BF16)
    hi, mid, lo = _split3(x)
    return _dot(t, hi) + _dot(t, mid) + _dot(t, lo)


def _lane_bcast(x, h, width=LANES):
    return jnp.broadcast_to(x[:, h:h + 1], (x.shape[0], width))


def _pair_expand(x, n_heads):
    rows = x.shape[0]
    lo = lax.broadcasted_iota(jnp.int32, (rows, LANES), 1) < SSD_HEAD_DIM
    return jnp.concatenate(
        [jnp.where(lo, _lane_bcast(x, 2 * j), _lane_bcast(x, 2 * j + 1)) for j in range(n_heads // 2)], axis=1)


def _pad_t(x):
    pad = jnp.zeros((LANES - x.shape[0], x.shape[1]), F32)
    return jnp.concatenate([x, pad], axis=0).T


def _ffn_body(*refs, nf, final, hosted):
    x_ref, g_ref, wi_ref, wo_ref = refs[:4]
    rest = list(refs[4:])
    fg_ref = rest.pop(0) if final else None
    if hosted:
        q_ref, k_ref, v_ref = rest[:3]
        o_ref, xo_ref = rest[3:]
        _xattn_sample_body(q_ref, k_ref, v_ref, xo_ref)
    else:
        o_ref, = rest
    x = x_ref[...]
    xn = _bf(_rms(x) * g_ref[...])
    acc = None
    for f in range(nf):
        fs = slice(f * FF_TILE, (f + 1) * FF_TILE)
        g = _dot(xn, wi_ref[:, fs])
        u = _dot(xn, wi_ref[:, D_FF + f * FF_TILE:D_FF + (f + 1) * FF_TILE])
        t = _dot(_bf(_silu(g) * u), wo_ref[fs, :])
        acc = t if acc is None else acc + t
    y = x + 0.5 * acc
    if final:
        y = _rms(y) * fg_ref[...]
    o_ref[...] = y


def _ffn(x, g3, w_in, w_out, layer, final_g=None, host=None):
    m = x.shape[0]
    tm = min(m, ROW_TILE) if host is None else m // (DEC_BATCH // XB)
    nf = D_FF // FF_TILE
    once = pl.Buffered(1)
    in_specs = [
        pl.BlockSpec((tm, D_MODEL), lambda i: (i, 0)),
        pl.BlockSpec((None, 1, D_MODEL), lambda i: (layer, 0, 0)),
        pl.BlockSpec((None, D_MODEL, 2 * D_FF), lambda i: (layer, 0, 0), pipeline_mode=once),
        pl.BlockSpec((None, D_FF, D_MODEL), lambda i: (layer, 0, 0), pipeline_mode=once),
    ]
    args = [x, g3, w_in, w_out]
    if final_g is not None:
        in_specs.append(pl.BlockSpec((1, D_MODEL), lambda i: (0, 0)))
        args.append(final_g.reshape(1, D_MODEL))
    out_specs = [pl.BlockSpec((tm, D_MODEL), lambda i: (i, 0))]
    out_shape = [jax.ShapeDtypeStruct((m, D_MODEL), F32)]
    if host is not None:
        q, cache_k, cache_v, cl = host
        blk = pl.BlockSpec((None, XB, MEM_LEN, X_HEADS, X_HEAD_DIM), lambda i: (cl, i, 0, 0, 0))
        qo = pl.BlockSpec((XB, X_HEADS, X_HEAD_DIM), lambda i: (i, 0, 0))
        in_specs += [qo, blk, blk]
        args += [q, cache_k, cache_v]
        out_specs.append(qo)
        out_shape.append(jax.ShapeDtypeStruct((DEC_BATCH, X_HEADS, X_HEAD_DIM), F32))
    res = pl.pallas_call(
        functools.partial(_ffn_body, nf=nf, final=final_g is not None, hosted=host is not None),
        grid=(m // tm,),
        in_specs=in_specs,
        out_specs=out_specs,
        out_shape=out_shape,
        compiler_params=_cp(("parallel",), 56),
        name="ffn",
    )(*args)
    return res if host is not None else res[0]


def _norm_proj_body(*refs, small):
    if small:
        x_ref, g_ref, w_ref, ws_ref, o_ref, os_ref, xn_ref = refs
    else:
        x_ref, g_ref, w_ref, o_ref, xn_ref = refs
    n = pl.program_id(1)

    @pl.when(n == 0)
    def _():
        xn = _bf(_rms(x_ref[...]) * g_ref[...])
        xn_ref[...] = xn
        if small:
            os_ref[...] = _dot(xn, _bf(ws_ref[...]))

    o_ref[...] = _dot(xn_ref[...], _bf(w_ref[...]))


def _norm_proj(x, g3, glayer, w3, wlayer, tn, w_small=None, col0=0, n_out=None):
    m = x.shape[0]
    tm = min(m, ROW_TILE)
    n_out = w3.shape[-1] if n_out is None else n_out
    in_specs = [
        pl.BlockSpec((tm, D_MODEL), lambda i, n: (i, 0)),
        pl.BlockSpec((None, 1, D_MODEL), lambda i, n: (glayer, 0, 0)),
        pl.BlockSpec((None, D_MODEL, tn), lambda i, n: (wlayer, 0, n + col0)),
    ]
    args = [x, g3, w3]
    out_specs = [pl.BlockSpec((tm, tn), lambda i, n: (i, n))]
    out_shape = [jax.ShapeDtypeStruct((m, n_out), F32)]
    if w_small is not None:
        ns = w_small.shape[-1]
        in_specs.append(pl.BlockSpec((None, D_MODEL, ns), lambda i, n: (0, 0, 0)))
        args.append(w_small)
        out_specs.append(pl.BlockSpec((tm, ns), lambda i, n: (i, 0)))
        out_shape.append(jax.ShapeDtypeStruct((m, ns), F32))
    res = pl.pallas_call(
        functools.partial(_norm_proj_body, small=w_small is not None),
        grid=(m // tm, n_out // tn),
        in_specs=in_specs,
        out_specs=out_specs,
        out_shape=out_shape,
        scratch_shapes=[pltpu.VMEM((tm, D_MODEL), BF16)],
        compiler_params=_cp(("parallel", "arbitrary"), 48),
        name="norm_proj",
    )(*args)
    return res if w_small is not None else res[0]


AB_TN = 1024
AB_XBC0 = (SSD_D_INNER + 2 * RET_V) // AB_TN
AB_GATE_BLOCKS = AB_XBC0
AB_XQK_BLOCKS = SSD_CONV_CH // AB_TN + 2 * RET_QK // AB_TN
AB_ROW_TILE = 512


def _ab_gates_body(x_ref, g_ref, w_ref, o_ref):
    xn = _bf(_rms(x_ref[...]) * g_ref[...])
    nz = SSD_D_INNER // AB_TN
    nv = RET_V // AB_TN
    for n in range(AB_GATE_BLOCKS):
        cs = slice(n * AB_TN, (n + 1) * AB_TN)
        r = _dot(xn, w_ref[:, cs])
        o_ref[:, cs] = _bf(r) if nz <= n < nz + nv else _bf(_silu(r))


def _ab_xqk_body(x_ref, g_ref, w_ref, ws_ref, dtb_ref, cw_ref, cb_ref, cos_ref, sin_ref,
                 o_ref, dt_ref, xpad, rbuf, ybuf, carry, *, tiles_per_seq):
    i = pl.program_id(0)
    tm = x_ref.shape[0]
    nslab = tm // SUBLANES
    pitch = nslab + SUBLANES
    nxb = SSD_CONV_CH // AB_TN
    ntap = SSD_CONV - 1

    @pl.when(i == 0)
    def _():
        carry[...] = jnp.zeros_like(carry)

    g = g_ref[...]
    xn = _bf(_rms(x_ref[...]) * g)
    dt_ref[...] = _softplus(_dot(xn, ws_ref[...]) + dtb_ref[...])
    for lb in range(D_MODEL // LANES):
        for s in range(SUBLANES):
            xpad[lb, s * pitch:s * pitch + nslab, :] = x_ref[s * nslab:(s + 1) * nslab, lb * LANES:(lb + 1) * LANES]
    xp = jnp.concatenate(
        [jnp.concatenate([xpad[lb, pl.ds(v, SUBLANES, stride=pitch), :] for lb in range(D_MODEL // LANES)], axis=1)
         for v in range(nslab)], axis=0)
    xnp = _bf(_rms(xp) * g)
    seq_start = i % tiles_per_seq == 0
    first = lax.broadcasted_iota(jnp.int32, (SUBLANES, AB_TN), 0) == 0
    halo = ntap * SUBLANES
    for n in range(nxb):
        cs = slice(n * AB_TN, (n + 1) * AB_TN)
        rbuf[n, halo:halo + tm, :] = _dot(xnp, w_ref[:, cs])
        prev_rows = jnp.where(seq_start, 0.0, carry[n])
        for k in range(ntap):
            hi = rbuf[n, tm + k * SUBLANES:tm + (k + 1) * SUBLANES, :]
            rbuf[n, k * SUBLANES:(k + 1) * SUBLANES, :] = jnp.where(first, prev_rows[k:k + 1, :],
                                                                     pltpu.roll(hi, 1, axis=0))
            carry[n, k:k + 1, :] = hi[SUBLANES - 1:SUBLANES, :]
        w = cw_ref[:, cs]
        conv = cb_ref[:, cs] + rbuf[n, halo:halo + tm, :] * w[ntap:ntap + 1, :]
        for j in range(1, SSD_CONV):
            off = halo - j * SUBLANES
            conv = conv + rbuf[n, off:off + tm, :] * w[ntap - j:ntap - j + 1, :]
        y = _silu(conv)
        for lb in range(AB_TN // LANES):
            ls = slice(lb * LANES, (lb + 1) * LANES)
            for v in range(nslab):
                ybuf[n, lb, pl.ds(v, SUBLANES, stride=pitch), :] = y[v * SUBLANES:(v + 1) * SUBLANES, ls]
            for s in range(SUBLANES):
                o_ref[s * nslab:(s + 1) * nslab, n * AB_TN + ls.start:n * AB_TN + ls.stop] = _bf(
                    ybuf[n, lb, s * pitch:s * pitch + nslab, :])
    cos = cos_ref[...]
    sin = sin_ref[...]
    for n in range(nxb, AB_XQK_BLOCKS):
        r = _dot(xn, w_ref[:, n * AB_TN:(n + 1) * AB_TN])
        scale = 1.0 if n == nxb else RET_DK ** -0.5
        for h in range(AB_TN // RET_DK):
            a = slice(h * RET_DK, h * RET_DK + LANES)
            b = slice(h * RET_DK + LANES, (h + 1) * RET_DK)
            x1, x2 = r[:, a], r[:, b]
            o_ref[:, n * AB_TN + a.start:n * AB_TN + a.stop] = _bf((x1 * cos - x2 * sin) * scale)
            o_ref[:, n * AB_TN + b.start:n * AB_TN + b.stop] = _bf((x1 * sin + x2 * cos) * scale)


def _ab_inproj(x, g3, w_gates, w_xqk, w_dt, dt_bias, conv_w, conv_b, cos, sin):
    m = x.shape[0]
    tm = AB_ROW_TILE
    tps = SEQ // tm
    nxb = SSD_CONV_CH // AB_TN
    once = pl.Buffered(1)

    def full(a):
        return pl.BlockSpec(a.shape, lambda i: (0,) * a.ndim, pipeline_mode=once)

    xspec = pl.BlockSpec((tm, D_MODEL), lambda i: (i, 0))
    gspec = pl.BlockSpec((None, 1, D_MODEL), lambda i: (0, 0, 0))
    gates = pl.pallas_call(
        _ab_gates_body,
        grid=(m // tm,),
        in_specs=[xspec, gspec, full(w_gates)],
        out_specs=pl.BlockSpec((tm, AB_GATE_BLOCKS * AB_TN), lambda i: (i, 0)),
        out_shape=jax.ShapeDtypeStruct((m, AB_GATE_BLOCKS * AB_TN), BF16),
        compiler_params=_cp(("parallel",), 56),
        name="ab_gates",
    )(x, g3, w_gates)
    xqk, dt = pl.pallas_call(
        functools.partial(_ab_xqk_body, tiles_per_seq=tps),
        grid=(m // tm,),
        in_specs=[xspec, gspec, full(w_xqk), full(w_dt), full(dt_bias), full(conv_w), full(conv_b),
                  pl.BlockSpec((tm, LANES), lambda i: (i % tps, 0)),
                  pl.BlockSpec((tm, LANES), lambda i: (i % tps, 0))],
        out_specs=[pl.BlockSpec((tm, AB_XQK_BLOCKS * AB_TN), lambda i: (i, 0)),
                   pl.BlockSpec((tm, LANES), lambda i: (i, 0))],
        out_shape=[jax.ShapeDtypeStruct((m, AB_XQK_BLOCKS * AB_TN), BF16), jax.ShapeDtypeStruct((m, LANES), F32)],
        scratch_shapes=[pltpu.VMEM((D_MODEL // LANES, tm + SUBLANES * SUBLANES, LANES), F32),
                        pltpu.VMEM((nxb, (SSD_CONV - 1) * SUBLANES + tm, AB_TN), F32),
                        pltpu.VMEM((nxb, AB_TN // LANES, tm + SUBLANES * SUBLANES, LANES), F32),
                        pltpu.VMEM((nxb, SUBLANES, AB_TN), F32)],
        compiler_params=_cp(("arbitrary",), 56),
        name="ab_xqk",
    )(x, g3, w_xqk, w_dt, dt_bias, conv_w, conv_b, cos, sin)
    return gates, xqk, dt


def _c_inproj_body(x_ref, g_ref, w_ref, wg_ref, o_ref, gt_ref):
    xn = _bf(_rms(x_ref[...]) * g_ref[...])
    gt_ref[...] = _dot(xn, wg_ref[...])
    o_ref[:, :M_QK] = _bf(_dot(xn, w_ref[:, :M_QK]))
    o_ref[:, M_QK:2 * M_QK] = _bf(_dot(xn, w_ref[:, M_QK:2 * M_QK]) * (M_DK ** -0.5))
    o_ref[:, 2 * M_QK:2 * M_QK + M_V] = _bf(_dot(xn, w_ref[:, 2 * M_QK:2 * M_QK + M_V]))
    o_ref[:, 2 * M_QK + M_V:] = _bf(jax.nn.sigmoid(_dot(xn, w_ref[:, 2 * M_QK + M_V:])))


def _c_inproj(x, g3, glayer, w_main, w_gate):
    m = x.shape[0]
    tm = ROW_TILE
    once = pl.Buffered(1)
    n_out = w_main.shape[-1]
    n_gate = w_gate.shape[-1]
    return pl.pallas_call(
        _c_inproj_body,
        grid=(m // tm,),
        in_specs=[
            pl.BlockSpec((tm, D_MODEL), lambda i: (i, 0)),
            pl.BlockSpec((None, 1, D_MODEL), lambda i: (glayer, 0, 0)),
            pl.BlockSpec((None, D_MODEL, n_out), lambda i: (0, 0, 0), pipeline_mode=once),
            pl.BlockSpec((None, D_MODEL, n_gate), lambda i: (0, 0, 0), pipeline_mode=once),
        ],
        out_specs=[pl.BlockSpec((tm, n_out), lambda i: (i, 0)), pl.BlockSpec((tm, n_gate), lambda i: (i, 0))],
        out_shape=[jax.ShapeDtypeStruct((m, n_out), BF16), jax.ShapeDtypeStruct((m, n_gate), F32)],
        compiler_params=_cp(("parallel",), 48),
        name="c_inproj",
    )(x, g3, w_main, w_gate)


def _proj_res_body(x_ref, y_ref, w_ref, o_ref):
    o_ref[...] = x_ref[...] + _dot(_bf(y_ref[...]), w_ref[...])


def _proj_residual(x, y, w3, layer):
    m = x.shape[0]
    tm = min(m, ROW_TILE)
    kdim = y.shape[1]
    return pl.pallas_call(
        _proj_res_body,
        grid=(m // tm,),
        in_specs=[
            pl.BlockSpec((tm, D_MODEL), lambda i: (i, 0)),
            pl.BlockSpec((tm, kdim), lambda i: (i, 0)),
            pl.BlockSpec((None, kdim, D_MODEL), lambda i: (layer, 0, 0), pipeline_mode=pl.Buffered(1)),
        ],
        out_specs=pl.BlockSpec((tm, D_MODEL), lambda i: (i, 0)),
        out_shape=jax.ShapeDtypeStruct((m, D_MODEL), F32),
        compiler_params=_cp(("parallel",), 56),
        name="proj_residual",
    )(x, y, w3)


def _mem_kv_body(x_ref, g_ref, w_ref, k_ref, v_ref, k4_ref, v4_ref):
    xn = _bf(_rms(x_ref[...]) * g_ref[...])
    kv = _dot(xn, _bf(w_ref[...]))
    k_ref[...] = kv[:, :D_MODEL]
    v_ref[...] = kv[:, D_MODEL:]
    for h in range(X_HEADS):
        k4_ref[:, h, :] = kv[:, h * X_HEAD_DIM:(h + 1) * X_HEAD_DIM]
        v4_ref[:, h, :] = kv[:, D_MODEL + h * X_HEAD_DIM:D_MODEL + (h + 1) * X_HEAD_DIM]


def _mem_kv(mem2d, g3, w_xkv):
    m = mem2d.shape[0]
    tm = 512
    shp = jax.ShapeDtypeStruct((DEPTH, m, D_MODEL), F32)
    shp4 = jax.ShapeDtypeStruct((DEPTH, m, X_HEADS, X_HEAD_DIM), F32)
    return pl.pallas_call(
        _mem_kv_body,
        grid=(DEPTH, m // tm),
        in_specs=[
            pl.BlockSpec((tm, D_MODEL), lambda l, i: (i, 0)),
            pl.BlockSpec((None, 1, D_MODEL), lambda l, i: (l, 0, 0)),
            pl.BlockSpec((None, D_MODEL, 2 * D_MODEL), lambda l, i: (l, 0, 0)),
        ],
        out_specs=[pl.BlockSpec((None, tm, D_MODEL), lambda l, i: (l, i, 0))] * 2
        + [pl.BlockSpec((None, tm, X_HEADS, X_HEAD_DIM), lambda l, i: (l, i, 0, 0))] * 2,
        out_shape=[shp, shp, shp4, shp4],
        compiler_params=_cp(("arbitrary", "arbitrary"), 48),
        name="mem_kv",
    )(mem2d, g3, w_xkv)


def _xattn_prompt_body(x_ref, g_ref, wq_ref, wo_ref, k_ref, v_ref, o_ref):
    x = x_ref[...]
    xn = _bf(_rms(x) * g_ref[...])
    q = _dot(xn, _bf(wq_ref[...]))
    k = _bf(k_ref[...])
    v = _bf(v_ref[...])
    outs = []
    for h in range(X_HEADS):
        sl = slice(h * X_HEAD_DIM, (h + 1) * X_HEAD_DIM)
        s = _dot_nt(_bf(q[:, sl]), k[:, sl]) * (X_HEAD_DIM ** -0.5)
        e = jnp.exp(s - jnp.max(s, axis=-1, keepdims=True))
        p = e / jnp.sum(e, axis=-1, keepdims=True)
        outs.append(_bf(_dot(_bf(p), v[:, sl])))
    o_ref[...] = x + _dot(jnp.concatenate(outs, axis=1), _bf(wo_ref[...]))


def _xattn_prompt(x, g3, w_xq, w_xo, memk, memv, layer):
    tq = ROW_TILE
    nq = SEQ // tq
    return pl.pallas_call(
        _xattn_prompt_body,
        grid=(BATCH, nq),
        in_specs=[
            pl.BlockSpec((tq, D_MODEL), lambda b, j: (b * nq + j, 0)),
            pl.BlockSpec((None, 1, D_MODEL), lambda b, j: (layer, 0, 0)),
            pl.BlockSpec((None, D_MODEL, D_MODEL), lambda b, j: (layer, 0, 0)),
            pl.BlockSpec((None, D_MODEL, D_MODEL), lambda b, j: (layer, 0, 0)),
            pl.BlockSpec((None, MEM_LEN, D_MODEL), lambda b, j: (layer, b, 0)),
            pl.BlockSpec((None, MEM_LEN, D_MODEL), lambda b, j: (layer, b, 0)),
        ],
        out_specs=pl.BlockSpec((tq, D_MODEL), lambda b, j: (b * nq + j, 0)),
        out_shape=jax.ShapeDtypeStruct((BATCH * SEQ, D_MODEL), F32),
        compiler_params=_cp(("parallel", "arbitrary"), 48),
        name="xattn_prompt",
    )(x, g3, w_xq, w_xo, memk, memv)


def _xattn_sample_body(q_ref, k_ref, v_ref, o_ref):
    for b in range(XB):
        s = jnp.sum(k_ref[b] * (q_ref[b] * (X_HEAD_DIM ** -0.5))[None], axis=-1, keepdims=True)
        e = jnp.exp(s - jnp.max(s, axis=0, keepdims=True))
        o_ref[b] = jnp.sum(e * v_ref[b], axis=0) / jnp.sum(e, axis=0)


def _ret_log_gamma(h):
    return math.log1p(-(2.0 ** (-5.0 - h)))


def _ab_prompt_body(z_ref, rv_ref, rg_ref, xbc_ref, rq_ref, rk_ref, dt_ref, alog_ref, dsk_ref, nrm_ref,
                    y_ref, h_ref, s_ref):
    c = pl.program_id(1)

    @pl.when(c == 0)
    def _():
        h_ref[...] = jnp.zeros_like(h_ref)
        s_ref[...] = jnp.zeros_like(s_ref)

    xs = xbc_ref[:, :SSD_D_INNER].astype(F32)
    bm = xbc_ref[:, SSD_D_INNER:SSD_D_INNER + GROUP_W]
    cm = xbc_ref[:, SSD_D_INNER + GROUP_W:]

    dt = dt_ref[...]
    da = dt * (-jnp.exp(alog_ref[...]))
    cs = _cumsum_rows(da)
    cs_t = cs.T
    row = lax.broadcasted_iota(jnp.int32, (CHUNK, CHUNK), 0)
    col = lax.broadcasted_iota(jnp.int32, (CHUNK, CHUNK), 1)
    tri = row >= col
    lo = col < SSD_HEAD_DIM
    for g in range(SSD_GROUPS):
        gs = slice(g * GROUP_W, (g + 1) * GROUP_W)
        ns = slice(g * SSD_D_STATE, (g + 1) * SSD_D_STATE)
        cmg = cm[:, ns]
        bmg = bm[:, ns]
        att = _dot_nt(cmg, bmg)
        hprev = h_ref[gs, :]
        yint = _dot_nt(cmg, _bf(hprev))
        ys, wxs, css = [], [], []
        for j in range(HEADS_PER_GROUP // 2):
            h0 = g * HEADS_PER_GROUP + 2 * j
            cb0 = _lane_bcast(cs, h0)
            cb1 = _lane_bcast(cs, h0 + 1)
            cs_p = jnp.where(lo, cb0, cb1)
            dt_p = jnp.where(lo, _lane_bcast(dt, h0), _lane_bcast(dt, h0 + 1))
            off = g * GROUP_W + j * LANES
            xdt = xs[:, off:off + LANES] * dt_p
            d0 = jnp.exp(jnp.where(tri, cb0 - cs_t[h0:h0 + 1, :], -jnp.inf))
            d1 = jnp.exp(jnp.where(tri, cb1 - cs_t[h0 + 1:h0 + 2, :], -jnp.inf))
            yy = _dot(jnp.concatenate([_bf(att * d0), _bf(att * d1)], axis=0), _bf(xdt))
            ys.append(jnp.where(lo, yy[:CHUNK], yy[CHUNK:]) + yint[:, j * LANES:(j + 1) * LANES] * jnp.exp(cs_p))
            wxs.append(_bf(xdt * jnp.exp(cs_p[CHUNK - 1:CHUNK, :] - cs_p)))
            css.append(cs_p)
        cs_g = jnp.concatenate(css, axis=1)
        last_t = jnp.broadcast_to(cs_g[CHUNK - 1:CHUNK, :], (CHUNK, GROUP_W)).T
        h_ref[gs, :] = hprev * jnp.exp(last_t) + _dot_tn(jnp.concatenate(wxs, axis=1), bmg)
        yg = jnp.concatenate(ys, axis=1)
        yg = (yg + xs[:, gs] * dsk_ref[:, gs]) * z_ref[:, gs].astype(F32)
        y_ref[:, gs] = _bf(_rms(yg) * nrm_ref[:, gs])

    tcol = row.astype(F32)
    diff = tcol - col.astype(F32)
    for h in range(RET_HEADS):
        lg = _ret_log_gamma(h)
        ks = slice(h * RET_DK, (h + 1) * RET_DK)
        vs = slice(h * RET_DV, (h + 1) * RET_DV)
        qb = rq_ref[:, ks]
        kb = rk_ref[:, ks]
        decay = jnp.exp(jnp.where(tri, diff * lg, -jnp.inf))
        att = _dot_nt(qb, kb) * decay
        vb = rv_ref[:, vs]
        s_prev = s_ref[h]
        inner = jnp.exp((tcol + 1.0) * lg)
        r = _dot(_bf(att), vb) + _dot(qb, _bf(s_prev)) * jnp.concatenate([inner] * (RET_DV // LANES), axis=1)
        tail_w = jnp.exp((CHUNK - 1.0 - tcol) * lg)
        kt = _bf(kb.astype(F32) * jnp.concatenate([tail_w] * (RET_DK // LANES), axis=1))
        s_ref[h] = s_prev * math.exp(CHUNK * lg) + _dot_tn(kt, vb)
        os = slice(SSD_D_INNER + h * RET_DV, SSD_D_INNER + (h + 1) * RET_DV)
        y_ref[:, os] = _bf(rg_ref[:, vs].astype(F32) * _rms(r))


def _mixer_ab_prompt(p_gates, p_xqk, p_dt, a_log, d_skip_e, ssd_norm):
    nc = SEQ // CHUNK
    m = BATCH * SEQ

    def rowspec(width, cb):
        return pl.BlockSpec((CHUNK, width), lambda b, c: (b * nc + c, cb))

    def full(a):
        return pl.BlockSpec(a.shape, lambda b, c: (0,) * a.ndim)

    params = [a_log, d_skip_e, ssd_norm]
    return pl.pallas_call(
        _ab_prompt_body,
        grid=(BATCH, nc),
        in_specs=[rowspec(SSD_D_INNER, 0), rowspec(RET_V, 1), rowspec(RET_V, 2), rowspec(SSD_CONV_CH, 0),
                  rowspec(RET_QK, SSD_CONV_CH // RET_QK), rowspec(RET_QK, SSD_CONV_CH // RET_QK + 1),
                  rowspec(LANES, 0)] + [full(a) for a in params],
        out_specs=[
            pl.BlockSpec((CHUNK, SSD_D_INNER + RET_V), lambda b, c: (b * nc + c, 0)),
            pl.BlockSpec((None, SSD_D_INNER, SSD_D_STATE), lambda b, c: (b, 0, 0)),
            pl.BlockSpec((None, RET_HEADS, RET_DK, RET_DV), lambda b, c: (b, 0, 0, 0)),
        ],
        out_shape=[
            jax.ShapeDtypeStruct((m, SSD_D_INNER + RET_V), BF16),
            jax.ShapeDtypeStruct((BATCH, SSD_D_INNER, SSD_D_STATE), F32),
            jax.ShapeDtypeStruct((BATCH, RET_HEADS, RET_DK, RET_DV), F32),
        ],
        compiler_params=_cp(("parallel", "arbitrary"), 48),
        name="mixer_ab_prompt",
    )(p_gates, p_gates, p_gates, p_xqk, p_xqk, p_xqk, p_dt, *params)


def _c_prompt_body(q_ref, k_ref, v_ref, o_ref, gt_ref, ib_ref, fb_ref, nrm_ref,
                   h_ref, c_ref, n_ref, m_ref, m_s):
    @pl.when(pl.program_id(1) == 0)
    def _():
        c_ref[...] = jnp.zeros_like(c_ref)
        n_ref[...] = jnp.zeros_like(n_ref)
        m_s[...] = jnp.zeros_like(m_s)

    row = lax.broadcasted_iota(jnp.int32, (CHUNK, CHUNK), 0)
    col = lax.broadcasted_iota(jnp.int32, (CHUNK, CHUNK), 1)
    tri = row >= col
    seqs = range(CB)
    ipre = [gt_ref[s, :, :LANES] + ib_ref[...] for s in seqs]
    lf = [-_softplus(-(gt_ref[s, :, LANES:] + fb_ref[...])) for s in seqs]
    b = [_cumsum_rows(lf[s]) for s in seqs]
    g = [ipre[s] - b[s] for s in seqs]
    g_t = [g[s].T for s in seqs]
    b_t = [b[s].T for s in seqs]
    cmax = list(g_t)
    sh = 1
    while sh < CHUNK:
        cmax = [jnp.maximum(cmax[s], jnp.where(col >= sh, pltpu.roll(cmax[s], sh, axis=1), -jnp.inf)) for s in seqs]
        sh *= 2
    m_prev = [m_s[s] for s in seqs]
    mt_t = [b_t[s] + jnp.maximum(m_prev[s], cmax[s]) for s in seqs]
    mt = [mt_t[s].T for s in seqs]
    m_prev_c = [m_prev[s].T for s in seqs]
    inter = [jnp.exp(b[s] + m_prev_c[s] - mt[s]) for s in seqs]
    emt = [jnp.exp(-mt[s]) for s in seqs]
    wl = [jnp.exp(g[s] + b[s][CHUNK - 1:CHUNK, :] - mt[s][CHUNK - 1:CHUNK, :]) for s in seqs]
    bm = [b[s] - mt[s] for s in seqs]
    m_new = [_lane_bcast(mt_t[s], CHUNK - 1) for s in seqs]
    dp_t = [jnp.exp(_lane_bcast(b_t[s], CHUNK - 1) + m_prev[s] - m_new[s]) for s in seqs]
    for s in seqs:
        m_s[s] = m_new[s]
        m_ref[s] = m_new[s][0:SUBLANES, :]
    for h in range(M_HEADS):
        ks = slice(h * M_DK, (h + 1) * M_DK)
        vs = slice(h * M_DV, (h + 1) * M_DV)
        for s in seqs:
            wgt = jnp.exp(jnp.where(tri, g_t[s][h:h + 1, :] + bm[s][:, h:h + 1], -jnp.inf))
            qb = q_ref[s, :, ks]
            kb = k_ref[s, :, ks]
            vb = v_ref[s, :, vs]
            a = _dot_nt(qb, kb) * wgt
            c_prev = c_ref[s, h]
            n_prev = n_ref[s, h:h + 1, :]
            ic = inter[s][:, h:h + 1]
            num = _dot(_bf(a), vb) + _dot(qb, _bf(c_prev)) * ic
            den = (jnp.sum(a, axis=1, keepdims=True)
                   + jnp.sum(qb.astype(F32) * n_prev, axis=1, keepdims=True) * ic)
            hc = num / jnp.maximum(jnp.abs(den), emt[s][:, h:h + 1])
            kw = kb.astype(F32) * wl[s][:, h:h + 1]
            dp_row = dp_t[s][h:h + 1, :]
            c_ref[s, h] = c_prev * jnp.concatenate([dp_row] * (M_DV // LANES), axis=1) + _dot_tn(_bf(kw), vb)
            n_ref[s, h:h + 1, :] = n_prev * dp_row + jnp.sum(kw, axis=0, keepdims=True)
            h_ref[s, :, vs] = _bf(o_ref[s, :, vs].astype(F32) * (_rms(hc) * nrm_ref[:, vs]))


def _mixer_c_prompt(p_main, p_gate, i_bias, f_bias, norm_g):
    nc = SEQ // CHUNK

    def rowspec(width, cb):
        return pl.BlockSpec((CB, CHUNK, width), lambda b, c: (b, c, cb))

    def full(a):
        return pl.BlockSpec(a.shape, lambda b, c: (0,) * a.ndim)

    params = [i_bias, f_bias, norm_g]
    return pl.pallas_call(
        _c_prompt_body,
        grid=(BATCH // CB, nc),
        in_specs=[rowspec(M_QK, 0), rowspec(M_QK, 1), rowspec(M_V, 1), rowspec(M_V, 2), rowspec(2 * LANES, 0)]
        + [full(a) for a in params],
        out_specs=[
            pl.BlockSpec((CB, CHUNK, M_V), lambda b, c: (b, c, 0)),
            pl.BlockSpec((CB, M_HEADS, M_DK, M_DV), lambda b, c: (b, 0, 0, 0)),
            pl.BlockSpec((CB, M_HEADS, M_DK), lambda b, c: (b, 0, 0)),
            pl.BlockSpec((CB, SUBLANES, LANES), lambda b, c: (b, 0, 0)),
        ],
        out_shape=[
            jax.ShapeDtypeStruct((BATCH, SEQ, M_V), BF16),
            jax.ShapeDtypeStruct((BATCH, M_HEADS, M_DK, M_DV), F32),
            jax.ShapeDtypeStruct((BATCH, M_HEADS, M_DK), F32),
            jax.ShapeDtypeStruct((BATCH, SUBLANES, LANES), F32),
        ],
        scratch_shapes=[pltpu.VMEM((CB, CHUNK, LANES), F32)],
        compiler_params=_cp(("parallel", "arbitrary"), 48),
        name="mixer_c_prompt",
    )(p_main, p_main, p_main, p_main, p_gate, *params)


def _ab_sample_prep_body(xbc_ref, rq_ref, rk_ref, dt_ref, cst_ref, cos_ref, sin_ref,
                         cw_ref, cb_ref, dtb_ref, alog_ref,
                         conv_ref, xs_ref, xdt_ref, eda_ref, bm_ref, cm_ref, q_ref, k_ref, gam_ref):
    ch = SSD_CONV_CH
    u = xbc_ref[...]
    w = cw_ref[...]
    b0 = cst_ref[:, 0:ch]
    b1 = cst_ref[:, ch:2 * ch]
    b2 = cst_ref[:, 2 * ch:3 * ch]
    conv = cb_ref[...] + (((b0 * w[0:1, :] + b1 * w[1:2, :]) + b2 * w[2:3, :]) + u * w[3:4, :])
    conv_ref[:, 0:ch] = b1
    conv_ref[:, ch:2 * ch] = b2
    conv_ref[:, 2 * ch:3 * ch] = u
    xbc = _silu(conv)
    xs = xbc[:, :SSD_D_INNER]
    xs_ref[...] = xs
    bm_ref[...] = xbc[:, SSD_D_INNER:SSD_D_INNER + GROUP_W]
    cm_ref[...] = xbc[:, SSD_D_INNER + GROUP_W:]
    dt = _softplus(dt_ref[...] + dtb_ref[...])
    eda = jnp.exp(dt * (-jnp.exp(alog_ref[...])))
    xdt_ref[...] = xs * _pair_expand(dt, SSD_HEADS)
    eda_ref[...] = _pair_expand(eda, SSD_HEADS)
    cos = cos_ref[...]
    sin = sin_ref[...]
    for h in range(RET_HEADS):
        a = slice(h * RET_DK, h * RET_DK + LANES)
        b = slice(h * RET_DK + LANES, (h + 1) * RET_DK)
        q1, q2 = rq_ref[:, a], rq_ref[:, b]
        k1, k2 = rk_ref[:, a], rk_ref[:, b]
        q_ref[:, a] = q1 * cos - q2 * sin
        q_ref[:, b] = q1 * sin + q2 * cos
        k_ref[:, a] = (k1 * cos - k2 * sin) * (RET_DK ** -0.5)
        k_ref[:, b] = (k1 * sin + k2 * cos) * (RET_DK ** -0.5)
        gam_ref[:, h * RET_DK:(h + 1) * RET_DK] = jnp.full((DEC_BATCH, RET_DK), math.exp(_ret_log_gamma(h)), F32)


def _ab_sample_prep(p_main, p_dt, conv_state, cos, sin, conv_w, conv_b, dt_bias, a_log):
    n = DEC_BATCH

    def colspec(width, cb):
        return pl.BlockSpec((n, width), lambda i: (0, cb))

    def full(a):
        return pl.BlockSpec(a.shape, lambda i: (0,) * a.ndim)

    small = [conv_state, cos, sin, conv_w, conv_b, dt_bias, a_log]

    def out(width):
        return jax.ShapeDtypeStruct((n, width), F32)

    widths = [(SSD_CONV - 1) * SSD_CONV_CH, SSD_D_INNER, SSD_D_INNER, SSD_D_INNER, GROUP_W, GROUP_W,
              RET_QK, RET_QK, RET_QK]
    return pl.pallas_call(
        _ab_sample_prep_body,
        grid=(1,),
        in_specs=[colspec(SSD_CONV_CH, 2), colspec(RET_QK, 9), colspec(RET_QK, 10), full(p_dt)]
        + [full(a) for a in small],
        out_specs=[pl.BlockSpec((n, wd), lambda i: (0, 0)) for wd in widths],
        out_shape=[out(wd) for wd in widths],
        compiler_params=_cp(("arbitrary",), 48),
        name="ab_sample_prep",
    )(p_main, p_main, p_main, p_dt, *small)


def _ssd_state_body(eda_ref, xdt_ref, bm_ref, cm_ref, h_ref, ho_ref, y_ref):
    eda_t = _pad_t(eda_ref[...])
    xdt_t = _pad_t(xdt_ref[...])
    cm = _bf(cm_ref[...])
    for b in range(SSD_SB):
        hn = h_ref[b] * eda_t[:, b:b + 1] + xdt_t[:, b:b + 1] * bm_ref[b:b + 1, :]
        ho_ref[b] = hn
        y_ref[b:b + 1, :] = _dot_nt(cm, _bf(hn))[b:b + 1, :]


def _ssd_state(eda, xdt, bm, cm, state):
    vec = pl.BlockSpec((SSD_SB, GROUP_W), lambda i, g: (i, g))
    bc = pl.BlockSpec((SSD_SB, SSD_D_STATE), lambda i, g: (i, g))
    st = pl.BlockSpec((SSD_SB, GROUP_W, SSD_D_STATE), lambda i, g: (i, g, 0))
    return pl.pallas_call(
        _ssd_state_body,
        grid=(DEC_BATCH // SSD_SB, SSD_GROUPS),
        in_specs=[vec, vec, bc, bc, st],
        out_specs=[st, vec],
        out_shape=[jax.ShapeDtypeStruct(state.shape, F32), jax.ShapeDtypeStruct((DEC_BATCH, SSD_D_INNER), F32)],
        compiler_params=_cp(("parallel", "arbitrary"), 48),
        name="ssd_state",
    )(eda, xdt, bm, cm, state)


def _outer_state_body(d_ref, k_ref, q_ref, v_ref, s_ref, so_ref, o_ref):
    d_t = _pad_t(d_ref[...])
    k_t = _pad_t(k_ref[...])
    q_t = _pad_t(q_ref[...])
    for b in range(d_ref.shape[0]):
        sn = s_ref[b] * d_t[:, b:b + 1] + k_t[:, b:b + 1] * v_ref[b:b + 1, :]
        so_ref[b] = sn
        o_ref[b:b + 1, :] = jnp.sum(sn * q_t[:, b:b + 1], axis=0, keepdims=True)


def _outer_state(d, k, q, v, state):
    _, nh, dk, dv = state.shape
    sb = STATE_BLOCK_BYTES // (dk * dv * 4)
    kv = pl.BlockSpec((sb, dk), lambda i, h: (i, h))
    vv = pl.BlockSpec((sb, dv), lambda i, h: (i, h))
    st = pl.BlockSpec((sb, None, dk, dv), lambda i, h: (i, h, 0, 0))
    return pl.pallas_call(
        _outer_state_body,
        grid=(DEC_BATCH // sb, nh),
        in_specs=[kv, kv, kv, vv, st],
        out_specs=[st, vv],
        out_shape=[jax.ShapeDtypeStruct(state.shape, F32), jax.ShapeDtypeStruct((DEC_BATCH, nh * dv), F32)],
        compiler_params=_cp(("parallel", "arbitrary"), 48),
        name="outer_state",
    )(d, k, q, v, state)


def _ab_sample_post_body(y_ref, xs_ref, z_ref, r_ref, rg_ref, dsk_ref, nrm_ref, o_ref):
    for g in range(SSD_GROUPS):
        gs = slice(g * GROUP_W, (g + 1) * GROUP_W)
        yg = (y_ref[:, gs] + xs_ref[:, gs] * dsk_ref[:, gs]) * _silu(z_ref[:, gs])
        o_ref[:, gs] = _bf(_rms(yg) * nrm_ref[:, gs])
    for h in range(RET_HEADS):
        vs = slice(h * RET_DV, (h + 1) * RET_DV)
        os = slice(SSD_D_INNER + h * RET_DV, SSD_D_INNER + (h + 1) * RET_DV)
        o_ref[:, os] = _bf(_silu(rg_ref[:, vs]) * _rms(r_ref[:, vs]))


def _ab_sample_post(y, xs, p_main, r, d_skip_e, ssd_norm):
    n = DEC_BATCH

    def full(a):
        return pl.BlockSpec(a.shape, lambda i: (0,) * a.ndim)

    return pl.pallas_call(
        _ab_sample_post_body,
        grid=(1,),
        in_specs=[full(y), full(xs), pl.BlockSpec((n, SSD_D_INNER), lambda i: (0, 0)), full(r),
                  pl.BlockSpec((n, RET_V), lambda i: (0, 2)), full(d_skip_e), full(ssd_norm)],
        out_specs=pl.BlockSpec((n, SSD_D_INNER + RET_V), lambda i: (0, 0)),
        out_shape=jax.ShapeDtypeStruct((n, SSD_D_INNER + RET_V), BF16),
        compiler_params=_cp(("arbitrary",), 48),
        name="ab_sample_post",
    )(y, xs, p_main, r, p_main, d_skip_e, ssd_norm)


def _c_sample_prep_body(q_ref, k_ref, gt_ref, n_ref, m_ref, ib_ref, fb_ref,
                        dpe_ref, kw_ref, nn_ref, mn_ref, dn_ref):
    ipre = gt_ref[:, :LANES] + ib_ref[...]
    lf = -_softplus(-(gt_ref[:, LANES:] + fb_ref[...]))
    m_prev = m_ref[...]
    mt = jnp.maximum(lf + m_prev, ipre)
    wgt = jnp.exp(ipre - mt)
    dp = jnp.exp(lf + m_prev - mt)
    emt = jnp.exp(-mt)
    mn_ref[...] = mt
    for h in range(M_HEADS):
        ks = slice(h * M_DK, (h + 1) * M_DK)
        dpe = _lane_bcast(dp, h)
        kw = k_ref[:, ks] * (M_DK ** -0.5) * _lane_bcast(wgt, h)
        nn = n_ref[:, ks] * dpe + kw
        den = jnp.sum(nn * q_ref[:, ks], axis=1, keepdims=True)
        dpe_ref[:, ks] = dpe
        kw_ref[:, ks] = kw
        nn_ref[:, ks] = nn
        dn_ref[:, h * M_DV:(h + 1) * M_DV] = jnp.broadcast_to(
            jnp.maximum(jnp.abs(den), emt[:, h:h + 1]), (DEC_BATCH, M_DV))


def _c_sample_prep(p_main, p_gate, n_state, m_state, i_bias, f_bias):
    n = DEC_BATCH

    def full(a):
        return pl.BlockSpec(a.shape, lambda i: (0,) * a.ndim)

    widths = [M_QK, M_QK, M_QK, LANES, M_V]
    return pl.pallas_call(
        _c_sample_prep_body,
        grid=(1,),
        in_specs=[pl.BlockSpec((n, M_QK), lambda i: (0, 0)), pl.BlockSpec((n, M_QK), lambda i: (0, 1)),
                  full(p_gate), full(n_state), full(m_state), full(i_bias), full(f_bias)],
        out_specs=[pl.BlockSpec((n, wd), lambda i: (0, 0)) for wd in widths],
        out_shape=[jax.ShapeDtypeStruct((n, wd), F32) for wd in widths],
        compiler_params=_cp(("arbitrary",), 48),
        name="c_sample_prep",
    )(p_main, p_main, p_gate, n_state, m_state, i_bias, f_bias)


def _c_sample_post_body(num_ref, dn_ref, o_ref, nrm_ref, h_ref):
    for h in range(M_HEADS):
        vs = slice(h * M_DV, (h + 1) * M_DV)
        hc = num_ref[:, vs] / dn_ref[:, vs]
        h_ref[:, vs] = _bf(jax.nn.sigmoid(o_ref[:, vs]) * (_rms(hc) * nrm_ref[:, vs]))


def _c_sample_post(num, den, p_main, norm_g):
    n = DEC_BATCH

    def full(a):
        return pl.BlockSpec(a.shape, lambda i: (0,) * a.ndim)

    return pl.pallas_call(
        _c_sample_post_body,
        grid=(1,),
        in_specs=[full(num), full(den), pl.BlockSpec((n, M_V), lambda i: (0, 2)), full(norm_g)],
        out_specs=pl.BlockSpec((n, M_V), lambda i: (0, 0)),
        out_shape=jax.ShapeDtypeStruct((n, M_V), BF16),
        compiler_params=_cp(("arbitrary",), 48),
        name="c_sample_post",
    )(num, den, p_main, norm_g)


def _rope_tables(pos):
    half = RET_DK // 2
    inv = jnp.exp(-math.log(ROPE_BASE) * jnp.arange(half, dtype=F32) / half)
    ang = pos.astype(F32)[:, None] * inv
    return jnp.cos(ang), jnp.sin(ang)


def _pad_lanes(v, width=LANES):
    return jnp.pad(v.reshape(1, -1), ((0, 0), (0, width - v.size)))


def kernel(x_prompt, x_sample, cache_mem_k, cache_mem_v, state_conv, state_ssm, state_ret, state_mlstm_c, state_mlstm_n, state_mlstm_m, mem_prompt, norm_ffn1, w_ffn1_in, w_ffn1_out, norm_mix, w_in_ab, ssd_conv_w, ssd_conv_b, ssd_dt_bias, ssd_a_log, ssd_d, ssd_norm, w_out_ab, w_in_c, mlstm_i_bias, mlstm_f_bias, mlstm_norm, w_out_c, norm_xattn, norm_mem, w_xq, w_xkv, w_xo, norm_ffn2, w_ffn2_in, w_ffn2_out, norm_final):
    g3 = lambda g: g.reshape(DEPTH, 1, D_MODEL)
    n_ffn1, n_mix, n_x, n_mem, n_ffn2 = g3(norm_ffn1), g3(norm_mix), g3(norm_xattn), g3(norm_mem), g3(norm_ffn2)

    f1_in, f1_out, f2_in, f2_out = (w.astype(BF16) for w in (w_ffn1_in, w_ffn1_out, w_ffn2_in, w_ffn2_out))
    w_out_ab, w_out_c, w_xq, w_xkv, w_xo = (w.astype(BF16) for w in (w_out_ab, w_out_c, w_xq, w_xkv, w_xo))

    wz, wxbc, wdt, wrq, wrk, wrv, wrg = jnp.split(w_in_ab[0], np_cumsum(AB_SIZES), axis=1)
    w_ab_main = jnp.concatenate([wz, wrv, wrg, wxbc, wrq, wrk], axis=1).astype(BF16)[None]
    w_ab_dt = jnp.pad(wdt, ((0, 0), (0, LANES - SSD_HEADS))).astype(BF16)[None]
    wq, wk, wv, wi, wf, wo = jnp.split(w_in_c[0], np_cumsum(C_SIZES), axis=1)
    w_c_main = jnp.concatenate([wq, wk, wv, wo], axis=1).astype(BF16)[None]
    gpad = ((0, 0), (0, LANES - M_HEADS))
    w_c_gate = jnp.concatenate([jnp.pad(wi, gpad), jnp.pad(wf, gpad)], axis=1).astype(BF16)[None]

    conv_w = ssd_conv_w[0]
    conv_b = ssd_conv_b.reshape(1, SSD_CONV_CH)
    dt_bias = _pad_lanes(ssd_dt_bias[0])
    a_log = _pad_lanes(ssd_a_log[0])
    d_skip_e = jnp.repeat(ssd_d[0], SSD_HEAD_DIM).reshape(1, SSD_D_INNER)
    s_norm = ssd_norm.reshape(1, SSD_D_INNER)
    i_bias = _pad_lanes(mlstm_i_bias[0])
    f_bias = _pad_lanes(mlstm_f_bias[0])
    m_norm = mlstm_norm.reshape(1, M_V)

    memk, memv, memk4, memv4 = _mem_kv(mem_prompt.reshape(BATCH * MEM_LEN, D_MODEL), n_mem, w_xkv)
    cos_p, sin_p = _rope_tables(jnp.arange(SEQ))
    x = x_prompt.reshape(BATCH * SEQ, D_MODEL)

    cos_s, sin_s = _rope_tables(PAST_LEN + jnp.arange(1))
    xs_ = x_sample.reshape(DEC_BATCH, D_MODEL)
    xs_ = _ffn(xs_, n_ffn1, f1_in, f1_out, 0)
    sp_main, sp_dt = _norm_proj(xs_, n_mix, 0, w_ab_main, 0, 1024, w_ab_dt)
    conv_s, xs_c, xdt, eda, bm_s, cm_s, q_s, k_s, gam = _ab_sample_prep(
        sp_main, sp_dt, state_conv.reshape(DEC_BATCH, (SSD_CONV - 1) * SSD_CONV_CH), cos_s, sin_s,
        conv_w, conv_b, dt_bias, a_log)
    ssm_s, y_s = _ssd_state(eda, xdt, bm_s, cm_s, state_ssm.reshape(DEC_BATCH, SSD_D_INNER, SSD_D_STATE))
    ret_s, r_s = _outer_state(gam, k_s, q_s, sp_main[:, SSD_D_INNER:SSD_D_INNER + RET_V], state_ret[0])
    ycat_s = _ab_sample_post(y_s, xs_c, sp_main, r_s, d_skip_e, s_norm)
    xs_ = _proj_residual(xs_, ycat_s, w_out_ab, 0)
    q_x0 = _norm_proj(xs_, n_x, 0, w_xq, 0, 1024).reshape(DEC_BATCH, X_HEADS, X_HEAD_DIM)

    x, o_x0 = _ffn(x, n_ffn1, f1_in, f1_out, 0, host=(q_x0, cache_mem_k, cache_mem_v, 0))

    xs_ = _proj_residual(xs_, o_x0.reshape(DEC_BATCH, D_MODEL), w_xo, 0)
    xs_ = _ffn(xs_, n_ffn2, f2_in, f2_out, 0)
    xs_ = _ffn(xs_, n_ffn1, f1_in, f1_out, 1)
    sc_main, sc_gate = _norm_proj(xs_, n_mix, 1, w_c_main, 0, 1024, w_c_gate)
    m_in = jnp.pad(state_mlstm_m[0], ((0, 0), (0, LANES - M_HEADS)))
    dpe, kw, mn_s, mm_s, den = _c_sample_prep(sc_main, sc_gate, state_mlstm_n.reshape(DEC_BATCH, M_QK), m_in,
                                              i_bias, f_bias)
    mc_s, num = _outer_state(dpe, kw, sc_main[:, :M_QK], sc_main[:, 2 * M_QK:2 * M_QK + M_V], state_mlstm_c[0])
    hout_s = _c_sample_post(num, den, sc_main, m_norm)
    xs_ = _proj_residual(xs_, hout_s, w_out_c, 0)
    q_x1 = _norm_proj(xs_, n_x, 1, w_xq, 1, 1024).reshape(DEC_BATCH, X_HEADS, X_HEAD_DIM)

    w_gates =w_ab_main[0, :, :AB_GATE_BLOCKS * AB_TN]
    w_xqk = w_ab_main[0, :, AB_GATE_BLOCKS * AB_TN:]
    p_gates, p_xqk, p_dt = _ab_inproj(x, n_mix, w_gates, w_xqk, w_ab_dt[0], dt_bias, conv_w, conv_b, cos_p, sin_p)
    ycat, ssm_p, ret_p = _mixer_ab_prompt(p_gates, p_xqk, p_dt, a_log, d_skip_e, s_norm)
    x_tail = x.reshape(BATCH, SEQ, D_MODEL)[:, SEQ - SSD_CONV:].reshape(BATCH * SSD_CONV, D_MODEL)
    conv_p = _norm_proj(x_tail, n_mix, 0, w_ab_main, 0, AB_TN, col0=AB_XBC0, n_out=SSD_CONV_CH)
    conv_p = conv_p.reshape(BATCH, SSD_CONV, SSD_CONV_CH)[:, 1:]
    x = _proj_residual(x, ycat, w_out_ab, 0)
    x = _xattn_prompt(x, n_x, w_xq, w_xo, memk, memv, 0)
    x, o_x1 = _ffn(x, n_ffn2, f2_in, f2_out, 0, host=(q_x1, cache_mem_k, cache_mem_v, 1))
    xs_ = _proj_residual(xs_, o_x1.reshape(DEC_BATCH, D_MODEL), w_xo, 1)
    y_sample = _ffn(xs_, n_ffn2, f2_in, f2_out, 1, norm_final).reshape(DEC_BATCH, 1, D_MODEL)
    x = _ffn(x, n_ffn1, f1_in, f1_out, 1)
    pc_main, pc_gate = _c_inproj(x, n_mix, 1, w_c_main, w_c_gate)
    hout, mc_p, mn_p, mm_p = _mixer_c_prompt(pc_main.reshape(BATCH, SEQ, -1), pc_gate.reshape(BATCH, SEQ, -1),
                                             i_bias, f_bias, m_norm)
    x = _proj_residual(x, hout.reshape(BATCH * SEQ, M_V), w_out_c, 0)
    x = _xattn_prompt(x, n_x, w_xq, w_xo, memk, memv, 1)
    y_prompt = _ffn(x, n_ffn2, f2_in, f2_out, 1, norm_final).reshape(BATCH, SEQ, D_MODEL)

    kv_shape = (DEPTH, BATCH, MEM_LEN, X_HEADS, X_HEAD_DIM)
    return (y_prompt, y_sample, memk4.reshape(kv_shape), memv4.reshape(kv_shape),
            conv_p.reshape(1, BATCH, SSD_CONV - 1, SSD_CONV_CH),
            conv_s.reshape(1, DEC_BATCH, SSD_CONV - 1, SSD_CONV_CH),
            ssm_p.reshape(1, BATCH, SSD_HEADS, SSD_HEAD_DIM, SSD_D_STATE),
            ssm_s.reshape(1, DEC_BATCH, SSD_HEADS, SSD_HEAD_DIM, SSD_D_STATE),
            ret_p[None], ret_s[None], mc_p[None], mc_s[None],
            mn_p[None], mn_s.reshape(1, DEC_BATCH, M_HEADS, M_DK),
            mm_p[:, :M_HEADS, 0][None], mm_s[:, :M_HEADS][None])


def np_cumsum(sizes):
    out, acc = [], 0
    for s in sizes[:-1]:
        acc += s
        out.append(acc)
    return out
```

```python
import functools
import math

import jax
import jax.numpy as jnp
from jax import lax
from jax.experimental import pallas as pl
from jax.experimental.pallas import tpu as pltpu

F32 = jnp.float32
BF16 = jnp.bfloat16
EPS = 1e-6

D_MODEL = 1024
BATCH = 8
SEQ = 2048
DEPTH = 2
DEC_BATCH = 128
PAST_LEN = 16384
CHUNK = 128
D_FF = 2816
SSD_D_INNER = 2 * D_MODEL
SSD_HEAD_DIM = 64
SSD_HEADS = SSD_D_INNER // SSD_HEAD_DIM
SSD_GROUPS = 4
SSD_D_STATE = 128
SSD_CONV = 4
SSD_CONV_CH = SSD_D_INNER + 2 * SSD_GROUPS * SSD_D_STATE
RET_HEADS = 4
RET_QK = D_MODEL
RET_V = 2 * D_MODEL
RET_DK = RET_QK // RET_HEADS
RET_DV = RET_V // RET_HEADS
ROPE_BASE = 10000.0
AB_SIZES = (SSD_D_INNER, SSD_CONV_CH, SSD_HEADS, RET_QK, RET_QK, RET_V, RET_V)
M_HEADS = 4
M_QK = D_MODEL // 2
M_V = D_MODEL
M_DK = M_QK // M_HEADS
M_DV = M_V // M_HEADS
C_SIZES = (M_QK, M_QK, M_V, M_HEADS, M_HEADS, M_V)
MEM_LEN = 256
X_HEADS = 4
X_HEAD_DIM = D_MODEL // X_HEADS

LANES = 128
SUBLANES = 8
GROUP_W = SSD_D_INNER // SSD_GROUPS
HEADS_PER_GROUP = SSD_HEADS // SSD_GROUPS
ROW_TILE = 1024
FF_TILE = 256
SSD_SB = 16
STATE_BLOCK_BYTES = 8 * 1024 * 1024
XB = 4
CB = 2


def _cp(sem, mib):
    return pltpu.CompilerParams(dimension_semantics=sem, vmem_limit_bytes=mib * 1024 * 1024)


def _bf(x):
    return x.astype(BF16)


def _dot(a, b):
    return jnp.dot(a, b, preferred_element_type=F32)


def _dot_nt(a, b):
    return lax.dot_general(a, b, (((1,), (1,)), ((), ())), preferred_element_type=F32)


def _dot_tn(a, b):
    return lax.dot_general(a, b, (((0,), (0,)), ((), ())), preferred_element_type=F32)


def _rms(x):
    return x * lax.rsqrt(jnp.mean(x * x, axis=-1, keepdims=True) + EPS)


def _silu(x):
    return x * jax.nn.sigmoid(x)


def _softplus(x):
    return jnp.maximum(x, 0.0) + jnp.log1p(jnp.exp(-jnp.abs(x)))


def _split3(x):
    hi = x.astype(BF16)
    r = x - hi.astype(F32)
    mid = r.astype(BF16)
    lo = (r - mid.astype(F32)).astype(BF16)
    return hi, mid, lo


def _cumsum_rows(x):
    n = x.shape[0]
    r = lax.broadcasted_iota(jnp.int32, (n, n), 0)
    c = lax.broadcasted_iota(jnp.int32, (n, n), 1)
    t = jnp.where(r >= c, 1.0, 0.0).astype(BF16)
    hi, mid, lo = _split3(x)
    return _dot(t, hi) + _dot(t, mid) + _dot(t, lo)


def _lane_bcast(x, h, width=LANES):
    return jnp.broadcast_to(x[:, h:h + 1], (x.shape[0], width))


def _pair_expand(x, n_heads):
    rows = x.shape[0]
    lo = lax.broadcasted_iota(jnp.int32, (rows, LANES), 1) < SSD_HEAD_DIM
    return jnp.concatenate(
        [jnp.where(lo, _lane_bcast(x, 2 * j), _lane_bcast(x, 2 * j + 1)) for j in range(n_heads // 2)], axis=1)


def _pad_t(x):
    pad = jnp.zeros((LANES - x.shape[0], x.shape[1]), F32)
    return jnp.concatenate([x, pad], axis=0).T


def _ffn_body(*refs, nf, final, hosted):
    x_ref, g_ref, wi_ref, wo_ref = refs[:4]
    rest = list(refs[4:])
    fg_ref = rest.pop(0) if final else None
    if hosted:
        q_ref, k_ref, v_ref = rest[:3]
        o_ref, xo_ref = rest[3:]
    else:
        o_ref, = rest
    x = x_ref[...]
    xn = _bf(_rms(x) * g_ref[...])
    acc = None
    for f in range(nf):
        fs = slice(f * FF_TILE, (f + 1) * FF_TILE)
        g = _dot(xn, wi_ref[:, fs])
        u = _dot(xn, wi_ref[:, D_FF + f * FF_TILE:D_FF + (f + 1) * FF_TILE])
        t = _dot(_bf(_silu(g) * u), wo_ref[fs, :])
        acc = t if acc is None else acc + t
        if hosted and f % 3 == 1 and f // 3 < XB:
            _xattn_sample_one(f // 3, q_ref, k_ref, v_ref, xo_ref)
    y = x + 0.5 * acc
    if final:
        y = _rms(y) * fg_ref[...]
    o_ref[...] = y


def _ffn(x, g3, w_in, w_out, layer, final_g=None, host=None):
    m = x.shape[0]
    tm = min(m, ROW_TILE) if host is None else m // (DEC_BATCH // XB)
    nf = D_FF // FF_TILE
    once = pl.Buffered(1)
    in_specs = [
        pl.BlockSpec((tm, D_MODEL), lambda i: (i, 0)),
        pl.BlockSpec((None, 1, D_MODEL), lambda i: (layer, 0, 0)),
        pl.BlockSpec((None, D_MODEL, 2 * D_FF), lambda i: (layer, 0, 0), pipeline_mode=once),
        pl.BlockSpec((None, D_FF, D_MODEL), lambda i: (layer, 0, 0), pipeline_mode=once),
    ]
    args = [x, g3, w_in, w_out]
    if final_g is not None:
        in_specs.append(pl.BlockSpec((1, D_MODEL), lambda i: (0, 0)))
        args.append(final_g.reshape(1, D_MODEL))
    out_specs = [pl.BlockSpec((tm, D_MODEL), lambda i: (i, 0))]
    out_shape = [jax.ShapeDtypeStruct((m, D_MODEL), F32)]
    if host is not None:
        q, cache_k, cache_v, cl = host
        blk = pl.BlockSpec((None, XB, MEM_LEN, X_HEADS, X_HEAD_DIM), lambda i: (cl, i, 0, 0, 0))
        qo = pl.BlockSpec((XB, X_HEADS, X_HEAD_DIM), lambda i: (i, 0, 0))
        in_specs += [qo, blk, blk]
        args += [q, cache_k, cache_v]
        out_specs.append(qo)
        out_shape.append(jax.ShapeDtypeStruct((DEC_BATCH, X_HEADS, X_HEAD_DIM), F32))
    res = pl.pallas_call(
        functools.partial(_ffn_body, nf=nf, final=final_g is not None, hosted=host is not None),
        grid=(m // tm,),
        in_specs=in_specs,
        out_specs=out_specs,
        out_shape=out_shape,
        compiler_params=_cp(("parallel",), 56),
        name="ffn",
    )(*args)
    return res if host is not None else res[0]


def _norm_proj_body(*refs, small):
    if small:
        x_ref, g_ref, w_ref, ws_ref, o_ref, os_ref, xn_ref = refs
    else:
        x_ref, g_ref, w_ref, o_ref, xn_ref = refs
    n = pl.program_id(1)

    @pl.when(n == 0)
    def _():
        xn = _bf(_rms(x_ref[...]) * g_ref[...])
        xn_ref[...] = xn
        if small:
            os_ref[...] = _dot(xn, _bf(ws_ref[...]))

    o_ref[...] = _dot(xn_ref[...], _bf(w_ref[...]))


def _norm_proj(x, g3, glayer, w3, wlayer, tn, w_small=None, col0=0, n_out=None):
    m = x.shape[0]
    tm = min(m, ROW_TILE)
    n_out = w3.shape[-1] if n_out is None else n_out
    in_specs = [
        pl.BlockSpec((tm, D_MODEL), lambda i, n: (i, 0)),
        pl.BlockSpec((None, 1, D_MODEL), lambda i, n: (glayer, 0, 0)),
        pl.BlockSpec((None, D_MODEL, tn), lambda i, n: (wlayer, 0, n + col0)),
    ]
    args = [x, g3, w3]
    out_specs = [pl.BlockSpec((tm, tn), lambda i, n: (i, n))]
    out_shape = [jax.ShapeDtypeStruct((m, n_out), F32)]
    if w_small is not None:
        ns = w_small.shape[-1]
        in_specs.append(pl.BlockSpec((None, D_MODEL, ns), lambda i, n: (0, 0, 0)))
        args.append(w_small)
        out_specs.append(pl.BlockSpec((tm, ns), lambda i, n: (i, 0)))
        out_shape.append(jax.ShapeDtypeStruct((m, ns), F32))
    res = pl.pallas_call(
        functools.partial(_norm_proj_body, small=w_small is not None),
        grid=(m // tm, n_out // tn),
        in_specs=in_specs,
        out_specs=out_specs,
        out_shape=out_shape,
        scratch_shapes=[pltpu.VMEM((tm, D_MODEL), BF16)],
        compiler_params=_cp(("parallel", "arbitrary"), 48),
        name="norm_proj",
    )(*args)
    return res if w_small is not None else res[0]


AB_TN = 1024
AB_XBC0 = (SSD_D_INNER + 2 * RET_V) // AB_TN
AB_GATE_BLOCKS = AB_XBC0
AB_XQK_BLOCKS = SSD_CONV_CH // AB_TN + 2 * RET_QK // AB_TN
AB_ROW_TILE = 512


def _ab_gates_body(x_ref, g_ref, w_ref, *rest):
    eda_ref, xdt_ref, bm_ref, cm_ref, h_ref, o_ref, ho_ref, y_ref = rest
    eda_t = _pad_t(eda_ref[...])
    xdt_t = _pad_t(xdt_ref[...])
    cm = _bf(cm_ref[...])
    per = -(-SSD_SB // AB_GATE_BLOCKS)
    xn = _bf(_rms(x_ref[...]) * g_ref[...])
    nz = SSD_D_INNER // AB_TN
    nv = RET_V // AB_TN
    for n in range(AB_GATE_BLOCKS):
        cs = slice(n * AB_TN, (n + 1) * AB_TN)
        r = _dot(xn, w_ref[:, cs])
        o_ref[:, cs] = _bf(r) if nz <= n < nz + nv else _bf(_silu(r))
        for b in range(n * per, min(SSD_SB, (n + 1) * per)):
            _ssd_state_one(b, eda_t, xdt_t, cm, bm_ref, h_ref, ho_ref, y_ref)


def _ab_xqk_body(x_ref, g_ref, w_ref, ws_ref, dtb_ref, cw_ref, cb_ref, cos_ref, sin_ref,
                 o_ref, dt_ref, xpad, rbuf, ybuf, carry, *, tiles_per_seq):
    i = pl.program_id(0)
    tm = x_ref.shape[0]
    nslab = tm // SUBLANES
    pitch = nslab + SUBLANES
    nxb = SSD_CONV_CH // AB_TN
    ntap = SSD_CONV - 1

    @pl.when(i == 0)
    def _():
        carry[...] = jnp.zeros_like(carry)

    g = g_ref[...]
    xn = _bf(_rms(x_ref[...]) * g)
    dt_ref[...] = _softplus(_dot(xn, ws_ref[...]) + dtb_ref[...])
    for lb in range(D_MODEL // LANES):
        for s in range(SUBLANES):
            xpad[lb, s * pitch:s * pitch + nslab, :] = x_ref[s * nslab:(s + 1) * nslab, lb * LANES:(lb + 1) * LANES]
    xp = jnp.concatenate(
        [jnp.concatenate([xpad[lb, pl.ds(v, SUBLANES, stride=pitch), :] for lb in range(D_MODEL // LANES)], axis=1)
         for v in range(nslab)], axis=0)
    xnp = _bf(_rms(xp) * g)
    seq_start = i % tiles_per_seq == 0
    first = lax.broadcasted_iota(jnp.int32, (SUBLANES, AB_TN), 0) == 0
    halo = ntap * SUBLANES
    for n in range(nxb):
        cs = slice(n * AB_TN, (n + 1) * AB_TN)
        rbuf[n, halo:halo + tm, :] = _dot(xnp, w_ref[:, cs])
        prev_rows = jnp.where(seq_start, 0.0, carry[n])
        for k in range(ntap):
            hi = rbuf[n, tm + k * SUBLANES:tm + (k + 1) * SUBLANES, :]
            rbuf[n, k * SUBLANES:(k + 1) * SUBLANES, :] = jnp.where(first, prev_rows[k:k + 1, :],
                                                                     pltpu.roll(hi, 1, axis=0))
            carry[n, k:k + 1, :] = hi[SUBLANES - 1:SUBLANES, :]
        w = cw_ref[:, cs]
        conv = cb_ref[:, cs] + rbuf[n, halo:halo + tm, :] * w[ntap:ntap + 1, :]
        for j in range(1, SSD_CONV):
            off = halo - j * SUBLANES
            conv = conv + rbuf[n, off:off + tm, :] * w[ntap - j:ntap - j + 1, :]
        y = _silu(conv)
        for lb in range(AB_TN // LANES):
            ls = slice(lb * LANES, (lb + 1) * LANES)
            for v in range(nslab):
                ybuf[n, lb, pl.ds(v, SUBLANES, stride=pitch), :] = y[v * SUBLANES:(v + 1) * SUBLANES, ls]
            for s in range(SUBLANES):
                o_ref[s * nslab:(s + 1) * nslab, n * AB_TN + ls.start:n * AB_TN + ls.stop] = _bf(
                    ybuf[n, lb, s * pitch:s * pitch + nslab, :])
    cos = cos_ref[...]
    sin = sin_ref[...]
    for n in range(nxb, AB_XQK_BLOCKS):
        r = _dot(xn, w_ref[:, n * AB_TN:(n + 1) * AB_TN])
        scale = 1.0 if n == nxb else RET_DK ** -0.5
        for h in range(AB_TN // RET_DK):
            a = slice(h * RET_DK, h * RET_DK + LANES)
            b = slice(h * RET_DK + LANES, (h + 1) * RET_DK)
            x1, x2 = r[:, a], r[:, b]
            o_ref[:, n * AB_TN + a.start:n * AB_TN + a.stop] = _bf((x1 * cos - x2 * sin) * scale)
            o_ref[:, n * AB_TN + b.start:n * AB_TN + b.stop] = _bf((x1 * sin + x2 * cos) * scale)


def _ab_inproj(x, g3, w_gates, w_xqk, w_dt, dt_bias, conv_w, conv_b, cos, sin, ssd_step):
    m = x.shape[0]
    tm = AB_ROW_TILE
    tps = SEQ // tm
    nxb = SSD_CONV_CH // AB_TN
    once = pl.Buffered(1)

    def full(a):
        return pl.BlockSpec(a.shape, lambda i: (0,) * a.ndim, pipeline_mode=once)

    xspec = pl.BlockSpec((tm, D_MODEL), lambda i: (i, 0))
    gspec = pl.BlockSpec((None, 1, D_MODEL), lambda i: (0, 0, 0))
    eda, xdt, bm_s, cm_s, state = ssd_step
    ng = SSD_GROUPS
    assert m // tm == (DEC_BATCH // SSD_SB) * ng
    vec = pl.BlockSpec((SSD_SB, GROUP_W), lambda i: (i // ng, i % ng))
    bc = pl.BlockSpec((SSD_SB, SSD_D_STATE), lambda i: (i // ng, i % ng))
    st = pl.BlockSpec((SSD_SB, GROUP_W, SSD_D_STATE), lambda i: (i // ng, i % ng, 0))
    gates, ssm_s, y_s = pl.pallas_call(
        _ab_gates_body,
        grid=(m // tm,),
        in_specs=[xspec, gspec, full(w_gates), vec, vec, bc, bc, st],
        out_specs=[pl.BlockSpec((tm, AB_GATE_BLOCKS * AB_TN), lambda i: (i, 0)), st, vec],
        out_shape=[jax.ShapeDtypeStruct((m, AB_GATE_BLOCKS * AB_TN), BF16),
                   jax.ShapeDtypeStruct(state.shape, F32),
                   jax.ShapeDtypeStruct((DEC_BATCH, SSD_D_INNER), F32)],
        compiler_params=_cp(("parallel",), 56),
        name="ab_gates",
    )(x, g3, w_gates, eda, xdt, bm_s, cm_s, state)
    xqk, dt = pl.pallas_call(
        functools.partial(_ab_xqk_body, tiles_per_seq=tps),
        grid=(m // tm,),
        in_specs=[xspec, gspec, full(w_xqk), full(w_dt), full(dt_bias), full(conv_w), full(conv_b),
                  pl.BlockSpec((tm, LANES), lambda i: (i % tps, 0)),
                  pl.BlockSpec((tm, LANES), lambda i: (i % tps, 0))],
        out_specs=[pl.BlockSpec((tm, AB_XQK_BLOCKS * AB_TN), lambda i: (i, 0)),
                   pl.BlockSpec((tm, LANES), lambda i: (i, 0))],
        out_shape=[jax.ShapeDtypeStruct((m, AB_XQK_BLOCKS * AB_TN), BF16), jax.ShapeDtypeStruct((m, LANES), F32)],
        scratch_shapes=[pltpu.VMEM((D_MODEL // LANES, tm + SUBLANES * SUBLANES, LANES), F32),
                        pltpu.VMEM((nxb, (SSD_CONV - 1) * SUBLANES + tm, AB_TN), F32),
                        pltpu.VMEM((nxb, AB_TN // LANES, tm + SUBLANES * SUBLANES, LANES), F32),
                        pltpu.VMEM((nxb, SUBLANES, AB_TN), F32)],
        compiler_params=_cp(("arbitrary",), 56),
        name="ab_xqk",
    )(x, g3, w_xqk, w_dt, dt_bias, conv_w, conv_b, cos, sin)
    return gates, xqk, dt, ssm_s, y_s


def _c_inproj_body(x_ref, g_ref, w_ref, wg_ref, o_ref, gt_ref):
    xn = _bf(_rms(x_ref[...]) * g_ref[...])
    gt_ref[...] = _dot(xn, wg_ref[...])
    o_ref[:, :M_QK] = _bf(_dot(xn, w_ref[:, :M_QK]))
    o_ref[:, M_QK:2 * M_QK] = _bf(_dot(xn, w_ref[:, M_QK:2 * M_QK]) * (M_DK ** -0.5))
    o_ref[:, 2 * M_QK:2 * M_QK + M_V] = _bf(_dot(xn, w_ref[:, 2 * M_QK:2 * M_QK + M_V]))
    o_ref[:, 2 * M_QK + M_V:] = _bf(jax.nn.sigmoid(_dot(xn, w_ref[:, 2 * M_QK + M_V:])))


def _c_inproj(x, g3, glayer, w_main, w_gate):
    m = x.shape[0]
    tm = ROW_TILE
    once = pl.Buffered(1)
    n_out = w_main.shape[-1]
    n_gate = w_gate.shape[-1]
    return pl.pallas_call(
        _c_inproj_body,
        grid=(m // tm,),
        in_specs=[
            pl.BlockSpec((tm, D_MODEL), lambda i: (i, 0)),
            pl.BlockSpec((None, 1, D_MODEL), lambda i: (glayer, 0, 0)),
            pl.BlockSpec((None, D_MODEL, n_out), lambda i: (0, 0, 0), pipeline_mode=once),
            pl.BlockSpec((None, D_MODEL, n_gate), lambda i: (0, 0, 0), pipeline_mode=once),
        ],
        out_specs=[pl.BlockSpec((tm, n_out), lambda i: (i, 0)), pl.BlockSpec((tm, n_gate), lambda i: (i, 0))],
        out_shape=[jax.ShapeDtypeStruct((m, n_out), BF16), jax.ShapeDtypeStruct((m, n_gate), F32)],
        compiler_params=_cp(("parallel",), 48),
        name="c_inproj",
    )(x, g3, w_main, w_gate)


def _proj_res_body(x_ref, y_ref, w_ref, o_ref):
    o_ref[...] = x_ref[...] + _dot(_bf(y_ref[...]), w_ref[...])


def _proj_residual(x, y, w3, layer):
    m = x.shape[0]
    tm = min(m, ROW_TILE)
    kdim = y.shape[1]
    return pl.pallas_call(
        _proj_res_body,
        grid=(m // tm,),
        in_specs=[
            pl.BlockSpec((tm, D_MODEL), lambda i: (i, 0)),
            pl.BlockSpec((tm, kdim), lambda i: (i, 0)),
            pl.BlockSpec((None, kdim, D_MODEL), lambda i: (layer, 0, 0), pipeline_mode=pl.Buffered(1)),
        ],
        out_specs=pl.BlockSpec((tm, D_MODEL), lambda i: (i, 0)),
        out_shape=jax.ShapeDtypeStruct((m, D_MODEL), F32),
        compiler_params=_cp(("parallel",), 56),
        name="proj_residual",
    )(x, y, w3)


def _mem_kv_body(x_ref, g_ref, w_ref, k_ref, v_ref, k4_ref, v4_ref):
    xn = _bf(_rms(x_ref[...]) * g_ref[...])
    kv = _dot(xn, _bf(w_ref[...]))
    k_ref[...] = kv[:, :D_MODEL]
    v_ref[...] = kv[:, D_MODEL:]
    for h in range(X_HEADS):
        k4_ref[:, h, :] = kv[:, h * X_HEAD_DIM:(h + 1) * X_HEAD_DIM]
        v4_ref[:, h, :] = kv[:, D_MODEL + h * X_HEAD_DIM:D_MODEL + (h + 1) * X_HEAD_DIM]


def _mem_kv(mem2d, g3, w_xkv):
    m = mem2d.shape[0]
    tm = 512
    shp = jax.ShapeDtypeStruct((DEPTH, m, D_MODEL), F32)
    shp4 = jax.ShapeDtypeStruct((DEPTH, m, X_HEADS, X_HEAD_DIM), F32)
    return pl.pallas_call(
        _mem_kv_body,
        grid=(DEPTH, m // tm),
        in_specs=[
            pl.BlockSpec((tm, D_MODEL), lambda l, i: (i, 0)),
            pl.BlockSpec((None, 1, D_MODEL), lambda l, i: (l, 0, 0)),
            pl.BlockSpec((None, D_MODEL, 2 * D_MODEL), lambda l, i: (l, 0, 0)),
        ],
        out_specs=[pl.BlockSpec((None, tm, D_MODEL), lambda l, i: (l, i, 0))] * 2
        + [pl.BlockSpec((None, tm, X_HEADS, X_HEAD_DIM), lambda l, i: (l, i, 0, 0))] * 2,
        out_shape=[shp, shp, shp4, shp4],
        compiler_params=_cp(("arbitrary", "arbitrary"), 48),
        name="mem_kv",
    )(mem2d, g3, w_xkv)


def _xattn_prompt_body(x_ref, g_ref, wq_ref, wo_ref, k_ref, v_ref, o_ref):
    x = x_ref[...]
    xn = _bf(_rms(x) * g_ref[...])
    q = _dot(xn, _bf(wq_ref[...]))
    k = _bf(k_ref[...])
    v = _bf(v_ref[...])
    outs = []
    for h in range(X_HEADS):
        sl = slice(h * X_HEAD_DIM, (h + 1) * X_HEAD_DIM)
        s = _dot_nt(_bf(q[:, sl]), k[:, sl]) * (X_HEAD_DIM ** -0.5)
        e = jnp.exp(s - jnp.max(s, axis=-1, keepdims=True))
        p = e / jnp.sum(e, axis=-1, keepdims=True)
        outs.append(_bf(_dot(_bf(p), v[:, sl])))
    o_ref[...] = x + _dot(jnp.concatenate(outs, axis=1), _bf(wo_ref[...]))


def _xattn_prompt(x, g3, w_xq, w_xo, memk, memv, layer):
    tq = ROW_TILE
    nq = SEQ // tq
    return pl.pallas_call(
        _xattn_prompt_body,
        grid=(BATCH, nq),
        in_specs=[
            pl.BlockSpec((tq, D_MODEL), lambda b, j: (b * nq + j, 0)),
            pl.BlockSpec((None, 1, D_MODEL), lambda b, j: (layer, 0, 0)),
            pl.BlockSpec((None, D_MODEL, D_MODEL), lambda b, j: (layer, 0, 0)),
            pl.BlockSpec((None, D_MODEL, D_MODEL), lambda b, j: (layer, 0, 0)),
            pl.BlockSpec((None, MEM_LEN, D_MODEL), lambda b, j: (layer, b, 0)),
            pl.BlockSpec((None, MEM_LEN, D_MODEL), lambda b, j: (layer, b, 0)),
        ],
        out_specs=pl.BlockSpec((tq, D_MODEL), lambda b, j: (b * nq + j, 0)),
        out_shape=jax.ShapeDtypeStruct((BATCH * SEQ, D_MODEL), F32),
        compiler_params=_cp(("parallel", "arbitrary"), 48),
        name="xattn_prompt",
    )(x, g3, w_xq, w_xo, memk, memv)


def _xattn_sample_one(b, q_ref, k_ref, v_ref, o_ref):
    s = jnp.sum(k_ref[b] * (q_ref[b] * (X_HEAD_DIM ** -0.5))[None], axis=-1, keepdims=True)
    e = jnp.exp(s - jnp.max(s, axis=0, keepdims=True))
    o_ref[b] = jnp.sum(e * v_ref[b], axis=0) / jnp.sum(e, axis=0)


def _ret_log_gamma(h):
    return math.log1p(-(2.0 ** (-5.0 - h)))


def _ab_prompt_body(z_ref, rv_ref, rg_ref, xbc_ref, rq_ref, rk_ref, dt_ref, alog_ref, dsk_ref, nrm_ref,
                    y_ref, h_ref, s_ref):
    c = pl.program_id(1)

    @pl.when(c == 0)
    def _():
        h_ref[...] = jnp.zeros_like(h_ref)
        s_ref[...] = jnp.zeros_like(s_ref)

    xs = xbc_ref[:, :SSD_D_INNER].astype(F32)
    bm = xbc_ref[:, SSD_D_INNER:SSD_D_INNER + GROUP_W]
    cm = xbc_ref[:, SSD_D_INNER + GROUP_W:]

    dt = dt_ref[...]
    da = dt * (-jnp.exp(alog_ref[...]))
    cs = _cumsum_rows(da)
    cs_t = cs.T
    row = lax.broadcasted_iota(jnp.int32, (CHUNK, CHUNK), 0)
    col = lax.broadcasted_iota(jnp.int32, (CHUNK, CHUNK), 1)
    tri = row >= col
    lo = col < SSD_HEAD_DIM
    for g in range(SSD_GROUPS):
        gs = slice(g * GROUP_W, (g + 1) * GROUP_W)
        ns = slice(g * SSD_D_STATE, (g + 1) * SSD_D_STATE)
        cmg = cm[:, ns]
        bmg = bm[:, ns]
        att = _dot_nt(cmg, bmg)
        hprev = h_ref[gs, :]
        yint = _dot_nt(cmg, _bf(hprev))
        ys, wxs, css = [], [], []
        for j in range(HEADS_PER_GROUP // 2):
            h0 = g * HEADS_PER_GROUP + 2 * j
            cb0 = _lane_bcast(cs, h0)
            cb1 = _lane_bcast(cs, h0 + 1)
            cs_p = jnp.where(lo, cb0, cb1)
            dt_p = jnp.where(lo, _lane_bcast(dt, h0), _lane_bcast(dt, h0 + 1))
            off = g * GROUP_W + j * LANES
            xdt = xs[:, off:off + LANES] * dt_p
            d0 = jnp.exp(jnp.where(tri, cb0 - cs_t[h0:h0 + 1, :], -jnp.inf))
            d1 = jnp.exp(jnp.where(tri, cb1 - cs_t[h0 + 1:h0 + 2, :], -jnp.inf))
            yy = _dot(jnp.concatenate([_bf(att * d0), _bf(att * d1)], axis=0), _bf(xdt))
            ys.append(jnp.where(lo, yy[:CHUNK], yy[CHUNK:]) + yint[:, j * LANES:(j + 1) * LANES] * jnp.exp(cs_p))
            wxs.append(_bf(xdt * jnp.exp(cs_p[CHUNK - 1:CHUNK, :] - cs_p)))
            css.append(cs_p)
        cs_g = jnp.concatenate(css, axis=1)
        last_t = jnp.broadcast_to(cs_g[CHUNK - 1:CHUNK, :], (CHUNK, GROUP_W)).T
        h_ref[gs, :] = hprev * jnp.exp(last_t) + _dot_tn(jnp.concatenate(wxs, axis=1), bmg)
        yg = jnp.concatenate(ys, axis=1)
        yg = (yg + xs[:, gs] * dsk_ref[:, gs]) * z_ref[:, gs].astype(F32)
        y_ref[:, gs] = _bf(_rms(yg) * nrm_ref[:, gs])

    tcol = row.astype(F32)
    diff = tcol - col.astype(F32)
    for h in range(RET_HEADS):
        lg = _ret_log_gamma(h)
        ks = slice(h * RET_DK, (h + 1) * RET_DK)
        vs = slice(h * RET_DV, (h + 1) * RET_DV)
        qb = rq_ref[:, ks]
        kb = rk_ref[:, ks]
        decay = jnp.exp(jnp.where(tri, diff * lg, -jnp.inf))
        att = _dot_nt(qb, kb) * decay
        vb = rv_ref[:, vs]
        s_prev = s_ref[h]
        inner = jnp.exp((tcol + 1.0) * lg)
        r = _dot(_bf(att), vb) + _dot(qb, _bf(s_prev)) * jnp.concatenate([inner] * (RET_DV // LANES), axis=1)
        tail_w = jnp.exp((CHUNK - 1.0 - tcol) * lg)
        kt = _bf(kb.astype(F32) * jnp.concatenate([tail_w] * (RET_DK // LANES), axis=1))
        s_ref[h] = s_prev * math.exp(CHUNK * lg) + _dot_tn(kt, vb)
        os = slice(SSD_D_INNER + h * RET_DV, SSD_D_INNER + (h + 1) * RET_DV)
        y_ref[:, os] = _bf(rg_ref[:, vs].astype(F32) * _rms(r))


def _mixer_ab_prompt(p_gates, p_xqk, p_dt, a_log, d_skip_e, ssd_norm):
    nc = SEQ // CHUNK
    m = BATCH * SEQ

    def rowspec(width, cb):
        return pl.BlockSpec((CHUNK, width), lambda b, c: (b * nc + c, cb))

    def full(a):
        return pl.BlockSpec(a.shape, lambda b, c: (0,) * a.ndim)

    params = [a_log, d_skip_e, ssd_norm]
    return pl.pallas_call(
        _ab_prompt_body,
        grid=(BATCH, nc),
        in_specs=[rowspec(SSD_D_INNER, 0), rowspec(RET_V, 1), rowspec(RET_V, 2), rowspec(SSD_CONV_CH, 0),
                  rowspec(RET_QK, SSD_CONV_CH // RET_QK), rowspec(RET_QK, SSD_CONV_CH // RET_QK + 1),
                  rowspec(LANES, 0)] + [full(a) for a in params],
        out_specs=[
            pl.BlockSpec((CHUNK, SSD_D_INNER + RET_V), lambda b, c: (b * nc + c, 0)),
            pl.BlockSpec((None, SSD_D_INNER, SSD_D_STATE), lambda b, c: (b, 0, 0)),
            pl.BlockSpec((None, RET_HEADS, RET_DK, RET_DV), lambda b, c: (b, 0, 0, 0)),
        ],
        out_shape=[
            jax.ShapeDtypeStruct((m, SSD_D_INNER + RET_V), BF16),
            jax.ShapeDtypeStruct((BATCH, SSD_D_INNER, SSD_D_STATE), F32),
            jax.ShapeDtypeStruct((BATCH, RET_HEADS, RET_DK, RET_DV), F32),
        ],
        compiler_params=_cp(("parallel", "arbitrary"), 48),
        name="mixer_ab_prompt",
    )(p_gates, p_gates, p_gates, p_xqk, p_xqk, p_xqk, p_dt, *params)


def _c_prompt_body(q_ref, k_ref, v_ref, o_ref, gt_ref, ib_ref, fb_ref, nrm_ref,
                   h_ref, c_ref, n_ref, m_ref, m_s):
    @pl.when(pl.program_id(1) == 0)
    def _():
        c_ref[...] = jnp.zeros_like(c_ref)
        n_ref[...] = jnp.zeros_like(n_ref)
        m_s[...] = jnp.zeros_like(m_s)

    row = lax.broadcasted_iota(jnp.int32, (CHUNK, CHUNK), 0)
    col = lax.broadcasted_iota(jnp.int32, (CHUNK, CHUNK), 1)
    tri = row >= col
    seqs = range(CB)
    ipre = [gt_ref[s, :, :LANES] + ib_ref[...] for s in seqs]
    lf = [-_softplus(-(gt_ref[s, :, LANES:] + fb_ref[...])) for s in seqs]
    b = [_cumsum_rows(lf[s]) for s in seqs]
    g = [ipre[s] - b[s] for s in seqs]
    g_t = [g[s].T for s in seqs]
    b_t = [b[s].T for s in seqs]
    cmax = list(g_t)
    sh = 1
    while sh < CHUNK:
        cmax = [jnp.maximum(cmax[s], jnp.where(col >= sh, pltpu.roll(cmax[s], sh, axis=1), -jnp.inf)) for s in seqs]
        sh *= 2
    m_prev = [m_s[s] for s in seqs]
    mt_t = [b_t[s] + jnp.maximum(m_prev[s], cmax[s]) for s in seqs]
    mt = [mt_t[s].T for s in seqs]
    m_prev_c = [m_prev[s].T for s in seqs]
    inter = [jnp.exp(b[s] + m_prev_c[s] - mt[s]) for s in seqs]
    emt = [jnp.exp(-mt[s]) for s in seqs]
    wl = [jnp.exp(g[s] + b[s][CHUNK - 1:CHUNK, :] - mt[s][CHUNK - 1:CHUNK, :]) for s in seqs]
    bm = [b[s] - mt[s] for s in seqs]
    m_new = [_lane_bcast(mt_t[s], CHUNK - 1) for s in seqs]
    dp_t = [jnp.exp(_lane_bcast(b_t[s], CHUNK - 1) + m_prev[s] - m_new[s]) for s in seqs]
    for s in seqs:
        m_s[s] = m_new[s]
        m_ref[s] = m_new[s][0:SUBLANES, :]
    for h in range(M_HEADS):
        ks = slice(h * M_DK, (h + 1) * M_DK)
        vs = slice(h * M_DV, (h + 1) * M_DV)
        for s in seqs:
            wgt = jnp.exp(jnp.where(tri, g_t[s][h:h + 1, :] + bm[s][:, h:h + 1], -jnp.inf))
            qb = q_ref[s, :, ks]
            kb = k_ref[s, :, ks]
            vb = v_ref[s, :, vs]
            a = _dot_nt(qb, kb) * wgt
            c_prev = c_ref[s, h]
            n_prev = n_ref[s, h:h + 1, :]
            ic = inter[s][:, h:h + 1]
            num = _dot(_bf(a), vb) + _dot(qb, _bf(c_prev)) * ic
            den = (jnp.sum(a, axis=1, keepdims=True)
                   + jnp.sum(qb.astype(F32) * n_prev, axis=1, keepdims=True) * ic)
            hc = num / jnp.maximum(jnp.abs(den), emt[s][:, h:h + 1])
            kw = kb.astype(F32) * wl[s][:, h:h + 1]
            dp_row = dp_t[s][h:h + 1, :]
            c_ref[s, h] = c_prev * jnp.concatenate([dp_row] * (M_DV // LANES), axis=1) + _dot_tn(_bf(kw), vb)
            n_ref[s, h:h + 1, :] = n_prev * dp_row + jnp.sum(kw, axis=0, keepdims=True)
            h_ref[s, :, vs] = _bf(o_ref[s, :, vs].astype(F32) * (_rms(hc) * nrm_ref[:, vs]))


def _mixer_c_prompt(p_main, p_gate, i_bias, f_bias, norm_g):
    nc = SEQ // CHUNK

    def rowspec(width, cb):
        return pl.BlockSpec((CB, CHUNK, width), lambda b, c: (b, c, cb))

    def full(a):
        return pl.BlockSpec(a.shape, lambda b, c: (0,) * a.ndim)

    params = [i_bias, f_bias, norm_g]
    return pl.pallas_call(
        _c_prompt_body,
        grid=(BATCH // CB, nc),
        in_specs=[rowspec(M_QK, 0), rowspec(M_QK, 1), rowspec(M_V, 1), rowspec(M_V, 2), rowspec(2 * LANES, 0)]
        + [full(a) for a in params],
        out_specs=[
            pl.BlockSpec((CB, CHUNK, M_V), lambda b, c: (b, c, 0)),
            pl.BlockSpec((CB, M_HEADS, M_DK, M_DV), lambda b, c: (b, 0, 0, 0)),
            pl.BlockSpec((CB, M_HEADS, M_DK), lambda b, c: (b, 0, 0)),
            pl.BlockSpec((CB, SUBLANES, LANES), lambda b, c: (b, 0, 0)),
        ],
        out_shape=[
            jax.ShapeDtypeStruct((BATCH, SEQ, M_V), BF16),
            jax.ShapeDtypeStruct((BATCH, M_HEADS, M_DK, M_DV), F32),
            jax.ShapeDtypeStruct((BATCH, M_HEADS, M_DK), F32),
            jax.ShapeDtypeStruct((BATCH, SUBLANES, LANES), F32),
        ],
        scratch_shapes=[pltpu.VMEM((CB, CHUNK, LANES), F32)],
        compiler_params=_cp(("parallel", "arbitrary"), 48),
        name="mixer_c_prompt",
    )(p_main, p_main, p_main, p_main, p_gate, *params)


def _ab_sample_prep_body(xbc_ref, rq_ref, rk_ref, dt_ref, cst_ref, cos_ref, sin_ref,
                         cw_ref, cb_ref, dtb_ref, alog_ref,
                         conv_ref, xs_ref, xdt_ref, eda_ref, bm_ref, cm_ref, q_ref, k_ref, gam_ref):
    ch = SSD_CONV_CH
    u = xbc_ref[...]
    w = cw_ref[...]
    b0 = cst_ref[:, 0:ch]
    b1 = cst_ref[:, ch:2 * ch]
    b2 = cst_ref[:, 2 * ch:3 * ch]
    conv = cb_ref[...] + (((b0 * w[0:1, :] + b1 * w[1:2, :]) + b2 * w[2:3, :]) + u * w[3:4, :])
    conv_ref[:, 0:ch] = b1
    conv_ref[:, ch:2 * ch] = b2
    conv_ref[:, 2 * ch:3 * ch] = u
    xbc = _silu(conv)
    xs = xbc[:, :SSD_D_INNER]
    xs_ref[...] = xs
    bm_ref[...] = xbc[:, SSD_D_INNER:SSD_D_INNER + GROUP_W]
    cm_ref[...] = xbc[:, SSD_D_INNER + GROUP_W:]
    dt = _softplus(dt_ref[...] + dtb_ref[...])
    eda = jnp.exp(dt * (-jnp.exp(alog_ref[...])))
    xdt_ref[...] = xs * _pair_expand(dt, SSD_HEADS)
    eda_ref[...] = _pair_expand(eda, SSD_HEADS)
    cos = cos_ref[...]
    sin = sin_ref[...]
    for h in range(RET_HEADS):
        a = slice(h * RET_DK, h * RET_DK + LANES)
        b = slice(h * RET_DK + LANES, (h + 1) * RET_DK)
        q1, q2 = rq_ref[:, a], rq_ref[:, b]
        k1, k2 = rk_ref[:, a], rk_ref[:, b]
        q_ref[:, a] = q1 * cos - q2 * sin
        q_ref[:, b] = q1 * sin + q2 * cos
        k_ref[:, a] = (k1 * cos - k2 * sin) * (RET_DK ** -0.5)
        k_ref[:, b] = (k1 * sin + k2 * cos) * (RET_DK ** -0.5)
        gam_ref[:, h * RET_DK:(h + 1) * RET_DK] = jnp.full((DEC_BATCH, RET_DK), math.exp(_ret_log_gamma(h)), F32)


def _ab_sample_prep(p_main, p_dt, conv_state, cos, sin, conv_w, conv_b, dt_bias, a_log):
    n = DEC_BATCH

    def colspec(width, cb):
        return pl.BlockSpec((n, width), lambda i: (0, cb))

    def full(a):
        return pl.BlockSpec(a.shape, lambda i: (0,) * a.ndim)

    small = [conv_state, cos, sin, conv_w, conv_b, dt_bias, a_log]

    def out(width):
        return jax.ShapeDtypeStruct((n, width), F32)

    widths = [(SSD_CONV - 1) * SSD_CONV_CH, SSD_D_INNER, SSD_D_INNER, SSD_D_INNER, GROUP_W, GROUP_W,
              RET_QK, RET_QK, RET_QK]
    return pl.pallas_call(
        _ab_sample_prep_body,
        grid=(1,),
        in_specs=[colspec(SSD_CONV_CH, 2), colspec(RET_QK, 9), colspec(RET_QK, 10), full(p_dt)]
        + [full(a) for a in small],
        out_specs=[pl.BlockSpec((n, wd), lambda i: (0, 0)) for wd in widths],
        out_shape=[out(wd) for wd in widths],
        compiler_params=_cp(("arbitrary",), 48),
        name="ab_sample_prep",
    )(p_main, p_main, p_main, p_dt, *small)


def _ssd_state_one(b, eda_t, xdt_t, cm, bm_ref, h_ref, ho_ref, y_ref):
    hn = h_ref[b] * eda_t[:, b:b + 1] + xdt_t[:, b:b + 1] * bm_ref[b:b + 1, :]
    ho_ref[b] = hn
    y_ref[b:b + 1, :] = _dot_nt(cm, _bf(hn))[b:b + 1, :]


def _outer_state_body(d_ref, k_ref, q_ref, v_ref, s_ref, so_ref, o_ref):
    d_t = _pad_t(d_ref[...])
    k_t = _pad_t(k_ref[...])
    q_t = _pad_t(q_ref[...])
    for b in range(d_ref.shape[0]):
        sn = s_ref[b] * d_t[:, b:b + 1] + k_t[:, b:b + 1] * v_ref[b:b + 1, :]
        so_ref[b] = sn
        o_ref[b:b + 1, :] = jnp.sum(sn * q_t[:, b:b + 1], axis=0, keepdims=True)


def _outer_state(d, k, q, v, state):
    _, nh, dk, dv = state.shape
    sb = STATE_BLOCK_BYTES // (dk * dv * 4)
    kv = pl.BlockSpec((sb, dk), lambda i, h: (i, h))
    vv = pl.BlockSpec((sb, dv), lambda i, h: (i, h))
    st = pl.BlockSpec((sb, None, dk, dv), lambda i, h: (i, h, 0, 0))
    return pl.pallas_call(
        _outer_state_body,
        grid=(DEC_BATCH // sb, nh),
        in_specs=[kv, kv, kv, vv, st],
        out_specs=[st, vv],
        out_shape=[jax.ShapeDtypeStruct(state.shape, F32), jax.ShapeDtypeStruct((DEC_BATCH, nh * dv), F32)],
        compiler_params=_cp(("parallel", "arbitrary"), 48),
        name="outer_state",
    )(d, k, q, v, state)


def _ab_sample_post_body(y_ref, xs_ref, z_ref, r_ref, rg_ref, dsk_ref, nrm_ref, o_ref):
    for g in range(SSD_GROUPS):
        gs = slice(g * GROUP_W, (g + 1) * GROUP_W)
        yg = (y_ref[:, gs] + xs_ref[:, gs] * dsk_ref[:, gs]) * _silu(z_ref[:, gs])
        o_ref[:, gs] = _bf(_rms(yg) * nrm_ref[:, gs])
    for h in range(RET_HEADS):
        vs = slice(h * RET_DV, (h + 1) * RET_DV)
        os = slice(SSD_D_INNER + h * RET_DV, SSD_D_INNER + (h + 1) * RET_DV)
        o_ref[:, os] = _bf(_silu(rg_ref[:, vs]) * _rms(r_ref[:, vs]))


def _ab_sample_post(y, xs, p_main, r, d_skip_e, ssd_norm):
    n = DEC_BATCH

    def full(a):
        return pl.BlockSpec(a.shape, lambda i: (0,) * a.ndim)

    return pl.pallas_call(
        _ab_sample_post_body,
        grid=(1,),
        in_specs=[full(y), full(xs), pl.BlockSpec((n, SSD_D_INNER), lambda i: (0, 0)), full(r),
                  pl.BlockSpec((n, RET_V), lambda i: (0, 2)), full(d_skip_e), full(ssd_norm)],
        out_specs=pl.BlockSpec((n, SSD_D_INNER + RET_V), lambda i: (0, 0)),
        out_shape=jax.ShapeDtypeStruct((n, SSD_D_INNER + RET_V), BF16),
        compiler_params=_cp(("arbitrary",), 48),
        name="ab_sample_post",
    )(y, xs, p_main, r, p_main, d_skip_e, ssd_norm)


def _c_sample_prep_body(q_ref, k_ref, gt_ref, n_ref, m_ref, ib_ref, fb_ref,
                        dpe_ref, kw_ref, nn_ref, mn_ref, dn_ref):
    ipre = gt_ref[:, :LANES] + ib_ref[...]
    lf = -_softplus(-(gt_ref[:, LANES:] + fb_ref[...]))
    m_prev = m_ref[...]
    mt = jnp.maximum(lf + m_prev, ipre)
    wgt = jnp.exp(ipre - mt)
    dp = jnp.exp(lf + m_prev - mt)
    emt = jnp.exp(-mt)
    mn_ref[...] = mt
    for h in range(M_HEADS):
        ks = slice(h * M_DK, (h + 1) * M_DK)
        dpe = _lane_bcast(dp, h)
        kw = k_ref[:, ks] * (M_DK ** -0.5) * _lane_bcast(wgt, h)
        nn = n_ref[:, ks] * dpe + kw
        den = jnp.sum(nn * q_ref[:, ks], axis=1, keepdims=True)
        dpe_ref[:, ks] = dpe
        kw_ref[:, ks] = kw
        nn_ref[:, ks] = nn
        dn_ref[:, h * M_DV:(h + 1) * M_DV] = jnp.broadcast_to(
            jnp.maximum(jnp.abs(den), emt[:, h:h + 1]), (DEC_BATCH, M_DV))


def _c_sample_prep(p_main, p_gate, n_state, m_state, i_bias, f_bias):
    n = DEC_BATCH

    def full(a):
        return pl.BlockSpec(a.shape, lambda i: (0,) * a.ndim)

    widths = [M_QK, M_QK, M_QK, LANES, M_V]
    return pl.pallas_call(
        _c_sample_prep_body,
        grid=(1,),
        in_specs=[pl.BlockSpec((n, M_QK), lambda i: (0, 0)), pl.BlockSpec((n, M_QK), lambda i: (0, 1)),
                  full(p_gate), full(n_state), full(m_state), full(i_bias), full(f_bias)],
        out_specs=[pl.BlockSpec((n, wd), lambda i: (0, 0)) for wd in widths],
        out_shape=[jax.ShapeDtypeStruct((n, wd), F32) for wd in widths],
        compiler_params=_cp(("arbitrary",), 48),
        name="c_sample_prep",
    )(p_main, p_main, p_gate, n_state, m_state, i_bias, f_bias)


def _c_sample_post_body(num_ref, dn_ref, o_ref, nrm_ref, h_ref):
    for h in range(M_HEADS):
        vs = slice(h * M_DV, (h + 1) * M_DV)
        hc = num_ref[:, vs] / dn_ref[:, vs]
        h_ref[:, vs] = _bf(jax.nn.sigmoid(o_ref[:, vs]) * (_rms(hc) * nrm_ref[:, vs]))


def _c_sample_post(num, den, p_main, norm_g):
    n = DEC_BATCH

    def full(a):
        return pl.BlockSpec(a.shape, lambda i: (0,) * a.ndim)

    return pl.pallas_call(
        _c_sample_post_body,
        grid=(1,),
        in_specs=[full(num), full(den), pl.BlockSpec((n, M_V), lambda i: (0, 2)), full(norm_g)],
        out_specs=pl.BlockSpec((n, M_V), lambda i: (0, 0)),
        out_shape=jax.ShapeDtypeStruct((n, M_V), BF16),
        compiler_params=_cp(("arbitrary",), 48),
        name="c_sample_post",
    )(num, den, p_main, norm_g)


def _rope_tables(pos):
    half = RET_DK // 2
    inv = jnp.exp(-math.log(ROPE_BASE) * jnp.arange(half, dtype=F32) / half)
    ang = pos.astype(F32)[:, None] * inv
    return jnp.cos(ang), jnp.sin(ang)


def _pad_lanes(v, width=LANES):
    return jnp.pad(v.reshape(1, -1), ((0, 0), (0, width - v.size)))


def kernel(x_prompt, x_sample, cache_mem_k, cache_mem_v, state_conv, state_ssm, state_ret, state_mlstm_c, state_mlstm_n, state_mlstm_m, mem_prompt, norm_ffn1, w_ffn1_in, w_ffn1_out, norm_mix, w_in_ab, ssd_conv_w, ssd_conv_b, ssd_dt_bias, ssd_a_log, ssd_d, ssd_norm, w_out_ab, w_in_c, mlstm_i_bias, mlstm_f_bias, mlstm_norm, w_out_c, norm_xattn, norm_mem, w_xq, w_xkv, w_xo, norm_ffn2, w_ffn2_in, w_ffn2_out, norm_final):
    g3 = lambda g: g.reshape(DEPTH, 1, D_MODEL)
    n_ffn1, n_mix, n_x, n_mem, n_ffn2 = g3(norm_ffn1), g3(norm_mix), g3(norm_xattn), g3(norm_mem), g3(norm_ffn2)

    f1_in, f1_out, f2_in, f2_out = (w.astype(BF16) for w in (w_ffn1_in, w_ffn1_out, w_ffn2_in, w_ffn2_out))
    w_out_ab, w_out_c, w_xq, w_xkv, w_xo = (w.astype(BF16) for w in (w_out_ab, w_out_c, w_xq, w_xkv, w_xo))

    wz, wxbc, wdt, wrq, wrk, wrv, wrg = jnp.split(w_in_ab[0], np_cumsum(AB_SIZES), axis=1)
    w_ab_main = jnp.concatenate([wz, wrv, wrg, wxbc, wrq, wrk], axis=1).astype(BF16)[None]
    w_ab_dt = jnp.pad(wdt, ((0, 0), (0, LANES - SSD_HEADS))).astype(BF16)[None]
    wq, wk, wv, wi, wf, wo = jnp.split(w_in_c[0], np_cumsum(C_SIZES), axis=1)
    w_c_main = jnp.concatenate([wq, wk, wv, wo], axis=1).astype(BF16)[None]
    gpad = ((0, 0), (0, LANES - M_HEADS))
    w_c_gate = jnp.concatenate([jnp.pad(wi, gpad), jnp.pad(wf, gpad)], axis=1).astype(BF16)[None]

    conv_w = ssd_conv_w[0]
    conv_b = ssd_conv_b.reshape(1, SSD_CONV_CH)
    dt_bias = _pad_lanes(ssd_dt_bias[0])
    a_log = _pad_lanes(ssd_a_log[0])
    d_skip_e = jnp.repeat(ssd_d[0], SSD_HEAD_DIM).reshape(1, SSD_D_INNER)
    s_norm = ssd_norm.reshape(1, SSD_D_INNER)
    i_bias = _pad_lanes(mlstm_i_bias[0])
    f_bias = _pad_lanes(mlstm_f_bias[0])
    m_norm = mlstm_norm.reshape(1, M_V)

    memk, memv, memk4, memv4 = _mem_kv(mem_prompt.reshape(BATCH * MEM_LEN, D_MODEL), n_mem, w_xkv)
    cos_p, sin_p = _rope_tables(jnp.arange(SEQ))
    x = x_prompt.reshape(BATCH * SEQ, D_MODEL)

    cos_s, sin_s = _rope_tables(PAST_LEN + jnp.arange(1))
    xs_ = x_sample.reshape(DEC_BATCH, D_MODEL)
    xs_ = _ffn(xs_, n_ffn1, f1_in, f1_out, 0)
    sp_main, sp_dt = _norm_proj(xs_, n_mix, 0, w_ab_main, 0, 1024, w_ab_dt)
    conv_s, xs_c, xdt, eda, bm_s, cm_s, q_s, k_s, gam = _ab_sample_prep(
        sp_main, sp_dt, state_conv.reshape(DEC_BATCH, (SSD_CONV - 1) * SSD_CONV_CH), cos_s, sin_s,
        conv_w, conv_b, dt_bias, a_log)
    ret_s, r_s = _outer_state(gam, k_s, q_s, sp_main[:, SSD_D_INNER:SSD_D_INNER + RET_V], state_ret[0])

    x = _ffn(x, n_ffn1, f1_in, f1_out, 0)
    w_gates = w_ab_main[0, :, :AB_GATE_BLOCKS * AB_TN]
    w_xqk = w_ab_main[0, :, AB_GATE_BLOCKS * AB_TN:]
    p_gates, p_xqk, p_dt, ssm_s, y_s = _ab_inproj(
        x, n_mix, w_gates, w_xqk, w_ab_dt[0], dt_bias, conv_w, conv_b, cos_p, sin_p,
        (eda, xdt, bm_s, cm_s, state_ssm.reshape(DEC_BATCH, SSD_D_INNER, SSD_D_STATE)))

    ycat_s = _ab_sample_post(y_s, xs_c, sp_main, r_s, d_skip_e, s_norm)
    xs_ = _proj_residual(xs_, ycat_s, w_out_ab, 0)
    q_x0 = _norm_proj(xs_, n_x, 0, w_xq, 0, 1024).reshape(DEC_BATCH, X_HEADS, X_HEAD_DIM)

    ycat, ssm_p, ret_p = _mixer_ab_prompt(p_gates, p_xqk, p_dt, a_log, d_skip_e, s_norm)
    x_tail = x.reshape(BATCH, SEQ, D_MODEL)[:, SEQ - SSD_CONV:].reshape(BATCH * SSD_CONV, D_MODEL)
    conv_p = _norm_proj(x_tail, n_mix, 0, w_ab_main, 0, AB_TN, col0=AB_XBC0, n_out=SSD_CONV_CH)
    conv_p = conv_p.reshape(BATCH, SSD_CONV, SSD_CONV_CH)[:, 1:]
    x = _proj_residual(x, ycat, w_out_ab, 0)
    x = _xattn_prompt(x, n_x, w_xq, w_xo, memk, memv, 0)
    x, o_x0 = _ffn(x, n_ffn2, f2_in, f2_out, 0, host=(q_x0, cache_mem_k, cache_mem_v, 0))

    xs_ = _proj_residual(xs_, o_x0.reshape(DEC_BATCH, D_MODEL), w_xo, 0)
    xs_ = _ffn(xs_, n_ffn2, f2_in, f2_out, 0)
    xs_ = _ffn(xs_, n_ffn1, f1_in, f1_out, 1)
    sc_main, sc_gate = _norm_proj(xs_, n_mix, 1, w_c_main, 0, 1024, w_c_gate)
    m_in = jnp.pad(state_mlstm_m[0], ((0, 0), (0, LANES - M_HEADS)))
    dpe, kw, mn_s, mm_s, den = _c_sample_prep(sc_main, sc_gate, state_mlstm_n.reshape(DEC_BATCH, M_QK), m_in,
                                              i_bias, f_bias)
    mc_s, num = _outer_state(dpe, kw, sc_main[:, :M_QK], sc_main[:, 2 * M_QK:2 * M_QK + M_V], state_mlstm_c[0])
    hout_s = _c_sample_post(num, den, sc_main, m_norm)
    xs_ = _proj_residual(xs_, hout_s, w_out_c, 0)
    q_x1 = _norm_proj(xs_, n_x, 1, w_xq, 1, 1024).reshape(DEC_BATCH, X_HEADS, X_HEAD_DIM)

    x, o_x1 = _ffn(x, n_ffn1, f1_in, f1_out, 1, host=(q_x1, cache_mem_k, cache_mem_v, 1))
    xs_ = _proj_residual(xs_, o_x1.reshape(DEC_BATCH, D_MODEL), w_xo, 1)
    y_sample = _ffn(xs_, n_ffn2, f2_in, f2_out, 1, norm_final).reshape(DEC_BATCH, 1, D_MODEL)
    pc_main, pc_gate = _c_inproj(x, n_mix, 1, w_c_main, w_c_gate)
    hout, mc_p, mn_p, mm_p = _mixer_c_prompt(pc_main.reshape(BATCH, SEQ, -1), pc_gate.reshape(BATCH, SEQ, -1),
                                             i_bias, f_bias, m_norm)
    x = _proj_residual(x, hout.reshape(BATCH * SEQ, M_V), w_out_c, 0)
    x = _xattn_prompt(x, n_x, w_xq, w_xo, memk, memv, 1)
    y_prompt = _ffn(x, n_ffn2, f2_in, f2_out, 1, norm_final).reshape(BATCH, SEQ, D_MODEL)

    kv_shape = (DEPTH, BATCH, MEM_LEN, X_HEADS, X_HEAD_DIM)
    return (y_prompt, y_sample, memk4.reshape(kv_shape), memv4.reshape(kv_shape),
            conv_p.reshape(1, BATCH, SSD_CONV - 1, SSD_CONV_CH),
            conv_s.reshape(1, DEC_BATCH, SSD_CONV - 1, SSD_CONV_CH),
            ssm_p.reshape(1, BATCH, SSD_HEADS, SSD_HEAD_DIM, SSD_D_STATE),
            ssm_s.reshape(1, DEC_BATCH, SSD_HEADS, SSD_HEAD_DIM, SSD_D_STATE),
            ret_p[None], ret_s[None], mc_p[None], mc_s[None],
            mn_p[None], mn_s.reshape(1, DEC_BATCH, M_HEADS, M_DK),
            mm_p[:, :M_HEADS, 0][None], mm_s[:, :M_HEADS][None])


def np_cumsum(sizes):
    out, acc = [], 0
    for s in sizes[:-1]:
        acc += s
        out.append(acc)
    return out
```

```python
import functools
import math

import jax
import jax.numpy as jnp
from jax import lax
from jax.experimental import pallas as pl
from jax.experimental.pallas import tpu as pltpu

F32 = jnp.float32
BF16 = jnp.bfloat16
EPS = 1e-6

D_MODEL = 1024
BATCH = 8
SEQ = 2048
DEPTH = 2
DEC_BATCH = 128
PAST_LEN = 16384
CHUNK = 128
D_FF = 2816
SSD_D_INNER = 2 * D_MODEL
SSD_HEAD_DIM = 64
SSD_HEADS = SSD_D_INNER // SSD_HEAD_DIM
SSD_GROUPS = 4
SSD_D_STATE = 128
SSD_CONV = 4
SSD_CONV_CH = SSD_D_INNER + 2 * SSD_GROUPS * SSD_D_STATE
RET_HEADS = 4
RET_QK = D_MODEL
RET_V = 2 * D_MODEL
RET_DK = RET_QK // RET_HEADS
RET_DV = RET_V // RET_HEADS
ROPE_BASE = 10000.0
AB_SIZES = (SSD_D_INNER, SSD_CONV_CH, SSD_HEADS, RET_QK, RET_QK, RET_V, RET_V)
M_HEADS = 4
M_QK = D_MODEL // 2
M_V = D_MODEL
M_DK = M_QK // M_HEADS
M_DV = M_V // M_HEADS
C_SIZES = (M_QK, M_QK, M_V, M_HEADS, M_HEADS, M_V)
MEM_LEN = 256
X_HEADS = 4
X_HEAD_DIM = D_MODEL // X_HEADS

LANES = 128
SUBLANES = 8
GROUP_W = SSD_D_INNER // SSD_GROUPS
HEADS_PER_GROUP = SSD_HEADS // SSD_GROUPS
ROW_TILE = 1024
FF_TILE = 256
SSD_SB = 16
STATE_BLOCK_BYTES = 8 * 1024 * 1024
XB = 4
CB = 2


def _cp(sem, mib):
    return pltpu.CompilerParams(dimension_semantics=sem, vmem_limit_bytes=mib * 1024 * 1024)


def _bf(x):
    return x.astype(BF16)


def _dot(a, b):
    return jnp.dot(a, b, preferred_element_type=F32)


def _dot_nt(a, b):
    return lax.dot_general(a, b, (((1,), (1,)), ((), ())), preferred_element_type=F32)


def _dot_tn(a, b):
    return lax.dot_general(a, b, (((0,), (0,)), ((), ())), preferred_element_type=F32)


def _rms(x):
    return x * lax.rsqrt(jnp.mean(x * x, axis=-1, keepdims=True) + EPS)


def _silu(x):
    return x * jax.nn.sigmoid(x)


def _softplus(x):
    return jnp.maximum(x, 0.0) + jnp.log1p(jnp.exp(-jnp.abs(x)))


def _split3(x):
    hi = x.astype(BF16)
    r = x - hi.astype(F32)
    mid = r.astype(BF16)
    lo = (r - mid.astype(F32)).astype(BF16)
    return hi, mid, lo


def _cumsum_rows(x):
    n = x.shape[0]
    r = lax.broadcasted_iota(jnp.int32, (n, n), 0)
    c = lax.broadcasted_iota(jnp.int32, (n, n), 1)
    t = jnp.where(r >= c, 1.0, 0.0).astype(BF16)
    hi, mid, lo = _split3(x)
    return _dot(t, hi) + _dot(t, mid) + _dot(t, lo)


def _lane_bcast(x, h, width=LANES):
    return jnp.broadcast_to(x[:, h:h + 1], (x.shape[0], width))


def _pair_expand(x, n_heads):
    rows = x.shape[0]
    lo = lax.broadcasted_iota(jnp.int32, (rows, LANES), 1) < SSD_HEAD_DIM
    return jnp.concatenate(
        [jnp.where(lo, _lane_bcast(x, 2 * j), _lane_bcast(x, 2 * j + 1)) for j in range(n_heads // 2)], axis=1)


def _pad_t(x):
    pad = jnp.zeros((LANES - x.shape[0], x.shape[1]), F32)
    return jnp.concatenate([x, pad], axis=0).T


def _ffn_body(*refs, nf, final, hosted):
    x_ref, g_ref, wi_ref, wo_ref = refs[:4]
    rest = list(refs[4:])
    fg_ref = rest.pop(0) if final else None
    if hosted:
        q_ref, k_ref, v_ref = rest[:3]
        o_ref, xo_ref = rest[3:]
    else:
        o_ref, = rest
    x = x_ref[...]
    xn = _bf(_rms(x) * g_ref[...])
    acc = None
    for f in range(nf):
        fs = slice(f * FF_TILE, (f + 1) * FF_TILE)
        g = _dot(xn, wi_ref[:, fs])
        u = _dot(xn, wi_ref[:, D_FF + f * FF_TILE:D_FF + (f + 1) * FF_TILE])
        t = _dot(_bf(_silu(g) * u), wo_ref[fs, :])
        acc = t if acc is None else acc + t
        if hosted and f % 3 == 1 and f // 3 < XB:
            _xattn_sample_one(f // 3, q_ref, k_ref, v_ref, xo_ref)
    y = x + 0.5 * acc
    if final:
        y = _rms(y) * fg_ref[...]
    o_ref[...] = y


def _ffn(x, g3, w_in, w_out, layer, final_g=None, host=None):
    m = x.shape[0]
    tm = min(m, ROW_TILE) if host is None else m // (DEC_BATCH // XB)
    nf = D_FF // FF_TILE
    once = pl.Buffered(1)
    in_specs = [
        pl.BlockSpec((tm, D_MODEL), lambda i: (i, 0)),
        pl.BlockSpec((None, 1, D_MODEL), lambda i: (layer, 0, 0)),
        pl.BlockSpec((None, D_MODEL, 2 * D_FF), lambda i: (layer, 0, 0), pipeline_mode=once),
        pl.BlockSpec((None, D_FF, D_MODEL), lambda i: (layer, 0, 0), pipeline_mode=once),
    ]
    args = [x, g3, w_in, w_out]
    if final_g is not None:
        in_specs.append(pl.BlockSpec((1, D_MODEL), lambda i: (0, 0)))
        args.append(final_g.reshape(1, D_MODEL))
    out_specs = [pl.BlockSpec((tm, D_MODEL), lambda i: (i, 0))]
    out_shape = [jax.ShapeDtypeStruct((m, D_MODEL), F32)]
    if host is not None:
        q, cache_k, cache_v, cl = host
        blk = pl.BlockSpec((None, XB, MEM_LEN, X_HEADS, X_HEAD_DIM), lambda i: (cl, i, 0, 0, 0))
        qo = pl.BlockSpec((XB, X_HEADS, X_HEAD_DIM), lambda i: (i, 0, 0))
        in_specs += [qo, blk, blk]
        args += [q, cache_k, cache_v]
        out_specs.append(qo)
        out_shape.append(jax.ShapeDtypeStruct((DEC_BATCH, X_HEADS, X_HEAD_DIM), F32))
    res = pl.pallas_call(
        functools.partial(_ffn_body, nf=nf, final=final_g is not None, hosted=host is not None),
        grid=(m // tm,),
        in_specs=in_specs,
        out_specs=out_specs,
        out_shape=out_shape,
        compiler_params=_cp(("parallel",), 56),
        name="ffn",
    )(*args)
    return res if host is not None else res[0]


def _norm_proj_body(*refs, small):
    if small:
        x_ref, g_ref, w_ref, ws_ref, o_ref, os_ref, xn_ref = refs
    else:
        x_ref, g_ref, w_ref, o_ref, xn_ref = refs
    n = pl.program_id(1)

    @pl.when(n == 0)
    def _():
        xn = _bf(_rms(x_ref[...]) * g_ref[...])
        xn_ref[...] = xn
        if small:
            os_ref[...] = _dot(xn, _bf(ws_ref[...]))

    o_ref[...] = _dot(xn_ref[...], _bf(w_ref[...]))


def _norm_proj(x, g3, glayer, w3, wlayer, tn, w_small=None, col0=0, n_out=None):
    m = x.shape[0]
    tm = min(m, ROW_TILE)
    n_out = w3.shape[-1] if n_out is None else n_out
    in_specs = [
        pl.BlockSpec((tm, D_MODEL), lambda i, n: (i, 0)),
        pl.BlockSpec((None, 1, D_MODEL), lambda i, n: (glayer, 0, 0)),
        pl.BlockSpec((None, D_MODEL, tn), lambda i, n: (wlayer, 0, n + col0)),
    ]
    args = [x, g3, w3]
    out_specs = [pl.BlockSpec((tm, tn), lambda i, n: (i, n))]
    out_shape = [jax.ShapeDtypeStruct((m, n_out), F32)]
    if w_small is not None:
        ns = w_small.shape[-1]
        in_specs.append(pl.BlockSpec((None, D_MODEL, ns), lambda i, n: (0, 0, 0)))
        args.append(w_small)
        out_specs.append(pl.BlockSpec((tm, ns), lambda i, n: (i, 0)))
        out_shape.append(jax.ShapeDtypeStruct((m, ns), F32))
    res = pl.pallas_call(
        functools.partial(_norm_proj_body, small=w_small is not None),
        grid=(m // tm, n_out // tn),
        in_specs=in_specs,
        out_specs=out_specs,
        out_shape=out_shape,
        scratch_shapes=[pltpu.VMEM((tm, D_MODEL), BF16)],
        compiler_params=_cp(("parallel", "arbitrary"), 48),
        name="norm_proj",
    )(*args)
    return res if w_small is not None else res[0]


AB_TN = 1024
AB_XBC0 = (SSD_D_INNER + 2 * RET_V) // AB_TN
AB_GATE_BLOCKS = AB_XBC0
AB_XQK_BLOCKS = SSD_CONV_CH // AB_TN + 2 * RET_QK // AB_TN
AB_ROW_TILE = 512


def _ab_gates_body(x_ref, g_ref, w_ref, *rest):
    eda_ref, xdt_ref, bm_ref, cm_ref, h_ref, o_ref, ho_ref, y_ref = rest
    eda_t = _pad_t(eda_ref[...])
    xdt_t = _pad_t(xdt_ref[...])
    cm = _bf(cm_ref[...])
    per = -(-SSD_SB // AB_GATE_BLOCKS)
    xn = _bf(_rms(x_ref[...]) * g_ref[...])
    nz = SSD_D_INNER // AB_TN
    nv = RET_V // AB_TN
    for n in range(AB_GATE_BLOCKS):
        cs = slice(n * AB_TN, (n + 1) * AB_TN)
        r = _dot(xn, w_ref[:, cs])
        o_ref[:, cs] = _bf(r) if nz <= n < nz + nv else _bf(_silu(r))
        for b in range(n * per, min(SSD_SB, (n + 1) * per)):
            _ssd_state_one(b, eda_t, xdt_t, cm, bm_ref, h_ref, ho_ref, y_ref)


def _ab_xqk_body(x_ref, g_ref, w_ref, ws_ref, dtb_ref, cw_ref, cb_ref, cos_ref, sin_ref,
                 o_ref, dt_ref, xpad, rbuf, ybuf, carry, *, tiles_per_seq):
    i = pl.program_id(0)
    tm = x_ref.shape[0]
    nslab = tm // SUBLANES
    pitch = nslab + SUBLANES
    nxb = SSD_CONV_CH // AB_TN
    ntap = SSD_CONV - 1

    @pl.when(i == 0)
    def _():
        carry[...] = jnp.zeros_like(carry)

    g = g_ref[...]
    xn = _bf(_rms(x_ref[...]) * g)
    dt_ref[...] = _softplus(_dot(xn, ws_ref[...]) + dtb_ref[...])
    for lb in range(D_MODEL // LANES):
        for s in range(SUBLANES):
            xpad[lb, s * pitch:s * pitch + nslab, :] = x_ref[s * nslab:(s + 1) * nslab, lb * LANES:(lb + 1) * LANES]
    xp = jnp.concatenate(
        [jnp.concatenate([xpad[lb, pl.ds(v, SUBLANES, stride=pitch), :] for lb in range(D_MODEL // LANES)], axis=1)
         for v in range(nslab)], axis=0)
    xnp = _bf(_rms(xp) * g)
    seq_start = i % tiles_per_seq == 0
    first = lax.broadcasted_iota(jnp.int32, (SUBLANES, AB_TN), 0) == 0
    halo = ntap * SUBLANES
    for n in range(nxb):
        cs = slice(n * AB_TN, (n + 1) * AB_TN)
        rbuf[n, halo:halo + tm, :] = _dot(xnp, w_ref[:, cs])
        prev_rows = jnp.where(seq_start, 0.0, carry[n])
        for k in range(ntap):
            hi = rbuf[n, tm + k * SUBLANES:tm + (k + 1) * SUBLANES, :]
            rbuf[n, k * SUBLANES:(k + 1) * SUBLANES, :] = jnp.where(first, prev_rows[k:k + 1, :],
                                                                     pltpu.roll(hi, 1, axis=0))
            carry[n, k:k + 1, :] = hi[SUBLANES - 1:SUBLANES, :]
        w = cw_ref[:, cs]
        conv = cb_ref[:, cs] + rbuf[n, halo:halo + tm, :] * w[ntap:ntap + 1, :]
        for j in range(1, SSD_CONV):
            off = halo - j * SUBLANES
            conv = conv + rbuf[n, off:off + tm, :] * w[ntap - j:ntap - j + 1, :]
        y = _silu(conv)
        for lb in range(AB_TN // LANES):
            ls = slice(lb * LANES, (lb + 1) * LANES)
            for v in range(nslab):
                ybuf[n, lb, pl.ds(v, SUBLANES, stride=pitch), :] = y[v * SUBLANES:(v + 1) * SUBLANES, ls]
            for s in range(SUBLANES):
                o_ref[s * nslab:(s + 1) * nslab, n * AB_TN + ls.start:n * AB_TN + ls.stop] = _bf(
                    ybuf[n, lb, s * pitch:s * pitch + nslab, :])
    cos = cos_ref[...]
    sin = sin_ref[...]
    for n in range(nxb, AB_XQK_BLOCKS):
        r = _dot(xn, w_ref[:, n * AB_TN:(n + 1) * AB_TN])
        scale = 1.0 if n == nxb else RET_DK ** -0.5
        for h in range(AB_TN // RET_DK):
            a = slice(h * RET_DK, h * RET_DK + LANES)
            b = slice(h * RET_DK + LANES, (h + 1) * RET_DK)
            x1, x2 = r[:, a], r[:, b]
            o_ref[:, n * AB_TN + a.start:n * AB_TN + a.stop] = _bf((x1 * cos - x2 * sin) * scale)
            o_ref[:, n * AB_TN + b.start:n * AB_TN + b.stop] = _bf((x1 * sin + x2 * cos) * scale)


def _ab_inproj(x, g3, w_gates, w_xqk, w_dt, dt_bias, conv_w, conv_b, cos, sin, ssd_step):
    m = x.shape[0]
    tm = AB_ROW_TILE
    tps = SEQ // tm
    nxb = SSD_CONV_CH // AB_TN
    once = pl.Buffered(1)

    def full(a):
        return pl.BlockSpec(a.shape, lambda i: (0,) * a.ndim, pipeline_mode=once)

    xspec = pl.BlockSpec((tm, D_MODEL), lambda i: (i, 0))
    gspec = pl.BlockSpec((None, 1, D_MODEL), lambda i: (0, 0, 0))
    eda, xdt, bm_s, cm_s, state = ssd_step
    ng = SSD_GROUPS
    assert m // tm == (DEC_BATCH // SSD_SB) * ng
    vec = pl.BlockSpec((SSD_SB, GROUP_W), lambda i: (i // ng, i % ng))
    bc = pl.BlockSpec((SSD_SB, SSD_D_STATE), lambda i: (i // ng, i % ng))
    st = pl.BlockSpec((SSD_SB, GROUP_W, SSD_D_STATE), lambda i: (i // ng, i % ng, 0))
    gates, ssm_s, y_s = pl.pallas_call(
        _ab_gates_body,
        grid=(m // tm,),
        in_specs=[xspec, gspec, full(w_gates), vec, vec, bc, bc, st],
        out_specs=[pl.BlockSpec((tm, AB_GATE_BLOCKS * AB_TN), lambda i: (i, 0)), st, vec],
        out_shape=[jax.ShapeDtypeStruct((m, AB_GATE_BLOCKS * AB_TN), BF16),
                   jax.ShapeDtypeStruct(state.shape, F32),
                   jax.ShapeDtypeStruct((DEC_BATCH, SSD_D_INNER), F32)],
        compiler_params=_cp(("parallel",), 56),
        name="ab_gates",
    )(x, g3, w_gates, eda, xdt, bm_s, cm_s, state)
    xqk, dt = pl.pallas_call(
        functools.partial(_ab_xqk_body, tiles_per_seq=tps),
        grid=(m // tm,),
        in_specs=[xspec, gspec, full(w_xqk), full(w_dt), full(dt_bias), full(conv_w), full(conv_b),
                  pl.BlockSpec((tm, LANES), lambda i: (i % tps, 0)),
                  pl.BlockSpec((tm, LANES), lambda i: (i % tps, 0))],
        out_specs=[pl.BlockSpec((tm, AB_XQK_BLOCKS * AB_TN), lambda i: (i, 0)),
                   pl.BlockSpec((tm, LANES), lambda i: (i, 0))],
        out_shape=[jax.ShapeDtypeStruct((m, AB_XQK_BLOCKS * AB_TN), BF16), jax.ShapeDtypeStruct((m, LANES), F32)],
        scratch_shapes=[pltpu.VMEM((D_MODEL // LANES, tm + SUBLANES * SUBLANES, LANES), F32),
                        pltpu.VMEM((nxb, (SSD_CONV - 1) * SUBLANES + tm, AB_TN), F32),
                        pltpu.VMEM((nxb, AB_TN // LANES, tm + SUBLANES * SUBLANES, LANES), F32),
                        pltpu.VMEM((nxb, SUBLANES, AB_TN), F32)],
        compiler_params=_cp(("arbitrary",), 56),
        name="ab_xqk",
    )(x, g3, w_xqk, w_dt, dt_bias, conv_w, conv_b, cos, sin)
    return gates, xqk, dt, ssm_s, y_s


def _c_inproj_body(x_ref, g_ref, w_ref, wg_ref, o_ref, gt_ref):
    xn = _bf(_rms(x_ref[...]) * g_ref[...])
    gt_ref[...] = _dot(xn, wg_ref[...])
    o_ref[:, :M_QK] = _bf(_dot(xn, w_ref[:, :M_QK]))
    o_ref[:, M_QK:2 * M_QK] = _bf(_dot(xn, w_ref[:, M_QK:2 * M_QK]) * (M_DK ** -0.5))
    o_ref[:, 2 * M_QK:2 * M_QK + M_V] = _bf(_dot(xn, w_ref[:, 2 * M_QK:2 * M_QK + M_V]))
    o_ref[:, 2 * M_QK + M_V:] = _bf(jax.nn.sigmoid(_dot(xn, w_ref[:, 2 * M_QK + M_V:])))


def _c_inproj(x, g3, glayer, w_main, w_gate):
    m = x.shape[0]
    tm = ROW_TILE
    once = pl.Buffered(1)
    n_out = w_main.shape[-1]
    n_gate = w_gate.shape[-1]
    return pl.pallas_call(
        _c_inproj_body,
        grid=(m // tm,),
        in_specs=[
            pl.BlockSpec((tm, D_MODEL), lambda i: (i, 0)),
            pl.BlockSpec((None, 1, D_MODEL), lambda i: (glayer, 0, 0)),
            pl.BlockSpec((None, D_MODEL, n_out), lambda i: (0, 0, 0), pipeline_mode=once),
            pl.BlockSpec((None, D_MODEL, n_gate), lambda i: (0, 0, 0), pipeline_mode=once),
        ],
        out_specs=[pl.BlockSpec((tm, n_out), lambda i: (i, 0)), pl.BlockSpec((tm, n_gate), lambda i: (i, 0))],
        out_shape=[jax.ShapeDtypeStruct((m, n_out), BF16), jax.ShapeDtypeStruct((m, n_gate), F32)],
        compiler_params=_cp(("parallel",), 48),
        name="c_inproj",
    )(x, g3, w_main, w_gate)


def _proj_res_body(x_ref, y_ref, w_ref, o_ref):
    o_ref[...] = x_ref[...] + _dot(_bf(y_ref[...]), w_ref[...])


def _proj_residual(x, y, w3, layer):
    m = x.shape[0]
    tm = min(m, ROW_TILE)
    kdim = y.shape[1]
    return pl.pallas_call(
        _proj_res_body,
        grid=(m // tm,),
        in_specs=[
            pl.BlockSpec((tm, D_MODEL), lambda i: (i, 0)),
            pl.BlockSpec((tm, kdim), lambda i: (i, 0)),
            pl.BlockSpec((None, kdim, D_MODEL), lambda i: (layer, 0, 0), pipeline_mode=pl.Buffered(1)),
        ],
        out_specs=pl.BlockSpec((tm, D_MODEL), lambda i: (i, 0)),
        out_shape=jax.ShapeDtypeStruct((m, D_MODEL), F32),
        compiler_params=_cp(("parallel",), 56),
        name="proj_residual",
    )(x, y, w3)


def _mem_kv_body(x_ref, g_ref, w_ref, k_ref, v_ref, k4_ref, v4_ref):
    xn = _bf(_rms(x_ref[...]) * g_ref[...])
    kv = _dot(xn, _bf(w_ref[...]))
    k_ref[...] = kv[:, :D_MODEL]
    v_ref[...] = kv[:, D_MODEL:]
    for h in range(X_HEADS):
        k4_ref[:, h, :] = kv[:, h * X_HEAD_DIM:(h + 1) * X_HEAD_DIM]
        v4_ref[:, h, :] = kv[:, D_MODEL + h * X_HEAD_DIM:D_MODEL + (h + 1) * X_HEAD_DIM]


def _mem_kv(mem2d, g3, w_xkv):
    m = mem2d.shape[0]
    tm = 512
    shp = jax.ShapeDtypeStruct((DEPTH, m, D_MODEL), F32)
    shp4 = jax.ShapeDtypeStruct((DEPTH, m, X_HEADS, X_HEAD_DIM), F32)
    return pl.pallas_call(
        _mem_kv_body,
        grid=(DEPTH, m // tm),
        in_specs=[
            pl.BlockSpec((tm, D_MODEL), lambda l, i: (i, 0)),
            pl.BlockSpec((None, 1, D_MODEL), lambda l, i: (l, 0, 0)),
            pl.BlockSpec((None, D_MODEL, 2 * D_MODEL), lambda l, i: (l, 0, 0)),
        ],
        out_specs=[pl.BlockSpec((None, tm, D_MODEL), lambda l, i: (l, i, 0))] * 2
        + [pl.BlockSpec((None, tm, X_HEADS, X_HEAD_DIM), lambda l, i: (l, i, 0, 0))] * 2,
        out_shape=[shp, shp, shp4, shp4],
        compiler_params=_cp(("arbitrary", "arbitrary"), 48),
        name="mem_kv",
    )(mem2d, g3, w_xkv)


def _xattn_prompt_body(x_ref, g_ref, wq_ref, wo_ref, k_ref, v_ref, o_ref):
    x = x_ref[...]
    xn = _bf(_rms(x) * g_ref[...])
    q = _dot(xn, _bf(wq_ref[...]))
    k = _bf(k_ref[...])
    v = _bf(v_ref[...])
    outs = []
    for h in range(X_HEADS):
        sl = slice(h * X_HEAD_DIM, (h + 1) * X_HEAD_DIM)
        s = _dot_nt(_bf(q[:, sl]), k[:, sl]) * (X_HEAD_DIM ** -0.5)
        e = jnp.exp(s - jnp.max(s, axis=-1, keepdims=True))
        p = e / jnp.sum(e, axis=-1, keepdims=True)
        outs.append(_bf(_dot(_bf(p), v[:, sl])))
    o_ref[...] = x + _dot(jnp.concatenate(outs, axis=1), _bf(wo_ref[...]))


def _xattn_prompt(x, g3, w_xq, w_xo, memk, memv, layer):
    tq = ROW_TILE
    nq = SEQ // tq
    return pl.pallas_call(
        _xattn_prompt_body,
        grid=(BATCH, nq),
        in_specs=[
            pl.BlockSpec((tq, D_MODEL), lambda b, j: (b * nq + j, 0)),
            pl.BlockSpec((None, 1, D_MODEL), lambda b, j: (layer, 0, 0)),
            pl.BlockSpec((None, D_MODEL, D_MODEL), lambda b, j: (layer, 0, 0)),
            pl.BlockSpec((None, D_MODEL, D_MODEL), lambda b, j: (layer, 0, 0)),
            pl.BlockSpec((None, MEM_LEN, D_MODEL), lambda b, j: (layer, b, 0)),
            pl.BlockSpec((None, MEM_LEN, D_MODEL), lambda b, j: (layer, b, 0)),
        ],
        out_specs=pl.BlockSpec((tq, D_MODEL), lambda b, j: (b * nq + j, 0)),
        out_shape=jax.ShapeDtypeStruct((BATCH * SEQ, D_MODEL), F32),
        compiler_params=_cp(("parallel", "arbitrary"), 48),
        name="xattn_prompt",
    )(x, g3, w_xq, w_xo, memk, memv)


def _xattn_sample_one(b, q_ref, k_ref, v_ref, o_ref):
    s = jnp.sum(k_ref[b] * (q_ref[b] * (X_HEAD_DIM ** -0.5))[None], axis=-1, keepdims=True)
    e = jnp.exp(s - jnp.max(s, axis=0, keepdims=True))
    o_ref[b] = jnp.sum(e * v_ref[b], axis=0) / jnp.sum(e, axis=0)


def _ret_log_gamma(h):
    return math.log1p(-(2.0 ** (-5.0 - h)))


def _ab_prompt_body(z_ref, rv_ref, rg_ref, xbc_ref, rq_ref, rk_ref, dt_ref, alog_ref, dsk_ref, nrm_ref, e_ref,
                    y_ref, h_ref, s_ref):
    c = pl.program_id(1)

    @pl.when(c == 0)
    def _():
        h_ref[...] = jnp.zeros_like(h_ref)
        s_ref[...] = jnp.zeros_like(s_ref)

    xs = xbc_ref[:, :SSD_D_INNER].astype(F32)
    bm = xbc_ref[:, SSD_D_INNER:SSD_D_INNER + GROUP_W]
    cm = xbc_ref[:, SSD_D_INNER + GROUP_W:]

    dt = dt_ref[...]
    da = dt * (-jnp.exp(alog_ref[...]))
    cs = _cumsum_rows(da)
    cs_t = cs.T
    e = e_ref[...]
    cs_e = sum(_dot(limb, e) for limb in _split3(cs))
    dt_e = sum(_dot(limb, e) for limb in _split3(dt))
    row = lax.broadcasted_iota(jnp.int32, (CHUNK, CHUNK), 0)
    col = lax.broadcasted_iota(jnp.int32, (CHUNK, CHUNK), 1)
    tri = row >= col
    lo = col < SSD_HEAD_DIM
    for g in range(SSD_GROUPS):
        gs = slice(g * GROUP_W, (g + 1) * GROUP_W)
        ns = slice(g * SSD_D_STATE, (g + 1) * SSD_D_STATE)
        cmg = cm[:, ns]
        bmg = bm[:, ns]
        att = _dot_nt(cmg, bmg)
        hprev = h_ref[gs, :]
        yint = _dot_nt(cmg, _bf(hprev))
        ys, wxs, css = [], [], []
        for j in range(HEADS_PER_GROUP // 2):
            h0 = g * HEADS_PER_GROUP + 2 * j
            cb0 = _lane_bcast(cs, h0)
            cb1 = _lane_bcast(cs, h0 + 1)
            off = g * GROUP_W + j * LANES
            cs_p = cs_e[:, off:off + LANES]
            dt_p = dt_e[:, off:off + LANES]
            xdt = xs[:, off:off + LANES] * dt_p
            d0 = jnp.exp(jnp.where(tri, cb0 - cs_t[h0:h0 + 1, :], -jnp.inf))
            d1 = jnp.exp(jnp.where(tri, cb1 - cs_t[h0 + 1:h0 + 2, :], -jnp.inf))
            yy = _dot(jnp.concatenate([_bf(att * d0), _bf(att * d1)], axis=0), _bf(xdt))
            ys.append(jnp.where(lo, yy[:CHUNK], yy[CHUNK:]) + yint[:, j * LANES:(j + 1) * LANES] * jnp.exp(cs_p))
            wxs.append(_bf(xdt * jnp.exp(cs_p[CHUNK - 1:CHUNK, :] - cs_p)))
            css.append(cs_p)
        cs_g = jnp.concatenate(css, axis=1)
        last_t = jnp.broadcast_to(cs_g[CHUNK - 1:CHUNK, :], (CHUNK, GROUP_W)).T
        h_ref[gs, :] = hprev * jnp.exp(last_t) + _dot_tn(jnp.concatenate(wxs, axis=1), bmg)
        yg = jnp.concatenate(ys, axis=1)
        yg = (yg + xs[:, gs] * dsk_ref[:, gs]) * z_ref[:, gs].astype(F32)
        y_ref[:, gs] = _bf(_rms(yg) * nrm_ref[:, gs])

    tcol = row.astype(F32)
    diff = tcol - col.astype(F32)
    for h in range(RET_HEADS):
        lg = _ret_log_gamma(h)
        ks = slice(h * RET_DK, (h + 1) * RET_DK)
        vs = slice(h * RET_DV, (h + 1) * RET_DV)
        qb = rq_ref[:, ks]
        kb = rk_ref[:, ks]
        decay = jnp.exp(jnp.where(tri, diff * lg, -jnp.inf))
        att = _dot_nt(qb, kb) * decay
        vb = rv_ref[:, vs]
        s_prev = s_ref[h]
        inner = jnp.exp((tcol + 1.0) * lg)
        r = _dot(_bf(att), vb) + _dot(qb, _bf(s_prev)) * jnp.concatenate([inner] * (RET_DV // LANES), axis=1)
        tail_w = jnp.exp((CHUNK - 1.0 - tcol) * lg)
        kt = _bf(kb.astype(F32) * jnp.concatenate([tail_w] * (RET_DK // LANES), axis=1))
        s_ref[h] = s_prev * math.exp(CHUNK * lg) + _dot_tn(kt, vb)
        os = slice(SSD_D_INNER + h * RET_DV, SSD_D_INNER + (h + 1) * RET_DV)
        y_ref[:, os] = _bf(rg_ref[:, vs].astype(F32) * _rms(r))


def _mixer_ab_prompt(p_gates, p_xqk, p_dt, a_log, d_skip_e, ssd_norm):
    nc = SEQ // CHUNK
    m = BATCH * SEQ

    def rowspec(width, cb):
        return pl.BlockSpec((CHUNK, width), lambda b, c: (b * nc + c, cb))

    def full(a):
        return pl.BlockSpec(a.shape, lambda b, c: (0,) * a.ndim)

    e_heads = (jnp.arange(SSD_D_INNER)[None, :] // SSD_HEAD_DIM == jnp.arange(LANES)[:, None]).astype(BF16)
    params = [a_log, d_skip_e, ssd_norm, e_heads]
    return pl.pallas_call(
        _ab_prompt_body,
        grid=(BATCH, nc),
        in_specs=[rowspec(SSD_D_INNER, 0), rowspec(RET_V, 1), rowspec(RET_V, 2), rowspec(SSD_CONV_CH, 0),
                  rowspec(RET_QK, SSD_CONV_CH // RET_QK), rowspec(RET_QK, SSD_CONV_CH // RET_QK + 1),
                  rowspec(LANES, 0)] + [full(a) for a in params],
        out_specs=[
            pl.BlockSpec((CHUNK, SSD_D_INNER + RET_V), lambda b, c: (b * nc + c, 0)),
            pl.BlockSpec((None, SSD_D_INNER, SSD_D_STATE), lambda b, c: (b, 0, 0)),
            pl.BlockSpec((None, RET_HEADS, RET_DK, RET_DV), lambda b, c: (b, 0, 0, 0)),
        ],
        out_shape=[
            jax.ShapeDtypeStruct((m, SSD_D_INNER + RET_V), BF16),
            jax.ShapeDtypeStruct((BATCH, SSD_D_INNER, SSD_D_STATE), F32),
            jax.ShapeDtypeStruct((BATCH, RET_HEADS, RET_DK, RET_DV), F32),
        ],
        compiler_params=_cp(("parallel", "arbitrary"), 48),
        name="mixer_ab_prompt",
    )(p_gates, p_gates, p_gates, p_xqk, p_xqk, p_xqk, p_dt, *params)


def _c_prompt_body(q_ref, k_ref, v_ref, o_ref, gt_ref, ib_ref, fb_ref, nrm_ref,
                   h_ref, c_ref, n_ref, m_ref, m_s):
    @pl.when(pl.program_id(1) == 0)
    def _():
        c_ref[...] = jnp.zeros_like(c_ref)
        n_ref[...] = jnp.zeros_like(n_ref)
        m_s[...] = jnp.zeros_like(m_s)

    row = lax.broadcasted_iota(jnp.int32, (CHUNK, CHUNK), 0)
    col = lax.broadcasted_iota(jnp.int32, (CHUNK, CHUNK), 1)
    tri = row >= col
    seqs = range(CB)
    ipre = [gt_ref[s, :, :LANES] + ib_ref[...] for s in seqs]
    lf = [-_softplus(-(gt_ref[s, :, LANES:] + fb_ref[...])) for s in seqs]
    b = [_cumsum_rows(lf[s]) for s in seqs]
    g = [ipre[s] - b[s] for s in seqs]
    g_t = [g[s].T for s in seqs]
    b_t = [b[s].T for s in seqs]
    cmax = list(g_t)
    sh = 1
    while sh < CHUNK:
        cmax = [jnp.maximum(cmax[s], jnp.where(col >= sh, pltpu.roll(cmax[s], sh, axis=1), -jnp.inf)) for s in seqs]
        sh *= 2
    m_prev = [m_s[s] for s in seqs]
    mt_t = [b_t[s] + jnp.maximum(m_prev[s], cmax[s]) for s in seqs]
    mt = [mt_t[s].T for s in seqs]
    m_prev_c = [m_prev[s].T for s in seqs]
    inter = [jnp.exp(b[s] + m_prev_c[s] - mt[s]) for s in seqs]
    emt = [jnp.exp(-mt[s]) for s in seqs]
    wl = [jnp.exp(g[s] + b[s][CHUNK - 1:CHUNK, :] - mt[s][CHUNK - 1:CHUNK, :]) for s in seqs]
    bm = [b[s] - mt[s] for s in seqs]
    m_new = [_lane_bcast(mt_t[s], CHUNK - 1) for s in seqs]
    dp_t = [jnp.exp(_lane_bcast(b_t[s], CHUNK - 1) + m_prev[s] - m_new[s]) for s in seqs]
    for s in seqs:
        m_s[s] = m_new[s]
        m_ref[s] = m_new[s][0:SUBLANES, :]
    for h in range(M_HEADS):
        ks = slice(h * M_DK, (h + 1) * M_DK)
        vs = slice(h * M_DV, (h + 1) * M_DV)
        for s in seqs:
            wgt = jnp.exp(jnp.where(tri, g_t[s][h:h + 1, :] + bm[s][:, h:h + 1], -jnp.inf))
            qb = q_ref[s, :, ks]
            kb = k_ref[s, :, ks]
            vb = v_ref[s, :, vs]
            a = _dot_nt(qb, kb) * wgt
            c_prev = c_ref[s, h]
            n_prev = n_ref[s, h:h + 1, :]
            ic = inter[s][:, h:h + 1]
            num = _dot(_bf(a), vb) + _dot(qb, _bf(c_prev)) * ic
            den = (jnp.sum(a, axis=1, keepdims=True)
                   + jnp.sum(qb.astype(F32) * n_prev, axis=1, keepdims=True) * ic)
            hc = num / jnp.maximum(jnp.abs(den), emt[s][:, h:h + 1])
            kw = kb.astype(F32) * wl[s][:, h:h + 1]
            dp_row = dp_t[s][h:h + 1, :]
            c_ref[s, h] = c_prev * jnp.concatenate([dp_row] * (M_DV // LANES), axis=1) + _dot_tn(_bf(kw), vb)
            n_ref[s, h:h + 1, :] = n_prev * dp_row + jnp.sum(kw, axis=0, keepdims=True)
            h_ref[s, :, vs] = _bf(o_ref[s, :, vs].astype(F32) * (_rms(hc) * nrm_ref[:, vs]))


def _mixer_c_prompt(p_main, p_gate, i_bias, f_bias, norm_g):
    nc = SEQ // CHUNK

    def rowspec(width, cb):
        return pl.BlockSpec((CB, CHUNK, width), lambda b, c: (b, c, cb))

    def full(a):
        return pl.BlockSpec(a.shape, lambda b, c: (0,) * a.ndim)

    params = [i_bias, f_bias, norm_g]
    return pl.pallas_call(
        _c_prompt_body,
        grid=(BATCH // CB, nc),
        in_specs=[rowspec(M_QK, 0), rowspec(M_QK, 1), rowspec(M_V, 1), rowspec(M_V, 2), rowspec(2 * LANES, 0)]
        + [full(a) for a in params],
        out_specs=[
            pl.BlockSpec((CB, CHUNK, M_V), lambda b, c: (b, c, 0)),
            pl.BlockSpec((CB, M_HEADS, M_DK, M_DV), lambda b, c: (b, 0, 0, 0)),
            pl.BlockSpec((CB, M_HEADS, M_DK), lambda b, c: (b, 0, 0)),
            pl.BlockSpec((CB, SUBLANES, LANES), lambda b, c: (b, 0, 0)),
        ],
        out_shape=[
            jax.ShapeDtypeStruct((BATCH, SEQ, M_V), BF16),
            jax.ShapeDtypeStruct((BATCH, M_HEADS, M_DK, M_DV), F32),
            jax.ShapeDtypeStruct((BATCH, M_HEADS, M_DK), F32),
            jax.ShapeDtypeStruct((BATCH, SUBLANES, LANES), F32),
        ],
        scratch_shapes=[pltpu.VMEM((CB, CHUNK, LANES), F32)],
        compiler_params=_cp(("parallel", "arbitrary"), 48),
        name="mixer_c_prompt",
    )(p_main, p_main, p_main, p_main, p_gate, *params)


def _ab_sample_prep_body(xbc_ref, rq_ref, rk_ref, dt_ref, cst_ref, cos_ref, sin_ref,
                         cw_ref, cb_ref, dtb_ref, alog_ref,
                         conv_ref, xs_ref, xdt_ref, eda_ref, bm_ref, cm_ref, q_ref, k_ref, gam_ref):
    ch = SSD_CONV_CH
    u = xbc_ref[...]
    w = cw_ref[...]
    b0 = cst_ref[:, 0:ch]
    b1 = cst_ref[:, ch:2 * ch]
    b2 = cst_ref[:, 2 * ch:3 * ch]
    conv = cb_ref[...] + (((b0 * w[0:1, :] + b1 * w[1:2, :]) + b2 * w[2:3, :]) + u * w[3:4, :])
    conv_ref[:, 0:ch] = b1
    conv_ref[:, ch:2 * ch] = b2
    conv_ref[:, 2 * ch:3 * ch] = u
    xbc = _silu(conv)
    xs = xbc[:, :SSD_D_INNER]
    xs_ref[...] = xs
    bm_ref[...] = xbc[:, SSD_D_INNER:SSD_D_INNER + GROUP_W]
    cm_ref[...] = xbc[:, SSD_D_INNER + GROUP_W:]
    dt = _softplus(dt_ref[...] + dtb_ref[...])
    eda = jnp.exp(dt * (-jnp.exp(alog_ref[...])))
    xdt_ref[...] = xs * _pair_expand(dt, SSD_HEADS)
    eda_ref[...] = _pair_expand(eda, SSD_HEADS)
    cos = cos_ref[...]
    sin = sin_ref[...]
    for h in range(RET_HEADS):
        a = slice(h * RET_DK, h * RET_DK + LANES)
        b = slice(h * RET_DK + LANES, (h + 1) * RET_DK)
        q1, q2 = rq_ref[:, a], rq_ref[:, b]
        k1, k2 = rk_ref[:, a], rk_ref[:, b]
        q_ref[:, a] = q1 * cos - q2 * sin
        q_ref[:, b] = q1 * sin + q2 * cos
        k_ref[:, a] = (k1 * cos - k2 * sin) * (RET_DK ** -0.5)
        k_ref[:, b] = (k1 * sin + k2 * cos) * (RET_DK ** -0.5)
        gam_ref[:, h * RET_DK:(h + 1) * RET_DK] = jnp.full((DEC_BATCH, RET_DK), math.exp(_ret_log_gamma(h)), F32)


def _ab_sample_prep(p_main, p_dt, conv_state, cos, sin, conv_w, conv_b, dt_bias, a_log):
    n = DEC_BATCH

    def colspec(width, cb):
        return pl.BlockSpec((n, width), lambda i: (0, cb))

    def full(a):
        return pl.BlockSpec(a.shape, lambda i: (0,) * a.ndim)

    small = [conv_state, cos, sin, conv_w, conv_b, dt_bias, a_log]

    def out(width):
        return jax.ShapeDtypeStruct((n, width), F32)

    widths = [(SSD_CONV - 1) * SSD_CONV_CH, SSD_D_INNER, SSD_D_INNER, SSD_D_INNER, GROUP_W, GROUP_W,
              RET_QK, RET_QK, RET_QK]
    return pl.pallas_call(
        _ab_sample_prep_body,
        grid=(1,),
        in_specs=[colspec(SSD_CONV_CH, 2), colspec(RET_QK, 9), colspec(RET_QK, 10), full(p_dt)]
        + [full(a) for a in small],
        out_specs=[pl.BlockSpec((n, wd), lambda i: (0, 0)) for wd in widths],
        out_shape=[out(wd) for wd in widths],
        compiler_params=_cp(("arbitrary",), 48),
        name="ab_sample_prep",
    )(p_main, p_main, p_main, p_dt, *small)


def _ssd_state_one(b, eda_t, xdt_t, cm, bm_ref, h_ref, ho_ref, y_ref):
    hn = h_ref[b] * eda_t[:, b:b + 1] + xdt_t[:, b:b + 1] * bm_ref[b:b + 1, :]
    ho_ref[b] = hn
    y_ref[b:b + 1, :] = _dot_nt(cm, _bf(hn))[b:b + 1, :]


def _outer_state_body(d_ref, k_ref, q_ref, v_ref, s_ref, so_ref, o_ref):
    d_t = _pad_t(d_ref[...])
    k_t = _pad_t(k_ref[...])
    q_t = _pad_t(q_ref[...])
    for b in range(d_ref.shape[0]):
        sn = s_ref[b] * d_t[:, b:b + 1] + k_t[:, b:b + 1] * v_ref[b:b + 1, :]
        so_ref[b] = sn
        o_ref[b:b + 1, :] = jnp.sum(sn * q_t[:, b:b + 1], axis=0, keepdims=True)


def _outer_state(d, k, q, v, state):
    _, nh, dk, dv = state.shape
    sb = STATE_BLOCK_BYTES // (dk * dv * 4)
    kv = pl.BlockSpec((sb, dk), lambda i, h: (i, h))
    vv = pl.BlockSpec((sb, dv), lambda i, h: (i, h))
    st = pl.BlockSpec((sb, None, dk, dv), lambda i, h: (i, h, 0, 0))
    return pl.pallas_call(
        _outer_state_body,
        grid=(DEC_BATCH // sb, nh),
        in_specs=[kv, kv, kv, vv, st],
        out_specs=[st, vv],
        out_shape=[jax.ShapeDtypeStruct(state.shape, F32), jax.ShapeDtypeStruct((DEC_BATCH, nh * dv), F32)],
        compiler_params=_cp(("parallel", "arbitrary"), 48),
        name="outer_state",
    )(d, k, q, v, state)


def _ab_sample_post_body(y_ref, xs_ref, z_ref, r_ref, rg_ref, dsk_ref, nrm_ref, o_ref):
    for g in range(SSD_GROUPS):
        gs = slice(g * GROUP_W, (g + 1) * GROUP_W)
        yg = (y_ref[:, gs] + xs_ref[:, gs] * dsk_ref[:, gs]) * _silu(z_ref[:, gs])
        o_ref[:, gs] = _bf(_rms(yg) * nrm_ref[:, gs])
    for h in range(RET_HEADS):
        vs = slice(h * RET_DV, (h + 1) * RET_DV)
        os = slice(SSD_D_INNER + h * RET_DV, SSD_D_INNER + (h + 1) * RET_DV)
        o_ref[:, os] = _bf(_silu(rg_ref[:, vs]) * _rms(r_ref[:, vs]))


def _ab_sample_post(y, xs, p_main, r, d_skip_e, ssd_norm):
    n = DEC_BATCH

    def full(a):
        return pl.BlockSpec(a.shape, lambda i: (0,) * a.ndim)

    return pl.pallas_call(
        _ab_sample_post_body,
        grid=(1,),
        in_specs=[full(y), full(xs), pl.BlockSpec((n, SSD_D_INNER), lambda i: (0, 0)), full(r),
                  pl.BlockSpec((n, RET_V), lambda i: (0, 2)), full(d_skip_e), full(ssd_norm)],
        out_specs=pl.BlockSpec((n, SSD_D_INNER + RET_V), lambda i: (0, 0)),
        out_shape=jax.ShapeDtypeStruct((n, SSD_D_INNER + RET_V), BF16),
        compiler_params=_cp(("arbitrary",), 48),
        name="ab_sample_post",
    )(y, xs, p_main, r, p_main, d_skip_e, ssd_norm)


def _c_sample_prep_body(q_ref, k_ref, gt_ref, n_ref, m_ref, ib_ref, fb_ref,
                        dpe_ref, kw_ref, nn_ref, mn_ref, dn_ref):
    ipre = gt_ref[:, :LANES] + ib_ref[...]
    lf = -_softplus(-(gt_ref[:, LANES:] + fb_ref[...]))
    m_prev = m_ref[...]
    mt = jnp.maximum(lf + m_prev, ipre)
    wgt = jnp.exp(ipre - mt)
    dp = jnp.exp(lf + m_prev - mt)
    emt = jnp.exp(-mt)
    mn_ref[...] = mt
    for h in range(M_HEADS):
        ks = slice(h * M_DK, (h + 1) * M_DK)
        dpe = _lane_bcast(dp, h)
        kw = k_ref[:, ks] * (M_DK ** -0.5) * _lane_bcast(wgt, h)
        nn = n_ref[:, ks] * dpe + kw
        den = jnp.sum(nn * q_ref[:, ks], axis=1, keepdims=True)
        dpe_ref[:, ks] = dpe
        kw_ref[:, ks] = kw
        nn_ref[:, ks] = nn
        dn_ref[:, h * M_DV:(h + 1) * M_DV] = jnp.broadcast_to(
            jnp.maximum(jnp.abs(den), emt[:, h:h + 1]), (DEC_BATCH, M_DV))


def _c_sample_prep(p_main, p_gate, n_state, m_state, i_bias, f_bias):
    n = DEC_BATCH

    def full(a):
        return pl.BlockSpec(a.shape, lambda i: (0,) * a.ndim)

    widths = [M_QK, M_QK, M_QK, LANES, M_V]
    return pl.pallas_call(
        _c_sample_prep_body,
        grid=(1,),
        in_specs=[pl.BlockSpec((n, M_QK), lambda i: (0, 0)), pl.BlockSpec((n, M_QK), lambda i: (0, 1)),
                  full(p_gate), full(n_state), full(m_state), full(i_bias), full(f_bias)],
        out_specs=[pl.BlockSpec((n, wd), lambda i: (0, 0)) for wd in widths],
        out_shape=[jax.ShapeDtypeStruct((n, wd), F32) for wd in widths],
        compiler_params=_cp(("arbitrary",), 48),
        name="c_sample_prep",
    )(p_main, p_main, p_gate, n_state, m_state, i_bias, f_bias)


def _c_sample_post_body(num_ref, dn_ref, o_ref, nrm_ref, h_ref):
    for h in range(M_HEADS):
        vs = slice(h * M_DV, (h + 1) * M_DV)
        hc = num_ref[:, vs] / dn_ref[:, vs]
        h_ref[:, vs] = _bf(jax.nn.sigmoid(o_ref[:, vs]) * (_rms(hc) * nrm_ref[:, vs]))


def _c_sample_post(num, den, p_main, norm_g):
    n = DEC_BATCH

    def full(a):
        return pl.BlockSpec(a.shape, lambda i: (0,) * a.ndim)

    return pl.pallas_call(
        _c_sample_post_body,
        grid=(1,),
        in_specs=[full(num), full(den), pl.BlockSpec((n, M_V), lambda i: (0, 2)), full(norm_g)],
        out_specs=pl.BlockSpec((n, M_V), lambda i: (0, 0)),
        out_shape=jax.ShapeDtypeStruct((n, M_V), BF16),
        compiler_params=_cp(("arbitrary",), 48),
        name="c_sample_post",
    )(num, den, p_main, norm_g)


def _rope_tables(pos):
    half = RET_DK // 2
    inv = jnp.exp(-math.log(ROPE_BASE) * jnp.arange(half, dtype=F32) / half)
    ang = pos.astype(F32)[:, None] * inv
    return jnp.cos(ang), jnp.sin(ang)


def _pad_lanes(v, width=LANES):
    return jnp.pad(v.reshape(1, -1), ((0, 0), (0, width - v.size)))


def kernel(x_prompt, x_sample, cache_mem_k, cache_mem_v, state_conv, state_ssm, state_ret, state_mlstm_c, state_mlstm_n, state_mlstm_m, mem_prompt, norm_ffn1, w_ffn1_in, w_ffn1_out, norm_mix, w_in_ab, ssd_conv_w, ssd_conv_b, ssd_dt_bias, ssd_a_log, ssd_d, ssd_norm, w_out_ab, w_in_c, mlstm_i_bias, mlstm_f_bias, mlstm_norm, w_out_c, norm_xattn, norm_mem, w_xq, w_xkv, w_xo, norm_ffn2, w_ffn2_in, w_ffn2_out, norm_final):
    g3 = lambda g: g.reshape(DEPTH, 1, D_MODEL)
    n_ffn1, n_mix, n_x, n_mem, n_ffn2 = g3(norm_ffn1), g3(norm_mix), g3(norm_xattn), g3(norm_mem), g3(norm_ffn2)

    f1_in, f1_out, f2_in, f2_out = (w.astype(BF16) for w in (w_ffn1_in, w_ffn1_out, w_ffn2_in, w_ffn2_out))
    w_out_ab, w_out_c, w_xq, w_xkv, w_xo = (w.astype(BF16) for w in (w_out_ab, w_out_c, w_xq, w_xkv, w_xo))

    wz, wxbc, wdt, wrq, wrk, wrv, wrg = jnp.split(w_in_ab[0], np_cumsum(AB_SIZES), axis=1)
    w_ab_main = jnp.concatenate([wz, wrv, wrg, wxbc, wrq, wrk], axis=1).astype(BF16)[None]
    w_ab_dt = jnp.pad(wdt, ((0, 0), (0, LANES - SSD_HEADS))).astype(BF16)[None]
    wq, wk, wv, wi, wf, wo = jnp.split(w_in_c[0], np_cumsum(C_SIZES), axis=1)
    w_c_main = jnp.concatenate([wq, wk, wv, wo], axis=1).astype(BF16)[None]
    gpad = ((0, 0), (0, LANES - M_HEADS))
    w_c_gate = jnp.concatenate([jnp.pad(wi, gpad), jnp.pad(wf, gpad)], axis=1).astype(BF16)[None]

    conv_w = ssd_conv_w[0]
    conv_b = ssd_conv_b.reshape(1, SSD_CONV_CH)
    dt_bias = _pad_lanes(ssd_dt_bias[0])
    a_log = _pad_lanes(ssd_a_log[0])
    d_skip_e = jnp.repeat(ssd_d[0], SSD_HEAD_DIM).reshape(1, SSD_D_INNER)
    s_norm = ssd_norm.reshape(1, SSD_D_INNER)
    i_bias = _pad_lanes(mlstm_i_bias[0])
    f_bias = _pad_lanes(mlstm_f_bias[0])
    m_norm = mlstm_norm.reshape(1, M_V)

    memk, memv, memk4, memv4 = _mem_kv(mem_prompt.reshape(BATCH * MEM_LEN, D_MODEL), n_mem, w_xkv)
    cos_p, sin_p = _rope_tables(jnp.arange(SEQ))
    x = x_prompt.reshape(BATCH * SEQ, D_MODEL)

    cos_s, sin_s = _rope_tables(PAST_LEN + jnp.arange(1))
    xs_ = x_sample.reshape(DEC_BATCH, D_MODEL)
    xs_ = _ffn(xs_, n_ffn1, f1_in, f1_out, 0)
    sp_main, sp_dt = _norm_proj(xs_, n_mix, 0, w_ab_main, 0, 1024, w_ab_dt)
    conv_s, xs_c, xdt, eda, bm_s, cm_s, q_s, k_s, gam = _ab_sample_prep(
        sp_main, sp_dt, state_conv.reshape(DEC_BATCH, (SSD_CONV - 1) * SSD_CONV_CH), cos_s, sin_s,
        conv_w, conv_b, dt_bias, a_log)
    ret_s, r_s = _outer_state(gam, k_s, q_s, sp_main[:, SSD_D_INNER:SSD_D_INNER + RET_V], state_ret[0])

    x = _ffn(x, n_ffn1, f1_in, f1_out, 0)
    w_gates = w_ab_main[0, :, :AB_GATE_BLOCKS * AB_TN]
    w_xqk = w_ab_main[0, :, AB_GATE_BLOCKS * AB_TN:]
    p_gates, p_xqk, p_dt, ssm_s, y_s = _ab_inproj(
        x, n_mix, w_gates, w_xqk, w_ab_dt[0], dt_bias, conv_w, conv_b, cos_p, sin_p,
        (eda, xdt, bm_s, cm_s, state_ssm.reshape(DEC_BATCH, SSD_D_INNER, SSD_D_STATE)))

    ycat_s = _ab_sample_post(y_s, xs_c, sp_main, r_s, d_skip_e, s_norm)
    xs_ = _proj_residual(xs_, ycat_s, w_out_ab, 0)
    q_x0 = _norm_proj(xs_, n_x, 0, w_xq, 0, 1024).reshape(DEC_BATCH, X_HEADS, X_HEAD_DIM)

    ycat, ssm_p, ret_p = _mixer_ab_prompt(p_gates, p_xqk, p_dt, a_log, d_skip_e, s_norm)
    x_tail = x.reshape(BATCH, SEQ, D_MODEL)[:, SEQ - SSD_CONV:].reshape(BATCH * SSD_CONV, D_MODEL)
    conv_p = _norm_proj(x_tail, n_mix, 0, w_ab_main, 0, AB_TN, col0=AB_XBC0, n_out=SSD_CONV_CH)
    conv_p = conv_p.reshape(BATCH, SSD_CONV, SSD_CONV_CH)[:, 1:]
    x = _proj_residual(x, ycat, w_out_ab, 0)
    x = _xattn_prompt(x, n_x, w_xq, w_xo, memk, memv, 0)
    x, o_x0 = _ffn(x, n_ffn2, f2_in, f2_out, 0, host=(q_x0, cache_mem_k, cache_mem_v, 0))

    xs_ = _proj_residual(xs_, o_x0.reshape(DEC_BATCH, D_MODEL), w_xo, 0)
    xs_ = _ffn(xs_, n_ffn2, f2_in, f2_out, 0)
    xs_ = _ffn(xs_, n_ffn1, f1_in, f1_out, 1)
    sc_main, sc_gate = _norm_proj(xs_, n_mix, 1, w_c_main, 0, 1024, w_c_gate)
    m_in = jnp.pad(state_mlstm_m[0], ((0, 0), (0, LANES - M_HEADS)))
    dpe, kw, mn_s, mm_s, den = _c_sample_prep(sc_main, sc_gate, state_mlstm_n.reshape(DEC_BATCH, M_QK), m_in,
                                              i_bias, f_bias)
    mc_s, num = _outer_state(dpe, kw, sc_main[:, :M_QK], sc_main[:, 2 * M_QK:2 * M_QK + M_V], state_mlstm_c[0])
    hout_s = _c_sample_post(num, den, sc_main, m_norm)
    xs_ = _proj_residual(xs_, hout_s, w_out_c, 0)
    q_x1 = _norm_proj(xs_, n_x, 1, w_xq, 1, 1024).reshape(DEC_BATCH, X_HEADS, X_HEAD_DIM)

    x, o_x1 = _ffn(x, n_ffn1, f1_in, f1_out, 1, host=(q_x1, cache_mem_k, cache_mem_v, 1))
    xs_ = _proj_residual(xs_, o_x1.reshape(DEC_BATCH, D_MODEL), w_xo, 1)
    y_sample = _ffn(xs_, n_ffn2, f2_in, f2_out, 1, norm_final).reshape(DEC_BATCH, 1, D_MODEL)
    pc_main, pc_gate = _c_inproj(x, n_mix, 1, w_c_main, w_c_gate)
    hout, mc_p, mn_p, mm_p = _mixer_c_prompt(pc_main.reshape(BATCH, SEQ, -1), pc_gate.reshape(BATCH, SEQ, -1),
                                             i_bias, f_bias, m_norm)
    x = _proj_residual(x, hout.reshape(BATCH * SEQ, M_V), w_out_c, 0)
    x = _xattn_prompt(x, n_x, w_xq, w_xo, memk, memv, 1)
    y_prompt = _ffn(x, n_ffn2, f2_in, f2_out, 1, norm_final).reshape(BATCH, SEQ, D_MODEL)

    kv_shape = (DEPTH, BATCH, MEM_LEN, X_HEADS, X_HEAD_DIM)
    return (y_prompt, y_sample, memk4.reshape(kv_shape), memv4.reshape(kv_shape),
            conv_p.reshape(1, BATCH, SSD_CONV - 1, SSD_CONV_CH),
            conv_s.reshape(1, DEC_BATCH, SSD_CONV - 1, SSD_CONV_CH),
            ssm_p.reshape(1, BATCH, SSD_HEADS, SSD_HEAD_DIM, SSD_D_STATE),
            ssm_s.reshape(1, DEC_BATCH, SSD_HEADS, SSD_HEAD_DIM, SSD_D_STATE),
            ret_p[None], ret_s[None], mc_p[None], mc_s[None],
            mn_p[None], mn_s.reshape(1, DEC_BATCH, M_HEADS, M_DK),
            mm_p[:, :M_HEADS, 0][None], mm_s[:, :M_HEADS][None])


def np_cumsum(sizes):
    out, acc = [], 0
    for s in sizes[:-1]:
        acc += s
        out.append(acc)
    return out
```
